```python
import jax, jax.numpy as jnp
from jax import lax
import numpy as np

D_MODEL = 4096
BATCH = 2
SEQ = 8192
DEPTH = 4

CHUNK = 64
N_MIXERS = 4
EPS = 1e-6

CONV_WIDTH = 3
HGRN_DK = 128
HGRN_HEADS = D_MODEL // HGRN_DK
HGRN_DV = D_MODEL // HGRN_HEADS
GLA_HEADS = 8
GLA_KEY_DIM = D_MODEL // 2
GLA_VAL_DIM = D_MODEL
GLA_DK = GLA_KEY_DIM // GLA_HEADS
GLA_DV = GLA_VAL_DIM // GLA_HEADS
GLA_GATE_RANK = 16
GLA_GATE_TAU = 16.0
GLA_IN = 2 * GLA_KEY_DIM + 2 * GLA_VAL_DIM + GLA_GATE_RANK
GMLP_BLOCK = 128
GMLP_WIDTH = 2 * D_MODEL
GMLP_GROUPS = 8
MOE_GROUPS = 4
MOE_EXPERTS_PER_GROUP = 4
MOE_EXPERTS = MOE_GROUPS * MOE_EXPERTS_PER_GROUP
MOE_TOP_K = 2
MOE_FF = 512
PLE_DIM = 256

kernel_name = 'interleaved_hybrid_streaming_trunk'


def rms_norm(x, gain):
    x32 = x.astype(jnp.float32)
    y = x32 * lax.rsqrt(jnp.mean(x32 * x32, axis=-1, keepdims=True) + EPS)
    return (y * gain.astype(jnp.float32)).astype(x.dtype)


def head_rms_norm(o, gain):
    h, dv = o.shape[-2], o.shape[-1]
    return rms_norm(o, gain.reshape(h, dv))


def chunk_gated_linear_attention(q, k, v, log_a):
    b, s, h, dk = q.shape
    dv = v.shape[-1]
    n = s // CHUNK

    def blocks(t):
        return t.astype(jnp.float32).reshape(b, n, CHUNK, h, t.shape[-1]).transpose(1, 0, 3, 2, 4)

    qc, kc, vc = blocks(q), blocks(k), blocks(v)
    gc = jnp.cumsum(blocks(log_a), axis=3)
    causal = jnp.tril(jnp.ones((CHUNK, CHUNK), dtype=bool))[:, :, None]

    def step(state, inp):
        qi, ki, vi, gi = inp
        rel = jnp.where(causal, gi[:, :, :, None, :] - gi[:, :, None, :, :], -jnp.inf)
        scores = jnp.sum(qi[:, :, :, None, :] * ki[:, :, None, :, :] * jnp.exp(rel), axis=-1)
        o = (jnp.einsum('bhij,bhjv->bhiv', scores, vi)
             + jnp.einsum('bhik,bhkv->bhiv', qi * jnp.exp(gi), state))
        g_last = gi[:, :, -1, :]
        k_to_end = ki * jnp.exp(g_last[:, :, None, :] - gi)
        state = jnp.exp(g_last)[..., None] * state + jnp.einsum('bhjk,bhjv->bhkv', k_to_end, vi)
        return state, o

    s0 = jnp.zeros((b, h, dk, dv), jnp.float32)
    _, o = lax.scan(step, s0, (qc, kc, vc, gc))
    return o.transpose(1, 0, 3, 2, 4).reshape(b, s, h, dv).astype(v.dtype)


def short_conv_mixer(x, w_in, w_conv, w_out):
    s = x.shape[1]
    gate_b, gate_c, u = jnp.split(x @ w_in, 3, axis=-1)
    z = gate_c * u
    zp = jnp.pad(z, ((0, 0), (CONV_WIDTH - 1, 0), (0, 0)))
    conv = w_conv[0] * zp[:, 0:s, :]
    for j in range(1, CONV_WIDTH):
        conv = conv + w_conv[j] * zp[:, j:j + s, :]
    return (gate_b * conv) @ w_out


def hgrn2_mixer(x, w_in, lower_bound, gain, w_out):
    b, s, _ = x.shape
    q, f, i, g = jnp.split(x @ w_in, 4, axis=-1)
    f32 = f.astype(jnp.float32)
    lb = lower_bound.astype(jnp.float32)
    log_forget = jnp.logaddexp(jnp.log(lb), jnp.log1p(-lb) + jax.nn.log_sigmoid(f32))
    key = (1.0 - lb) * jax.nn.sigmoid(-f32)
    shp = (b, s, HGRN_HEADS, HGRN_DK)
    o = chunk_gated_linear_attention(jax.nn.silu(q).reshape(shp), key.reshape(shp),
                                     i.reshape(b, s, HGRN_HEADS, HGRN_DV), log_forget.reshape(shp))
    o = head_rms_norm(o, gain).reshape(b, s, HGRN_HEADS * HGRN_DV) * jax.nn.silu(g)
    return o @ w_out


def gla_mixer(x, w_in, w_a2, b_a, gain, w_out):
    b, s, _ = x.shape
    splits = [GLA_KEY_DIM, 2 * GLA_KEY_DIM, 2 * GLA_KEY_DIM + GLA_VAL_DIM, 2 * GLA_KEY_DIM + 2 * GLA_VAL_DIM]
    q, k, v, r, a_low = jnp.split(x @ w_in, splits, axis=-1)
    log_a = jax.nn.log_sigmoid((a_low @ w_a2 + b_a).astype(jnp.float32)) / GLA_GATE_TAU
    kshp = (b, s, GLA_HEADS, GLA_DK)
    o = chunk_gated_linear_attention((q * GLA_DK ** -0.5).reshape(kshp), k.reshape(kshp),
                                     v.reshape(b, s, GLA_HEADS, GLA_DV), log_a.reshape(kshp))
    o = head_rms_norm(o, gain).reshape(b, s, GLA_VAL_DIM) * jax.nn.silu(r)
    return o @ w_out


def gmlp_mixer(x, w_in, ln_g, ln_b, w_s, b_s, w_out):
    b, s, _ = x.shape
    u, v = jnp.split(jax.nn.gelu(x @ w_in), 2, axis=-1)
    v32 = v.astype(jnp.float32)
    mu = jnp.mean(v32, axis=-1, keepdims=True)
    var = jnp.mean(jnp.square(v32 - mu), axis=-1, keepdims=True)
    v = ((v32 - mu) * lax.rsqrt(var + EPS) * ln_g + ln_b).astype(x.dtype)
    n = s // GMLP_BLOCK
    vb = v.reshape(b, n, GMLP_BLOCK, GMLP_GROUPS, GMLP_WIDTH // GMLP_GROUPS)
    w_causal = w_s * jnp.tril(jnp.ones((GMLP_BLOCK, GMLP_BLOCK), w_s.dtype))
    mixed = jnp.einsum('gij,bnjgc->bnigc', w_causal, vb) + b_s.T[None, None, :, :, None]
    return (u * mixed.reshape(b, s, GMLP_WIDTH)) @ w_out


def hier_moe(x, w_group, w_expert, w_gate, w_up, w_down):
    b, s, d = x.shape
    h = x.reshape(-1, d)
    group_prob = jax.nn.softmax((h @ w_group).astype(jnp.float32), axis=-1)
    group_p, group_idx = lax.top_k(group_prob, 1)
    expert_logits = (h @ w_expert).astype(jnp.float32).reshape(-1, MOE_GROUPS, MOE_EXPERTS_PER_GROUP)
    in_group = jnp.take_along_axis(expert_logits, group_idx[:, :, None], axis=1)[:, 0]
    top_logits, top_local = lax.top_k(in_group, MOE_TOP_K)
    weights = jax.nn.softmax(top_logits, axis=-1) * group_p
    expert_idx = group_idx * MOE_EXPERTS_PER_GROUP + top_local
    combine = jnp.sum(jax.nn.one_hot(expert_idx, MOE_EXPERTS, dtype=jnp.float32) * weights[..., None], axis=1)
    hidden = jax.nn.silu(jnp.einsum('td,edf->tef', h, w_gate)) * jnp.einsum('td,edf->tef', h, w_up)
    y = jnp.einsum('tef,efd->td', hidden * combine[:, :, None].astype(h.dtype), w_down)
    return y.reshape(b, s, d)


def _uses(kind):
    return len(range(kind, DEPTH, N_MIXERS))


def setup_inputs(seed: int = 0) -> dict:
    key = jax.random.key(seed)
    ks = iter(jax.random.split(key, 40))

    def nrm(shape, scale):
        return jax.random.normal(next(ks), shape, jnp.float32) * scale

    def gain(shape):
        return 1.0 + nrm(shape, 0.02)

    nA, nB, nC, nD = _uses(0), _uses(1), _uses(2), _uses(3)
    d = D_MODEL
    return {
        'x': nrm((BATCH, SEQ, d), 1.0),
        'p': nrm((DEPTH, BATCH, SEQ, PLE_DIM), 1.0),
        'norm_mix': gain((DEPTH, d)),
        'norm_ffn': gain((DEPTH, d)),
        'norm_final': gain((d,)),
        'conv_w_in': nrm((nA, d, 3 * d), d ** -0.5),
        'conv_w': nrm((nA, CONV_WIDTH, d), CONV_WIDTH ** -0.5),
        'conv_w_out': nrm((nA, d, d), d ** -0.5),
        'hgrn_w_in': nrm((nB, d, 4 * d), d ** -0.5),
        'hgrn_lb': nrm((DEPTH, HGRN_HEADS * HGRN_DK), 0.5),
        'hgrn_norm': gain((nB, HGRN_HEADS * HGRN_DV)),
        'hgrn_w_out': nrm((nB, HGRN_HEADS * HGRN_DV, d), d ** -0.5),
        'gla_w_in': nrm((nC, d, GLA_IN), d ** -0.5),
        'gla_w_a2': nrm((nC, GLA_GATE_RANK, GLA_KEY_DIM), GLA_GATE_RANK ** -0.5),
        'gla_b_a': nrm((nC, GLA_KEY_DIM), 0.01),
        'gla_norm': gain((nC, GLA_VAL_DIM)),
        'gla_w_out': nrm((nC, GLA_VAL_DIM, d), GLA_VAL_DIM ** -0.5),
        'gmlp_w_in': nrm((nD, d, 2 * GMLP_WIDTH), d ** -0.5),
        'gmlp_ln_g': gain((nD, GMLP_WIDTH)),
        'gmlp_ln_b': nrm((nD, GMLP_WIDTH), 0.01),
        'gmlp_w_s': nrm((nD, GMLP_GROUPS, GMLP_BLOCK, GMLP_BLOCK), GMLP_BLOCK ** -0.5),
        'gmlp_b_s': 1.0 + nrm((nD, GMLP_GROUPS, GMLP_BLOCK), 0.1),
        'gmlp_w_out': nrm((nD, GMLP_WIDTH, d), GMLP_WIDTH ** -0.5),
        'moe_w_group': nrm((DEPTH, d, MOE_GROUPS), d ** -0.5),
        'moe_w_expert': nrm((DEPTH, d, MOE_EXPERTS), d ** -0.5),
        'moe_w_gate': nrm((DEPTH, MOE_EXPERTS, d, MOE_FF), d ** -0.5),
        'moe_w_up': nrm((DEPTH, MOE_EXPERTS, d, MOE_FF), d ** -0.5),
        'moe_w_down': nrm((DEPTH, MOE_EXPERTS, MOE_FF, d), MOE_FF ** -0.5),
        'ple_w_proj': nrm((DEPTH, PLE_DIM, d), PLE_DIM ** -0.5),
        'ple_w_gate': nrm((DEPTH, d, d), d ** -0.5),
    }


def reference(x, p, norm_mix, norm_ffn, norm_final, conv_w_in, conv_w, conv_w_out,
              hgrn_w_in, hgrn_lb, hgrn_norm, hgrn_w_out,
              gla_w_in, gla_w_a2, gla_b_a, gla_norm, gla_w_out,
              gmlp_w_in, gmlp_ln_g, gmlp_ln_b, gmlp_w_s, gmlp_b_s, gmlp_w_out,
              moe_w_group, moe_w_expert, moe_w_gate, moe_w_up, moe_w_down,
              ple_w_proj, ple_w_gate):
    lb_w = jax.nn.softmax(hgrn_lb.astype(jnp.float32), axis=0)
    lower_bounds = jnp.cumsum(lb_w, axis=0) - lb_w[0]
    h = x
    for layer in range(DEPTH):
        kind, j = layer % N_MIXERS, layer // N_MIXERS
        a = rms_norm(h, norm_mix[layer])
        if kind == 0:
            mix = short_conv_mixer(a, conv_w_in[j], conv_w[j], conv_w_out[j])
        elif kind == 1:
            mix = hgrn2_mixer(a, hgrn_w_in[j], lower_bounds[layer], hgrn_norm[j], hgrn_w_out[j])
        elif kind == 2:
            mix = gla_mixer(a, gla_w_in[j], gla_w_a2[j], gla_b_a[j], gla_norm[j], gla_w_out[j])
        else:
            mix = gmlp_mixer(a, gmlp_w_in[j], gmlp_ln_g[j], gmlp_ln_b[j], gmlp_w_s[j], gmlp_b_s[j], gmlp_w_out[j])
        h = h + mix
        c = rms_norm(h, norm_ffn[layer])
        h = h + hier_moe(c, moe_w_group[layer], moe_w_expert[layer], moe_w_gate[layer],
                         moe_w_up[layer], moe_w_down[layer])
        h = h + (p[layer] @ ple_w_proj[layer]) * jax.nn.sigmoid(c @ ple_w_gate[layer])
    return rms_norm(h, norm_final)
```

```python
import functools

import numpy as np
import jax
import jax.numpy as jnp
from jax import lax
from jax.experimental import pallas as pl
from jax.experimental.pallas import tpu as pltpu

F32 = jnp.float32
BF16 = jnp.bfloat16

EPS = 1e-6
CHUNK = 64
N_LEVELS = 6
LANES = 128
VMEM_LIMIT = 56 * 1024 * 1024

CONV_WIDTH = 3
HGRN_DK = 128
GLA_HEADS = 8
GLA_GATE_RANK = 16
GLA_GATE_TAU = 16.0
GMLP_BLOCK = 128
GMLP_GROUPS = 8
MOE_GROUPS = 4
MOE_EPG = 4
MOE_TILE = 256


def _cparams(sem):
    return pltpu.CompilerParams(dimension_semantics=sem, vmem_limit_bytes=VMEM_LIMIT)


def _sigmoid(x):
    return jax.nn.sigmoid(x)


def _silu(x):
    return x * jax.nn.sigmoid(x)


def _mm_kernel(*refs, n_w, n_e, epilogue):
    a_ref = refs[0]
    w_refs = refs[1:1 + n_w]
    e_refs = refs[1 + n_w:1 + n_w + n_e]
    o_refs = refs[1 + n_w + n_e:]
    a = a_ref[...]
    accs = [jnp.dot(a, w[...], preferred_element_type=F32) for w in w_refs]
    res = epilogue(accs, [e[...] for e in e_refs])
    for o, r in zip(o_refs, res):
        o[...] = r.astype(o.dtype)


def _mm(a, ws, extras, out_dtypes, epilogue, *, n_cols, tm, tn):
    m, k = a.shape
    grid = (m // tm, n_cols // tn)
    in_specs = [pl.BlockSpec((tm, k), lambda i, n: (i, 0))]
    args = [a]
    for w, off in ws:
        in_specs.append(pl.BlockSpec((k, tn), lambda i, n, off=off: (0, n + off)))
        args.append(w)
    for arr, kind in extras:
        if kind == 'row':
            in_specs.append(pl.BlockSpec((1, tn), lambda i, n: (0, n)))
        elif kind == 'tile':
            in_specs.append(pl.BlockSpec((tm, tn), lambda i, n: (i, n)))
        elif kind == 'rowtile':
            in_specs.append(pl.BlockSpec((tm, arr.shape[1]), lambda i, n: (i, 0)))
        elif kind == 'kcol':
            in_specs.append(pl.BlockSpec((arr.shape[0], tn), lambda i, n: (0, n)))
        else:
            raise ValueError(kind)
        args.append(arr)
    out_shape = [jax.ShapeDtypeStruct((m, n_cols), dt) for dt in out_dtypes]
    out_specs = [pl.BlockSpec((tm, tn), lambda i, n: (i, n)) for _ in out_dtypes]
    res = pl.pallas_call(
        functools.partial(_mm_kernel, n_w=len(ws), n_e=len(extras), epilogue=epilogue),
        grid=grid, in_specs=in_specs, out_specs=out_specs, out_shape=out_shape,
        compiler_params=_cparams(("parallel", "arbitrary")),
    )(*args)
    return res


def _rms_rows(x, gain):
    ms = jnp.mean(x * x, axis=-1, keepdims=True)
    return x * lax.rsqrt(ms + EPS) * gain


def _rms_kernel(h_ref, g_ref, o_ref):
    o_ref[...] = _rms_rows(h_ref[...], g_ref[...]).astype(o_ref.dtype)


def _rmsnorm(h, gain, out_dtype, tm=256):
    t, d = h.shape
    return pl.pallas_call(
        _rms_kernel, grid=(t // tm,),
        in_specs=[pl.BlockSpec((tm, d), lambda i: (i, 0)), pl.BlockSpec((1, d), lambda i: (0, 0))],
        out_specs=pl.BlockSpec((tm, d), lambda i: (i, 0)),
        out_shape=jax.ShapeDtypeStruct((t, d), out_dtype),
        compiler_params=_cparams(("parallel",)),
    )(h, gain.reshape(1, d))


def _first_max_index(vals, m):
    idx = jnp.full(m.shape, len(vals) - 1, jnp.int32)
    for j in range(len(vals) - 2, -1, -1):
        idx = jnp.where(vals[j] == m, j, idx)
    return idx


def _route_cols(logits):
    col = lambda j: logits[:, j:j + 1]
    gl = [col(j) for j in range(MOE_GROUPS)]
    gm = functools.reduce(jnp.maximum, gl)
    gsum = functools.reduce(lambda a, b: a + b, [jnp.exp(g - gm) for g in gl])
    group_p = 1.0 / gsum
    gidx = _first_max_index(gl, gm)
    ig = []
    for e in range(MOE_EPG):
        v = col(MOE_GROUPS + (MOE_GROUPS - 1) * MOE_EPG + e)
        for g in range(MOE_GROUPS - 2, -1, -1):
            v = jnp.where(gidx == g, col(MOE_GROUPS + g * MOE_EPG + e), v)
        ig.append(v)
    m1 = functools.reduce(jnp.maximum, ig)
    i1 = _first_max_index(ig, m1)
    ig2 = [jnp.where(i1 == e, -jnp.inf, ig[e]) for e in range(MOE_EPG)]
    m2 = functools.reduce(jnp.maximum, ig2)
    i2 = _first_max_index(ig2, m2)
    t = jnp.exp(m2 - m1)
    w1 = 1.0 / (1.0 + t) * group_p
    w2 = t / (1.0 + t) * group_p
    lane = lax.broadcasted_iota(jnp.int32, logits.shape, 1)
    out = jnp.where(lane == MOE_EPG, gidx.astype(F32), 0.0)
    for e in range(MOE_EPG):
        ce = jnp.where(i1 == e, w1, jnp.where(i2 == e, w2, 0.0))
        out = jnp.where(lane == e, ce, out)
    return out


def _rms_router_kernel(h_ref, g_ref, wr_ref, c_ref, c32_ref, route_ref):
    y = _rms_rows(h_ref[...], g_ref[...])
    c_ref[...] = y.astype(BF16)
    c32_ref[...] = y
    logits = jnp.dot(y, wr_ref[...], preferred_element_type=F32, precision=lax.Precision.HIGHEST)
    route_ref[...] = _route_cols(logits)


def _rms_router(h, gain, w_router, tm=256):
    t, d = h.shape
    return pl.pallas_call(
        _rms_router_kernel, grid=(t // tm,),
        in_specs=[pl.BlockSpec((tm, d), lambda i: (i, 0)),
                  pl.BlockSpec((1, d), lambda i: (0, 0)),
                  pl.BlockSpec((d, LANES), lambda i: (0, 0))],
        out_specs=[pl.BlockSpec((tm, d), lambda i: (i, 0)),
                   pl.BlockSpec((tm, d), lambda i: (i, 0)),
                   pl.BlockSpec((tm, LANES), lambda i: (i, 0))],
        out_shape=[jax.ShapeDtypeStruct((t, d), BF16),
                   jax.ShapeDtypeStruct((t, d), F32),
                   jax.ShapeDtypeStruct((t, LANES), F32)],
        compiler_params=_cparams(("parallel",)),
    )(h, gain.reshape(1, d), w_router)


def _moe_up_kernel(tg_ref, nu_ref, x_ref, comb_ref, wg_ref, wu_ref, o_ref):
    e = pl.program_id(0)
    i = pl.program_id(1)

    @pl.when(i < nu_ref[0])
    def _():
        x = x_ref[...]
        g = jnp.dot(x, wg_ref[...], preferred_element_type=F32)
        u = jnp.dot(x, wu_ref[...], preferred_element_type=F32)
        comb = comb_ref[...]
        lane = lax.broadcasted_iota(jnp.int32, comb.shape, 1)
        sc = jnp.sum(jnp.where(lane == e, comb, 0.0), axis=1, keepdims=True)
        o_ref[...] = (_silu(g) * u * sc).astype(o_ref.dtype)


def _moe_down_kernel(tg_ref, nu_ref, hid_ref, wd_ref, o_ref):
    i = pl.program_id(1)

    @pl.when(i < nu_ref[0])
    def _():
        o_ref[...] = jnp.dot(hid_ref[...], wd_ref[...], preferred_element_type=F32)


def _moe_sorted(x_sorted, comb_sorted, tile_group, n_used, w_gate, w_up, w_down):
    tp, d = x_sorted.shape
    ff = w_gate.shape[-1]
    tm = MOE_TILE
    nt = tp // tm
    row = lambda i, nu: jnp.minimum(i, nu[0] - 1)
    hidden = pl.pallas_call(
        _moe_up_kernel,
        grid_spec=pltpu.PrefetchScalarGridSpec(
            num_scalar_prefetch=2, grid=(MOE_EPG, nt),
            in_specs=[
                pl.BlockSpec((tm, d), lambda e, i, tg, nu: (row(i, nu), 0)),
                pl.BlockSpec((tm, LANES), lambda e, i, tg, nu: (row(i, nu), 0)),
                pl.BlockSpec((None, d, ff), lambda e, i, tg, nu: (tg[i] * MOE_EPG + e, 0, 0)),
                pl.BlockSpec((None, d, ff), lambda e, i, tg, nu: (tg[i] * MOE_EPG + e, 0, 0)),
            ],
            out_specs=pl.BlockSpec((tm, ff), lambda e, i, tg, nu: (row(i, nu), e)),
        ),
        out_shape=jax.ShapeDtypeStruct((tp, MOE_EPG * ff), BF16),
        compiler_params=_cparams(("arbitrary", "arbitrary")),
    )(tile_group, n_used, x_sorted, comb_sorted, w_gate, w_up)
    tn = d // 2
    y = pl.pallas_call(
        _moe_down_kernel,
        grid_spec=pltpu.PrefetchScalarGridSpec(
            num_scalar_prefetch=2, grid=(d // tn, nt),
            in_specs=[
                pl.BlockSpec((tm, MOE_EPG * ff), lambda n, i, tg, nu: (row(i, nu), 0)),
                pl.BlockSpec((None, MOE_EPG * ff, tn), lambda n, i, tg, nu: (tg[i], 0, n)),
            ],
            out_specs=pl.BlockSpec((tm, tn), lambda n, i, tg, nu: (row(i, nu), n)),
        ),
        out_shape=jax.ShapeDtypeStruct((tp, d), F32),
        compiler_params=_cparams(("arbitrary", "arbitrary")),
    )(tile_group, n_used, hidden, w_down)
    return y


def _moe_layer(c32, route, w_gate, w_up, w_down):
    t, d = c32.shape
    tm = MOE_TILE
    nt = t // tm + MOE_GROUPS
    gidx = route[:, MOE_EPG].astype(jnp.int32)
    onehot = (gidx[:, None] == jnp.arange(MOE_GROUPS, dtype=jnp.int32)[None, :]).astype(jnp.int32)
    csum = jnp.cumsum(onehot, axis=0)
    rank = jnp.sum(csum * onehot, axis=1) - 1
    counts = csum[-1]
    ntiles = (counts + tm - 1) // tm
    tile_end = jnp.cumsum(ntiles)
    tile_start = tile_end - ntiles
    pos = tile_start[gidx] * tm + rank
    n_used = tile_end[-1:].astype(jnp.int32)
    tile_group = jnp.minimum(
        jnp.sum((jnp.arange(nt, dtype=jnp.int32)[:, None] >= tile_end[None, :]).astype(jnp.int32), axis=1),
        MOE_GROUPS - 1).astype(jnp.int32)
    src = jnp.zeros((nt * tm,), jnp.int32).at[pos].set(jnp.arange(t, dtype=jnp.int32))
    x_sorted = jnp.take(c32, src, axis=0).astype(BF16)
    comb_sorted = jnp.take(route, src, axis=0)
    y_sorted = _moe_sorted(x_sorted, comb_sorted, tile_group, n_used, w_gate, w_up, w_down)
    return jnp.take(y_sorted, pos, axis=0)


def _conv_in_kernel(a_ref, wb_ref, wc_ref, wu_ref, cw_ref, o_ref, carry_ref, *, tm, seq):
    i = pl.program_id(0)
    n = pl.program_id(1)
    a = a_ref[...]
    gate_b = jnp.dot(a, wb_ref[...], preferred_element_type=F32)
    gate_c = jnp.dot(a, wc_ref[...], preferred_element_type=F32)
    u = jnp.dot(a, wu_ref[...], preferred_element_type=F32)
    z = gate_c * u
    @pl.when((i * tm) % seq == 0)
    def _():
        carry_ref[n] = jnp.zeros(carry_ref.shape[1:], F32)

    prev = carry_ref[n]
    rid = lax.broadcasted_iota(jnp.int32, z.shape, 0)
    z1 = jnp.where(rid == 0, prev[7:8], pltpu.roll(z, 1, 0))
    z2 = jnp.where(rid == 0, prev[6:7], jnp.where(rid == 1, prev[7:8], pltpu.roll(z, 2, 0)))
    cw = cw_ref[...]
    conv = cw[0:1] * z2 + cw[1:2] * z1 + cw[2:3] * z
    o_ref[...] = (gate_b * conv).astype(o_ref.dtype)
    carry_ref[n] = z[tm - 8:tm]


def _conv_in(a, w_in, conv_w, seq, tm=1024, tn=256):
    t, d = a.shape
    nb = d // tn
    return pl.pallas_call(
        functools.partial(_conv_in_kernel, tm=tm, seq=seq),
        grid=(t // tm, nb),
        in_specs=[pl.BlockSpec((tm, d), lambda i, n: (i, 0)),
                  pl.BlockSpec((d, tn), lambda i, n: (0, n)),
                  pl.BlockSpec((d, tn), lambda i, n: (0, n + nb)),
                  pl.BlockSpec((d, tn), lambda i, n: (0, n + 2 * nb)),
                  pl.BlockSpec((CONV_WIDTH, tn), lambda i, n: (0, n))],
        out_specs=pl.BlockSpec((tm, tn), lambda i, n: (i, n)),
        out_shape=jax.ShapeDtypeStruct((t, d), BF16),
        scratch_shapes=[pltpu.VMEM((nb, 8, tn), F32)],
        compiler_params=_cparams(("arbitrary", "arbitrary")),
    )(a, w_in, w_in, w_in, conv_w)


def _build_decay_mats():
    t = np.arange(CHUNK)
    i = t[:, None]
    tt = t[None, :]
    mats = [tt <= i, tt > i]
    for l in range(N_LEVELS):
        s = 1 << l
        ref = ((t // (2 * s)) * (2 * s) + s - 1)[:, None]
        upper = ((t % (2 * s)) >= s)[:, None]
        mats.append(np.where(upper, (tt > ref) & (tt <= i), (tt > i) & (tt <= ref)))
    return np.concatenate(mats, axis=0).astype(np.float32)


def _build_level_matrix():
    lv = np.full((CHUNK, CHUNK), -1, np.int32)
    for i in range(CHUNK):
        lv[i, i] = N_LEVELS
        for j in range(i):
            lv[i, j] = int(np.floor(np.log2(i ^ j)))
    return lv


_DECAY_MATS = _build_decay_mats()
_LEVEL_MAT = _build_level_matrix()

_NT = (((1,), (1,)), ((), ()))
_TN = (((0,), (0,)), ((), ()))


def _rec_kernel(q_ref, k_ref, la_ref, v_ref, sg_ref, gain_ref, mall_ref, lvl_ref, o_ref, st_ref,
                *, nh, dk, dv, ts, seq):
    t = pl.program_id(1)

    @pl.when((t * ts) % seq == 0)
    def _():
        st_ref[...] = jnp.zeros_like(st_ref)

    mall = mall_ref[...]
    lvl = lvl_ref[...]
    gain = gain_ref[...]

    def chunk(c, carry):
        r0 = pl.multiple_of(c * CHUNK, CHUNK)
        rows = pl.ds(r0, CHUNK)
        for h in range(nh):
            kc = slice(h * dk, (h + 1) * dk)
            vc = slice(h * dv, (h + 1) * dv)
            q = q_ref[rows, kc].astype(F32)
            k = k_ref[rows, kc].astype(F32)
            la = la_ref[rows, kc]
            la_hi = la.astype(BF16)
            la_lo = (la - la_hi.astype(F32)).astype(BF16)
            dall = (jnp.dot(mall, la_hi, preferred_element_type=F32)
                    + jnp.dot(mall, la_lo, preferred_element_type=F32))
            g = dall[0:CHUNK]
            suffix = dall[CHUNK:2 * CHUNK]
            scores = jnp.where(
                lvl == N_LEVELS,
                lax.dot_general(q.astype(BF16), k.astype(BF16), _NT, preferred_element_type=F32), 0.0)
            for l in range(N_LEVELS):
                e = jnp.exp(dall[(2 + l) * CHUNK:(3 + l) * CHUNK])
                r = lax.dot_general((q * e).astype(BF16), (k * e).astype(BF16), _NT,
                                    preferred_element_type=F32)
                scores = jnp.where(lvl == l, r, scores)
            v = v_ref[rows, vc]
            state_t = st_ref[h]
            o = (jnp.dot(scores.astype(BF16), v, preferred_element_type=F32)
                 + lax.dot_general((q * jnp.exp(g)).astype(BF16), state_t.astype(BF16), _NT,
                                   preferred_element_type=F32))
            k_end = (k * jnp.exp(suffix)).astype(BF16)
            st_ref[h] = (state_t * jnp.exp(g[CHUNK - 1:CHUNK])
                         + lax.dot_general(v, k_end, _TN, preferred_element_type=F32))
            on = _rms_rows(o, gain[:, vc]) * sg_ref[rows, vc].astype(F32)
            o_ref[rows, vc] = on.astype(o_ref.dtype)
        return carry

    lax.fori_loop(0, ts // CHUNK, chunk, 0)


def _recurrence(q, k, la, v, sg, gain, *, dk, dv, nh, seq, ts=1024):
    t = q.shape[0]
    heads = q.shape[1] // dk
    ts = min(ts, seq)
    mall = jnp.asarray(_DECAY_MATS, BF16)
    lvl = jnp.asarray(_LEVEL_MAT)
    kspec = pl.BlockSpec((ts, nh * dk), lambda hg, s: (s, hg))
    vspec = pl.BlockSpec((ts, nh * dv), lambda hg, s: (s, hg))
    return pl.pallas_call(
        functools.partial(_rec_kernel, nh=nh, dk=dk, dv=dv, ts=ts, seq=seq),
        grid=(heads // nh, t // ts),
        in_specs=[kspec, kspec, kspec, vspec, vspec,
                  pl.BlockSpec((1, nh * dv), lambda hg, s: (0, hg)),
                  pl.BlockSpec(mall.shape, lambda hg, s: (0, 0)),
                  pl.BlockSpec(lvl.shape, lambda hg, s: (0, 0))],
        out_specs=vspec,
        out_shape=jax.ShapeDtypeStruct((t, heads * dv), BF16),
        scratch_shapes=[pltpu.VMEM((nh, dv, dk), F32)],
        compiler_params=_cparams(("parallel", "arbitrary")),
    )(q, k, la, v, sg, gain.reshape(1, -1), mall, lvl)


def _gelu_tanh(x):
    return jax.nn.gelu(x, approximate=True)


def _gmlp_spatial_kernel(u_ref, v_ref, lng_ref, lnb_ref, ws_ref, bs_ref, o_ref, *, groups):
    v = v_ref[...].astype(F32)
    mu = jnp.mean(v, axis=-1, keepdims=True)
    vc = v - mu
    var = jnp.mean(vc * vc, axis=-1, keepdims=True)
    vn = (vc * lax.rsqrt(var + EPS) * lng_ref[...] + lnb_ref[...]).astype(BF16)
    blk = v.shape[0]
    gw = v.shape[1] // groups
    ri = lax.broadcasted_iota(jnp.int32, (blk, blk), 0)
    ci = lax.broadcasted_iota(jnp.int32, (blk, blk), 1)
    for g in range(groups):
        cols = slice(g * gw, (g + 1) * gw)
        wc = jnp.where(ci <= ri, ws_ref[g], 0.0).astype(BF16)
        mixed = jnp.dot(wc, vn[:, cols], preferred_element_type=F32) + bs_ref[g]
        o_ref[:, cols] = (u_ref[:, cols].astype(F32) * mixed).astype(o_ref.dtype)


def _gmlp_spatial(u, v, ln_g, ln_b, w_s, b_s):
    t, w = u.shape
    groups, blk = w_s.shape[0], w_s.shape[1]
    rows = pl.BlockSpec((blk, w), lambda i: (i, 0))
    return pl.pallas_call(
        functools.partial(_gmlp_spatial_kernel, groups=groups),
        grid=(t // blk,),
        in_specs=[rows, rows,
                  pl.BlockSpec((1, w), lambda i: (0, 0)),
                  pl.BlockSpec((1, w), lambda i: (0, 0)),
                  pl.BlockSpec((groups, blk, blk), lambda i: (0, 0, 0)),
                  pl.BlockSpec((groups, blk, 1), lambda i: (0, 0, 0))],
        out_specs=rows,
        out_shape=jax.ShapeDtypeStruct((t, w), BF16),
        compiler_params=_cparams(("parallel",)),
    )(u, v, ln_g.reshape(1, w), ln_b.reshape(1, w), w_s, b_s.reshape(groups, blk, 1))


def _epi_residual(accs, extras):
    return [extras[0] + accs[0]]


def _epi_hgrn(accs, extras):
    q, f, i, g = accs
    lb = extras[0]
    e = jnp.exp(-jnp.abs(f))
    log_sig = jnp.minimum(f, 0.0) - jnp.log1p(e)
    a = jnp.log(lb)
    b = jnp.log1p(-lb) + log_sig
    log_forget = jnp.maximum(a, b) + jnp.log1p(jnp.exp(-jnp.abs(a - b)))
    key = (1.0 - lb) * (jnp.where(f >= 0.0, e, 1.0) / (1.0 + e))
    return [_silu(q), key, log_forget, i, _silu(g)]


def _epi_gelu2(accs, extras):
    return [_gelu_tanh(accs[0]), _gelu_tanh(accs[1])]


def _epi_scale(scale):
    return lambda accs, extras: [accs[0] * scale]


def _epi_identity(accs, extras):
    return [accs[0]]


def _epi_silu(accs, extras):
    return [_silu(accs[0])]


def _epi_gla_gate(accs, extras):
    z = accs[0] + extras[0]
    log_sig = jnp.minimum(z, 0.0) - jnp.log1p(jnp.exp(-jnp.abs(z)))
    return [log_sig / GLA_GATE_TAU]


def _epi_ple(accs, extras):
    h, y, p, proj = extras
    emb = jnp.dot(p.astype(BF16), proj, preferred_element_type=F32)
    return [(h + y) + emb * _sigmoid(accs[0])]


def kernel(x, p, norm_mix, norm_ffn, norm_final, conv_w_in, conv_w, conv_w_out, hgrn_w_in, hgrn_lb, hgrn_norm, hgrn_w_out, gla_w_in, gla_w_a2, gla_b_a, gla_norm, gla_w_out, gmlp_w_in, gmlp_ln_g, gmlp_ln_b, gmlp_w_s, gmlp_b_s, gmlp_w_out, moe_w_group, moe_w_expert, moe_w_gate, moe_w_up, moe_w_down, ple_w_proj, ple_w_gate):
    batch, seq, d = x.shape
    depth = p.shape[0]
    t = batch * seq
    bf = lambda w: w.astype(BF16)

    lb_w = jax.nn.softmax(hgrn_lb.astype(F32), axis=0)
    lower_bounds = jnp.cumsum(lb_w, axis=0) - lb_w[0]

    h = x.reshape(t, d)
    for layer in range(depth):
        kind, j = layer % 4, layer // 4
        a = _rmsnorm(h, norm_mix[layer], BF16)
        if kind == 0:
            gated = _conv_in(a, bf(conv_w_in[j]), conv_w[j], seq)
            w_out = bf(conv_w_out[j])
        elif kind == 1:
            nb = d // 256
            w = bf(hgrn_w_in[j])
            qt, key, log_f, val, sg = _mm(
                a, [(w, o * nb) for o in range(4)],
                [(lower_bounds[layer].reshape(1, d), 'row')],
                [BF16, BF16, F32, BF16, BF16], _epi_hgrn, n_cols=d, tm=1024, tn=256)
            gated = _recurrence(qt, key, log_f, val, sg, hgrn_norm[j],
                                dk=HGRN_DK, dv=HGRN_DK, nh=2, seq=seq)
            w_out = bf(hgrn_w_out[j])
        elif kind == 2:
            kd = d // 2
            w = gla_w_in[j]
            dk = kd // GLA_HEADS
            dv = d // GLA_HEADS
            (q,) = _mm(a, [(bf(w[:, :kd]), 0)], [], [BF16], _epi_scale(dk ** -0.5),
                       n_cols=kd, tm=1024, tn=512)
            (k,) = _mm(a, [(bf(w[:, kd:2 * kd]), 0)], [], [BF16], _epi_identity,
                       n_cols=kd, tm=1024, tn=512)
            (v,) = _mm(a, [(bf(w[:, 2 * kd:2 * kd + d]), 0)], [], [BF16], _epi_identity,
                       n_cols=d, tm=1024, tn=512)
            (sg,) = _mm(a, [(bf(w[:, 2 * kd + d:2 * kd + 2 * d]), 0)], [], [BF16], _epi_silu,
                        n_cols=d, tm=1024, tn=512)
            w_low = jnp.pad(w[:, 2 * kd + 2 * d:], ((0, 0), (0, LANES - GLA_GATE_RANK)))
            (a_low,) = _mm(a, [(bf(w_low), 0)], [], [BF16], _epi_identity,
                           n_cols=LANES, tm=1024, tn=LANES)
            w_a2 = jnp.pad(gla_w_a2[j], ((0, LANES - GLA_GATE_RANK), (0, 0)))
            (log_a,) = _mm(a_low, [(bf(w_a2), 0)], [(gla_b_a[j].reshape(1, kd), 'row')], [F32],
                           _epi_gla_gate, n_cols=kd, tm=1024, tn=512)
            gated = _recurrence(q, k, log_a, v, sg, gla_norm[j], dk=dk, dv=dv, nh=1, seq=seq)
            w_out = bf(gla_w_out[j])
        else:
            wd = gmlp_w_in.shape[-1] // 2
            nb = wd // 512
            w = bf(gmlp_w_in[j])
            u, v = _mm(a, [(w, 0), (w, nb)], [], [BF16, BF16], _epi_gelu2,
                       n_cols=wd, tm=1024, tn=512)
            gated = _gmlp_spatial(u, v, gmlp_ln_g[j], gmlp_ln_b[j], gmlp_w_s[j], gmlp_b_s[j])
            w_out = bf(gmlp_w_out[j])
        (h,) = _mm(gated, [(w_out, 0)], [(h, 'tile')], [F32], _epi_residual,
                   n_cols=d, tm=512, tn=512)

        w_router = jnp.pad(jnp.concatenate([moe_w_group[layer], moe_w_expert[layer]], axis=1),
                           ((0, 0), (0, LANES - MOE_GROUPS - MOE_GROUPS * MOE_EPG)))
        c, c32, route = _rms_router(h, norm_ffn[layer], w_router)
        ff = moe_w_gate.shape[-1]
        y = _moe_layer(c32, route, bf(moe_w_gate[layer]), bf(moe_w_up[layer]),
                       bf(moe_w_down[layer]).reshape(MOE_GROUPS, MOE_EPG * ff, d))
        (h,) = _mm(c, [(bf(ple_w_gate[layer]), 0)],
                   [(h, 'tile'), (y, 'tile'), (p[layer].reshape(t, -1), 'rowtile'),
                    (bf(ple_w_proj[layer]), 'kcol')],
                   [F32], _epi_ple, n_cols=d, tm=1024, tn=512)
    out = _rmsnorm(h, norm_final, F32)
    return out.reshape(batch, seq, d)
```

```python
import functools

import numpy as np
import jax
import jax.numpy as jnp
from jax import lax
from jax.experimental import pallas as pl
from jax.experimental.pallas import tpu as pltpu

F32 = jnp.float32
BF16 = jnp.bfloat16

EPS = 1e-6
CHUNK = 128
N_LEVELS = 7
LOG2_E = 1.4426950408889634
LANES = 128
VMEM_LIMIT = 56 * 1024 * 1024

CONV_WIDTH = 3
HGRN_DK = 128
GLA_HEADS = 8
GLA_GATE_RANK = 16
GLA_GATE_TAU = 16.0
GMLP_BLOCK = 128
GMLP_GROUPS = 8
MOE_GROUPS = 4
MOE_EPG = 4
MOE_TILE = 256


def _cparams(sem):
    return pltpu.CompilerParams(dimension_semantics=sem, vmem_limit_bytes=VMEM_LIMIT)


def _sigmoid(x):
    return jax.nn.sigmoid(x)


def _silu(x):
    return x * jax.nn.sigmoid(x)


def _mm_kernel(*refs, n_w, n_e, epilogue, kinds, rb):
    a_ref = refs[0]
    w_refs = refs[1:1 + n_w]
    e_refs = refs[1 + n_w:1 + n_w + n_e]
    o_refs = refs[1 + n_w + n_e:-1]
    wbf_ref = refs[-1]

    @pl.when(pl.program_id(1) == 0)
    def _():
        for j, w in enumerate(w_refs):
            wbf_ref[j] = w[...].astype(BF16)

    tm = a_ref.shape[0]
    for r in range(0, tm, rb):
        rows = slice(r, r + rb)
        a = a_ref[rows, :]
        accs = [jnp.dot(a, wbf_ref[j], preferred_element_type=F32) for j in range(n_w)]
        ex = [e[rows, :] if kind in ('tile', 'rowtile') else e[...] for e, kind in zip(e_refs, kinds)]
        res = epilogue(accs, ex)
        for o, val in zip(o_refs, res):
            o[rows, :] = val.astype(o.dtype)


def _mm(a, ws, extras, out_dtypes, epilogue, *, n_cols, tm, tn, rb=256):
    m, k = a.shape
    grid = (n_cols // tn, m // tm)
    in_specs = [pl.BlockSpec((tm, k), lambda n, i: (i, 0))]
    args = [a]
    for w, lead, off in ws:
        in_specs.append(pl.BlockSpec((None, k, tn), lambda n, i, lead=lead, off=off: (lead, 0, n + off),
                                     pipeline_mode=pl.Buffered(1)))
        args.append(w)
    for arr, kind in extras:
        if kind == 'row':
            in_specs.append(pl.BlockSpec((1, tn), lambda n, i: (0, n)))
        elif kind == 'tile':
            in_specs.append(pl.BlockSpec((tm, tn), lambda n, i: (i, n)))
        elif kind == 'rowtile':
            in_specs.append(pl.BlockSpec((tm, arr.shape[1]), lambda n, i: (i, 0)))
        elif kind == 'kcol':
            in_specs.append(pl.BlockSpec((arr.shape[0], tn), lambda n, i: (0, n)))
        else:
            raise ValueError(kind)
        args.append(arr)
    out_shape = [jax.ShapeDtypeStruct((m, n_cols), dt) for dt in out_dtypes]
    out_specs = [pl.BlockSpec((tm, tn), lambda n, i: (i, n)) for _ in out_dtypes]
    res = pl.pallas_call(
        functools.partial(_mm_kernel, n_w=len(ws), n_e=len(extras), epilogue=epilogue,
                          kinds=tuple(kind for _, kind in extras), rb=min(rb, tm)),
        grid=grid, in_specs=in_specs, out_specs=out_specs, out_shape=out_shape,
        scratch_shapes=[pltpu.VMEM((len(ws), k, tn), BF16)],
        compiler_params=_cparams(("arbitrary", "arbitrary")),
    )(*args)
    return res


def _rms_rows(x, gain):
    ms = jnp.mean(x * x, axis=-1, keepdims=True)
    return x * lax.rsqrt(ms + EPS) * gain


def _rms_kernel(h_ref, g_ref, o_ref):
    o_ref[...] = _rms_rows(h_ref[...], g_ref[...]).astype(o_ref.dtype)


def _rmsnorm(h, gain, out_dtype, tm=256):
    t, d = h.shape
    return pl.pallas_call(
        _rms_kernel, grid=(t // tm,),
        in_specs=[pl.BlockSpec((tm, d), lambda i: (i, 0)), pl.BlockSpec((1, d), lambda i: (0, 0))],
        out_specs=pl.BlockSpec((tm, d), lambda i: (i, 0)),
        out_shape=jax.ShapeDtypeStruct((t, d), out_dtype),
        compiler_params=_cparams(("parallel",)),
    )(h, gain.reshape(1, d))


def _first_max_index(vals, m):
    idx = jnp.full(m.shape, len(vals) - 1, jnp.int32)
    for j in range(len(vals) - 2, -1, -1):
        idx = jnp.where(vals[j] == m, j, idx)
    return idx


def _route_cols(logits):
    col = lambda j: logits[:, j:j + 1]
    gl = [col(j) for j in range(MOE_GROUPS)]
    gm = functools.reduce(jnp.maximum, gl)
    gsum = functools.reduce(lambda a, b: a + b, [jnp.exp(g - gm) for g in gl])
    group_p = 1.0 / gsum
    gidx = _first_max_index(gl, gm)
    ig = []
    for e in range(MOE_EPG):
        v = col(MOE_GROUPS + (MOE_GROUPS - 1) * MOE_EPG + e)
        for g in range(MOE_GROUPS - 2, -1, -1):
            v = jnp.where(gidx == g, col(MOE_GROUPS + g * MOE_EPG + e), v)
        ig.append(v)
    m1 = functools.reduce(jnp.maximum, ig)
    i1 = _first_max_index(ig, m1)
    ig2 = [jnp.where(i1 == e, -jnp.inf, ig[e]) for e in range(MOE_EPG)]
    m2 = functools.reduce(jnp.maximum, ig2)
    i2 = _first_max_index(ig2, m2)
    t = jnp.exp(m2 - m1)
    w1 = 1.0 / (1.0 + t) * group_p
    w2 = t / (1.0 + t) * group_p
    lane = lax.broadcasted_iota(jnp.int32, logits.shape, 1)
    out = jnp.where(lane == MOE_EPG, gidx.astype(F32), 0.0)
    for e in range(MOE_EPG):
        ce = jnp.where(i1 == e, w1, jnp.where(i2 == e, w2, 0.0))
        out = jnp.where(lane == e, ce, out)
    return out


def _rms_router_kernel(h_ref, g_ref, whi_ref, wlo_ref, c_ref, route_ref):
    y = _rms_rows(h_ref[...], g_ref[...])
    c = y.astype(BF16)
    c_ref[...] = c
    y_lo = (y - c.astype(F32)).astype(BF16)
    whi = whi_ref[...]
    logits = (jnp.dot(c, whi, preferred_element_type=F32)
              + (jnp.dot(c, wlo_ref[...], preferred_element_type=F32)
                 + jnp.dot(y_lo, whi, preferred_element_type=F32)))
    route_ref[...] = _route_cols(logits)


def _rms_router(h, gain, w_router, tm=256):
    t, d = h.shape
    w_hi = w_router.astype(BF16)
    w_lo = (w_router - w_hi.astype(F32)).astype(BF16)
    return pl.pallas_call(
        _rms_router_kernel, grid=(t // tm,),
        in_specs=[pl.BlockSpec((tm, d), lambda i: (i, 0)),
                  pl.BlockSpec((1, d), lambda i: (0, 0)),
                  pl.BlockSpec((d, LANES), lambda i: (0, 0)),
                  pl.BlockSpec((d, LANES), lambda i: (0, 0))],
        out_specs=[pl.BlockSpec((tm, d), lambda i: (i, 0)),
                   pl.BlockSpec((tm, LANES), lambda i: (i, 0))],
        out_shape=[jax.ShapeDtypeStruct((t, d), BF16),
                   jax.ShapeDtypeStruct((t, LANES), F32)],
        compiler_params=_cparams(("parallel",)),
    )(h, gain.reshape(1, d), w_hi, w_lo)


def _gather_kernel(src_ref, nu_ref, h_hbm, gain_ref, x_ref, hbuf, hsem, *, tm):
    i = pl.program_id(0)
    nu = nu_ref[0]

    def issue(tile, slot):
        def body(r, carry):
            tok = src_ref[tile * tm + r]
            pltpu.make_async_copy(h_hbm.at[pl.ds(tok, 1), :], hbuf.at[slot, pl.ds(r, 1), :],
                                  hsem.at[slot]).start()
            return carry
        lax.fori_loop(0, tm, body, 0, unroll=8)

    @pl.when(i == 0)
    def _():
        issue(0, 0)

    @pl.when(i + 1 < nu)
    def _():
        issue(i + 1, (i + 1) % 2)

    @pl.when(i < nu)
    def _():
        slot = i % 2
        pltpu.make_async_copy(h_hbm.at[pl.ds(0, tm), :], hbuf.at[slot], hsem.at[slot]).wait()
        x_ref[...] = _rms_rows(hbuf[slot], gain_ref[...]).astype(x_ref.dtype)

    @pl.when(i >= nu)
    def _():
        x_ref[...] = jnp.zeros_like(x_ref)


def _gather_sorted(h, gain, src, n_used, nt):
    t, d = h.shape
    tm = MOE_TILE
    return pl.pallas_call(
        functools.partial(_gather_kernel, tm=tm),
        grid_spec=pltpu.PrefetchScalarGridSpec(
            num_scalar_prefetch=2, grid=(nt,),
            in_specs=[pl.BlockSpec(memory_space=pl.ANY),
                      pl.BlockSpec((1, d), lambda i, src, nu: (0, 0))],
            out_specs=pl.BlockSpec((tm, d), lambda i, src, nu: (i, 0)),
            scratch_shapes=[pltpu.VMEM((2, tm, d), F32), pltpu.SemaphoreType.DMA((2,))],
        ),
        out_shape=jax.ShapeDtypeStruct((nt * tm, d), BF16),
        compiler_params=_cparams(("arbitrary",)),
    )(src, n_used, h, gain.reshape(1, d))


def _moe_up_kernel(tg_ref, nu_ref, x_ref, comb_ref, wg_ref, wu_ref, o_ref, wbf_ref):
    e = pl.program_id(0)
    i = pl.program_id(1)

    @pl.when(i < nu_ref[0])
    def _():
        @pl.when((i == 0) | (tg_ref[i] != tg_ref[jnp.maximum(i - 1, 0)]))
        def _():
            wbf_ref[0] = wg_ref[...].astype(BF16)
            wbf_ref[1] = wu_ref[...].astype(BF16)

        x = x_ref[...]
        g = jnp.dot(x, wbf_ref[0], preferred_element_type=F32)
        u = jnp.dot(x, wbf_ref[1], preferred_element_type=F32)
        comb = comb_ref[...]
        lane = lax.broadcasted_iota(jnp.int32, comb.shape, 1)
        sc = jnp.sum(jnp.where(lane == e, comb, 0.0), axis=1, keepdims=True)
        o_ref[...] = (_silu(g) * u * sc).astype(o_ref.dtype)

    @pl.when(i >= nu_ref[0])
    def _():
        o_ref[...] = jnp.zeros_like(o_ref)


def _moe_down_kernel(tg_ref, nu_ref, dst_ref, hid_ref, wd_ref, y_hbm, obuf, sem, *, tm, nt, t):
    i = pl.program_id(0)
    nu = nu_ref[0]

    def wait_slot(slot):
        pltpu.make_async_copy(obuf.at[slot], y_hbm.at[pl.ds(0, tm), :], sem.at[slot]).wait()

    @pl.when(i == 0)
    def _():
        obuf[0] = jnp.zeros(obuf.shape[1:], F32)
        for half in range(2):
            zero_copy = pltpu.make_async_copy(obuf.at[0], y_hbm.at[pl.ds(t + half * tm, tm), :], sem.at[0])
            zero_copy.start()
            zero_copy.wait()

    @pl.when(i < nu)
    def _():
        slot = i % 2

        @pl.when(i >= 2)
        def _():
            wait_slot(slot)

        obuf[slot] = jnp.dot(hid_ref[...], wd_ref[...], preferred_element_type=F32)

        def body(r, carry):
            pltpu.make_async_copy(obuf.at[slot, pl.ds(r, 1), :],
                                  y_hbm.at[pl.ds(dst_ref[i * tm + r], 1), :], sem.at[slot]).start()
            return carry
        lax.fori_loop(0, tm, body, 0, unroll=8)

    @pl.when(i == nt - 1)
    def _():
        wait_slot((nu - 1) % 2)

        @pl.when(nu >= 2)
        def _():
            wait_slot(nu % 2)


def _moe_sorted(x_sorted, comb_sorted, tile_group, n_used, dst, t, w_gate, w_up, base, w_down):
    tp, d = x_sorted.shape
    ff = w_gate.shape[-1]
    tm = MOE_TILE
    nt = tp // tm
    row = lambda i, nu: jnp.minimum(i, nu[0] - 1)
    hidden = pl.pallas_call(
        _moe_up_kernel,
        grid_spec=pltpu.PrefetchScalarGridSpec(
            num_scalar_prefetch=2, grid=(MOE_EPG, nt),
            in_specs=[
                pl.BlockSpec((tm, d), lambda e, i, tg, nu: (row(i, nu), 0)),
                pl.BlockSpec((tm, comb_sorted.shape[1]), lambda e, i, tg, nu: (row(i, nu), 0)),
                pl.BlockSpec((None, d, ff), lambda e, i, tg, nu: (base + tg[i] * MOE_EPG + e, 0, 0)),
                pl.BlockSpec((None, d, ff), lambda e, i, tg, nu: (base + tg[i] * MOE_EPG + e, 0, 0)),
            ],
            out_specs=pl.BlockSpec((tm, ff), lambda e, i, tg, nu: (i, e)),
            scratch_shapes=[pltpu.VMEM((2, d, ff), BF16)],
        ),
        out_shape=jax.ShapeDtypeStruct((tp, MOE_EPG * ff), BF16),
        compiler_params=_cparams(("arbitrary", "arbitrary")),
    )(tile_group, n_used, x_sorted, comb_sorted, w_gate, w_up)
    y = pl.pallas_call(
        functools.partial(_moe_down_kernel, tm=tm, nt=nt, t=t),
        grid_spec=pltpu.PrefetchScalarGridSpec(
            num_scalar_prefetch=3, grid=(nt,),
            in_specs=[
                pl.BlockSpec((tm, MOE_EPG * ff), lambda i, tg, nu, dst: (row(i, nu), 0)),
                pl.BlockSpec((None, MOE_EPG * ff, d), lambda i, tg, nu, dst: (tg[i], 0, 0)),
            ],
            out_specs=pl.BlockSpec(memory_space=pl.ANY),
            scratch_shapes=[pltpu.VMEM((2, tm, d), F32), pltpu.SemaphoreType.DMA((2,))],
        ),
        out_shape=jax.ShapeDtypeStruct((t + 2 * tm, d), F32),
        compiler_params=_cparams(("arbitrary",)),
    )(tile_group, n_used, dst, hidden, w_down)
    return y


def _moe_layer(h, route, gain, w_gate, w_up, base, w_down):
    t, d = h.shape
    tm = MOE_TILE
    nt = t // tm + MOE_GROUPS
    gidx = route[:, MOE_EPG].astype(jnp.int32)
    onehot = (gidx[:, None] == jnp.arange(MOE_GROUPS, dtype=jnp.int32)[None, :]).astype(jnp.int32)
    csum = jnp.cumsum(onehot, axis=0)
    rank = jnp.sum(csum * onehot, axis=1) - 1
    counts = csum[-1]
    ntiles = (counts + tm - 1) // tm
    tile_end = jnp.cumsum(ntiles)
    tile_start = tile_end - ntiles
    pos = tile_start[gidx] * tm + rank
    n_used = tile_end[-1:].astype(jnp.int32)
    tile_group = jnp.minimum(
        jnp.sum((jnp.arange(nt, dtype=jnp.int32)[:, None] >= tile_end[None, :]).astype(jnp.int32), axis=1),
        MOE_GROUPS - 1).astype(jnp.int32)
    token = jnp.full((nt * tm,), -1, jnp.int32).at[pos].set(jnp.arange(t, dtype=jnp.int32))
    r = jnp.arange(nt * tm, dtype=jnp.int32)
    src = jnp.maximum(token, 0)
    dst = jnp.where(token >= 0, token, t + (r // tm % 2) * tm + r % tm)
    x_sorted = _gather_sorted(h, gain, src, n_used, nt)
    comb_sorted = jnp.take(route[:, :2 * MOE_EPG], src, axis=0)
    return _moe_sorted(x_sorted, comb_sorted, tile_group, n_used, dst, t, w_gate, w_up, base, w_down)


def _conv_in_kernel(a_ref, wb_ref, wc_ref, wu_ref, cw_ref, o_ref, carry_ref, wbf_ref, *, tm, seq):
    i = pl.program_id(1)

    @pl.when(i == 0)
    def _():
        wbf_ref[0] = wb_ref[...].astype(BF16)
        wbf_ref[1] = wc_ref[...].astype(BF16)
        wbf_ref[2] = wu_ref[...].astype(BF16)

    a = a_ref[...]
    gate_b = jnp.dot(a, wbf_ref[0], preferred_element_type=F32)
    gate_c = jnp.dot(a, wbf_ref[1], preferred_element_type=F32)
    u = jnp.dot(a, wbf_ref[2], preferred_element_type=F32)
    z = gate_c * u
    @pl.when((i * tm) % seq == 0)
    def _():
        carry_ref[...] = jnp.zeros(carry_ref.shape, F32)

    prev = carry_ref[...]
    rid = lax.broadcasted_iota(jnp.int32, z.shape, 0)
    z1 = jnp.where(rid == 0, prev[7:8], pltpu.roll(z, 1, 0))
    z2 = jnp.where(rid == 0, prev[6:7], jnp.where(rid == 1, prev[7:8], pltpu.roll(z, 2, 0)))
    cw = cw_ref[...]
    conv = cw[0:1] * z2 + cw[1:2] * z1 + cw[2:3] * z
    o_ref[...] = (gate_b * conv).astype(o_ref.dtype)
    carry_ref[...] = z[tm - 8:tm]


def _conv_in(a, w_in, j, conv_w, seq, tm=512, tn=256):
    t, d = a.shape
    nb = d // tn
    return pl.pallas_call(
        functools.partial(_conv_in_kernel, tm=tm, seq=seq),
        grid=(nb, t // tm),
        in_specs=[pl.BlockSpec((tm, d), lambda n, i: (i, 0)),
                  pl.BlockSpec((None, d, tn), lambda n, i: (j, 0, n), pipeline_mode=pl.Buffered(1)),
                  pl.BlockSpec((None, d, tn), lambda n, i: (j, 0, n + nb), pipeline_mode=pl.Buffered(1)),
                  pl.BlockSpec((None, d, tn), lambda n, i: (j, 0, n + 2 * nb), pipeline_mode=pl.Buffered(1)),
                  pl.BlockSpec((CONV_WIDTH, tn), lambda n, i: (0, n))],
        out_specs=pl.BlockSpec((tm, tn), lambda n, i: (i, n)),
        out_shape=jax.ShapeDtypeStruct((t, d), BF16),
        scratch_shapes=[pltpu.VMEM((8, tn), F32), pltpu.VMEM((3, d, tn), BF16)],
        compiler_params=_cparams(("arbitrary", "arbitrary")),
    )(a, w_in, w_in, w_in, conv_w)


def _build_decay_mats():
    t = np.arange(CHUNK)
    i = t[:, None]
    tt = t[None, :]
    mats = [tt <= i, tt > i]
    for l in range(N_LEVELS):
        s = 1 << l
        ref = ((t // (2 * s)) * (2 * s) + s - 1)[:, None]
        upper = ((t % (2 * s)) >= s)[:, None]
        mats.append(np.where(upper, (tt > ref) & (tt <= i), (tt > i) & (tt <= ref)))
    return np.concatenate(mats, axis=0).astype(np.float32)


def _build_level_matrix():
    lv = np.full((CHUNK, CHUNK), -1, np.int32)
    for i in range(CHUNK):
        lv[i, i] = N_LEVELS
        for j in range(i):
            lv[i, j] = int(np.floor(np.log2(i ^ j)))
    return lv


_DECAY_MATS = _build_decay_mats()
_LEVEL_MAT = _build_level_matrix()

_NT = (((1,), (1,)), ((), ()))
_TN = (((0,), (0,)), ((), ()))


def _rec_kernel(q_ref, k_ref, la_ref, v_ref, sg_ref, gain_ref, mall_ref, lvl_ref, o_ref, st_ref,
                *, nh, dk, dv, ts, seq, unroll):
    t = pl.program_id(1)

    @pl.when((t * ts) % seq == 0)
    def _():
        st_ref[...] = jnp.zeros_like(st_ref)

    mall = mall_ref[...]
    lvl = lvl_ref[...]
    gain = gain_ref[...]

    def chunk(c, carry):
        r0 = pl.multiple_of(c * CHUNK, CHUNK)
        rows = pl.ds(r0, CHUNK)
        la = la_ref[rows, :] * LOG2_E
        la_hi = la.astype(BF16)
        la_lo = (la - la_hi.astype(F32)).astype(BF16)
        dall = jnp.dot(mall, jnp.concatenate([la_hi, la_lo], axis=0), preferred_element_type=F32)
        for h in range(nh):
            kc = slice(h * dk, (h + 1) * dk)
            vc = slice(h * dv, (h + 1) * dv)
            q = q_ref[rows, kc].astype(F32)
            k = k_ref[rows, kc].astype(F32)
            g = dall[0:CHUNK, kc]
            suffix = dall[CHUNK:2 * CHUNK, kc]
            scores = jnp.where(
                lvl == N_LEVELS,
                lax.dot_general(q.astype(BF16), k.astype(BF16), _NT, preferred_element_type=F32), 0.0)
            for l in range(N_LEVELS):
                e = jnp.exp2(dall[(2 + l) * CHUNK:(3 + l) * CHUNK, kc])
                r = lax.dot_general((q * e).astype(BF16), (k * e).astype(BF16), _NT,
                                    preferred_element_type=F32)
                scores = jnp.where(lvl == l, r, scores)
            v = v_ref[rows, vc]
            state_t = st_ref[h]
            o = (jnp.dot(scores.astype(BF16), v, preferred_element_type=F32)
                 + lax.dot_general((q * jnp.exp2(g)).astype(BF16), state_t.astype(BF16), _NT,
                                   preferred_element_type=F32))
            k_end = (k * jnp.exp2(suffix)).astype(BF16)
            st_ref[h] = (state_t * jnp.exp2(g[CHUNK - 1:CHUNK])
                         + lax.dot_general(v, k_end, _TN, preferred_element_type=F32))
            on = _rms_rows(o, gain[:, vc]) * sg_ref[rows, vc].astype(F32)
            o_ref[rows, vc] = on.astype(o_ref.dtype)
        return carry

    lax.fori_loop(0, ts // CHUNK, chunk, 0, unroll=unroll)


def _recurrence(q, k, la, v, sg, gain, *, dk, dv, nh, seq, ts=1024, unroll=2):
    t = q.shape[0]
    heads = q.shape[1] // dk
    ts = min(ts, seq)
    mall = jnp.asarray(np.concatenate([_DECAY_MATS, _DECAY_MATS], axis=1), BF16)
    lvl = jnp.asarray(_LEVEL_MAT)
    kspec = pl.BlockSpec((ts, nh * dk), lambda hg, s: (s, hg))
    vspec = pl.BlockSpec((ts, nh * dv), lambda hg, s: (s, hg))
    return pl.pallas_call(
        functools.partial(_rec_kernel, nh=nh, dk=dk, dv=dv, ts=ts, seq=seq, unroll=unroll),
        grid=(heads // nh, t // ts),
        in_specs=[kspec, kspec, kspec, vspec, vspec,
                  pl.BlockSpec((1, nh * dv), lambda hg, s: (0, hg)),
                  pl.BlockSpec(mall.shape, lambda hg, s: (0, 0)),
                  pl.BlockSpec(lvl.shape, lambda hg, s: (0, 0))],
        out_specs=vspec,
        out_shape=jax.ShapeDtypeStruct((t, heads * dv), BF16),
        scratch_shapes=[pltpu.VMEM((nh, dv, dk), F32)],
        compiler_params=_cparams(("parallel", "arbitrary")),
    )(q, k, la, v, sg, gain.reshape(1, -1), mall, lvl)


def _gelu_tanh(x):
    return jax.nn.gelu(x, approximate=True)


def _gmlp_spatial_kernel(u_ref, v_ref, lng_ref, lnb_ref, ws_ref, bs_ref, o_ref, *, groups):
    v = v_ref[...].astype(F32)
    mu = jnp.mean(v, axis=-1, keepdims=True)
    vc = v - mu
    var = jnp.mean(vc * vc, axis=-1, keepdims=True)
    vn = (vc * lax.rsqrt(var + EPS) * lng_ref[...] + lnb_ref[...]).astype(BF16)
    blk = v.shape[0]
    gw = v.shape[1] // groups
    ri = lax.broadcasted_iota(jnp.int32, (blk, blk), 0)
    ci = lax.broadcasted_iota(jnp.int32, (blk, blk), 1)
    for g in range(groups):
        cols = slice(g * gw, (g + 1) * gw)
        wc = jnp.where(ci <= ri, ws_ref[g], 0.0).astype(BF16)
        mixed = jnp.dot(wc, vn[:, cols], preferred_element_type=F32) + bs_ref[g]
        o_ref[:, cols] = (u_ref[:, cols].astype(F32) * mixed).astype(o_ref.dtype)


def _gmlp_spatial(u, v, ln_g, ln_b, w_s, b_s):
    t, w = u.shape
    groups, blk = w_s.shape[0], w_s.shape[1]
    rows = pl.BlockSpec((blk, w), lambda i: (i, 0))
    return pl.pallas_call(
        functools.partial(_gmlp_spatial_kernel, groups=groups),
        grid=(t // blk,),
        in_specs=[rows, rows,
                  pl.BlockSpec((1, w), lambda i: (0, 0)),
                  pl.BlockSpec((1, w), lambda i: (0, 0)),
                  pl.BlockSpec((groups, blk, blk), lambda i: (0, 0, 0)),
                  pl.BlockSpec((groups, blk, 1), lambda i: (0, 0, 0))],
        out_specs=rows,
        out_shape=jax.ShapeDtypeStruct((t, w), BF16),
        compiler_params=_cparams(("parallel",)),
    )(u, v, ln_g.reshape(1, w), ln_b.reshape(1, w), w_s, b_s.reshape(groups, blk, 1))


def _epi_residual(accs, extras):
    return [extras[0] + accs[0]]


def _epi_hgrn(accs, extras):
    q, f, i, g = accs
    lb = extras[0]
    e = jnp.exp(-jnp.abs(f))
    log_sig = jnp.minimum(f, 0.0) - jnp.log1p(e)
    a = jnp.log(lb)
    b = jnp.log1p(-lb) + log_sig
    log_forget = jnp.maximum(a, b) + jnp.log1p(jnp.exp(-jnp.abs(a - b)))
    key = (1.0 - lb) * (jnp.where(f >= 0.0, e, 1.0) / (1.0 + e))
    return [_silu(q), key, log_forget, i, _silu(g)]


def _epi_gelu2(accs, extras):
    return [_gelu_tanh(accs[0]), _gelu_tanh(accs[1])]


def _epi_scale(scale):
    return lambda accs, extras: [accs[0] * scale]


def _epi_identity(accs, extras):
    return [accs[0]]


def _epi_silu(accs, extras):
    return [_silu(accs[0])]


def _epi_gla_gate(accs, extras):
    z = accs[0] + extras[0]
    log_sig = jnp.minimum(z, 0.0) - jnp.log1p(jnp.exp(-jnp.abs(z)))
    return [log_sig / GLA_GATE_TAU]


def _epi_first_lanes(width):
    def epi(accs, extras):
        lane = lax.broadcasted_iota(jnp.int32, accs[0].shape, 1)
        return [jnp.where(lane < width, accs[0], 0.0)]
    return epi


def _epi_ple(accs, extras):
    h, y, p, proj = extras
    emb = jnp.dot(p.astype(BF16), proj.astype(BF16), preferred_element_type=F32)
    return [(h + y) + emb * _sigmoid(accs[0])]


def kernel(x, p, norm_mix, norm_ffn, norm_final, conv_w_in, conv_w, conv_w_out, hgrn_w_in, hgrn_lb, hgrn_norm, hgrn_w_out, gla_w_in, gla_w_a2, gla_b_a, gla_norm, gla_w_out, gmlp_w_in, gmlp_ln_g, gmlp_ln_b, gmlp_w_s, gmlp_b_s, gmlp_w_out, moe_w_group, moe_w_expert, moe_w_gate, moe_w_up, moe_w_down, ple_w_proj, ple_w_gate):
    batch, seq, d = x.shape
    depth = p.shape[0]
    t = batch * seq
    ff = moe_w_gate.shape[-1]
    n_exp = moe_w_gate.shape[1]

    lb_w = jax.nn.softmax(hgrn_lb.astype(F32), axis=0)
    lower_bounds = jnp.cumsum(lb_w, axis=0) - lb_w[0]
    w_gate_all = moe_w_gate.reshape(depth * n_exp, d, ff)
    w_up_all = moe_w_up.reshape(depth * n_exp, d, ff)

    h = x.reshape(t, d)
    for layer in range(depth):
        kind, j = layer % 4, layer // 4
        a = _rmsnorm(h, norm_mix[layer], BF16)
        if kind == 0:
            gated = _conv_in(a, conv_w_in, j, conv_w[j], seq)
            w_out = conv_w_out
        elif kind == 1:
            nb = d // 256
            qt, key, log_f, val, sg = _mm(
                a, [(hgrn_w_in, j, o * nb) for o in range(4)],
                [(lower_bounds[layer].reshape(1, d), 'row')],
                [BF16, BF16, F32, BF16, BF16], _epi_hgrn, n_cols=d, tm=1024, tn=256)
            gated = _recurrence(qt, key, log_f, val, sg, hgrn_norm[j],
                                dk=HGRN_DK, dv=HGRN_DK, nh=4, seq=seq)
            w_out = hgrn_w_out
        elif kind == 2:
            kd = d // 2
            dk = kd // GLA_HEADS
            dv = d // GLA_HEADS
            tn = 1024
            (q,) = _mm(a, [(gla_w_in, j, 0)], [], [BF16], _epi_scale(dk ** -0.5),
                       n_cols=kd, tm=512, tn=tn)
            (k,) = _mm(a, [(gla_w_in, j, kd // tn)], [], [BF16], _epi_identity,
                       n_cols=kd, tm=512, tn=tn)
            (v,) = _mm(a, [(gla_w_in, j, 2 * kd // tn)], [], [BF16], _epi_identity,
                       n_cols=d, tm=512, tn=tn)
            (sg,) = _mm(a, [(gla_w_in, j, (2 * kd + d) // tn)], [], [BF16], _epi_silu,
                        n_cols=d, tm=512, tn=tn)
            (a_low,) = _mm(a, [(gla_w_in, j, (2 * kd + 2 * d) // LANES)], [], [BF16],
                           _epi_first_lanes(GLA_GATE_RANK), n_cols=LANES, tm=512, tn=LANES)
            w_a2 = jnp.pad(gla_w_a2, ((0, 0), (0, LANES - GLA_GATE_RANK), (0, 0)))
            (log_a,) = _mm(a_low, [(w_a2, j, 0)], [(gla_b_a[j].reshape(1, kd), 'row')], [F32],
                           _epi_gla_gate, n_cols=kd, tm=1024, tn=tn)
            gated = _recurrence(q, k, log_a, v, sg, gla_norm[j], dk=dk, dv=dv, nh=1, seq=seq)
            w_out = gla_w_out
        else:
            wd = gmlp_w_in.shape[-1] // 2
            u, v = _mm(a, [(gmlp_w_in, j, 0), (gmlp_w_in, j, wd // 512)], [], [BF16, BF16], _epi_gelu2,
                       n_cols=wd, tm=1024, tn=512)
            gated = _gmlp_spatial(u, v, gmlp_ln_g[j], gmlp_ln_b[j], gmlp_w_s[j], gmlp_b_s[j])
            w_out = gmlp_w_out
        (h,) = _mm(gated, [(w_out, j, 0)], [(h, 'tile')], [F32], _epi_residual,
                   n_cols=d, tm=512, tn=512 if gated.shape[1] > d else 1024)

        w_router = jnp.pad(jnp.concatenate([moe_w_group[layer], moe_w_expert[layer]], axis=1),
                           ((0, 0), (0, LANES - MOE_GROUPS - MOE_GROUPS * MOE_EPG)))
        c, route = _rms_router(h, norm_ffn[layer], w_router)
        y = _moe_layer(h, route, norm_ffn[layer], w_gate_all, w_up_all, layer * n_exp,
                       moe_w_down[layer].astype(BF16).reshape(MOE_GROUPS, MOE_EPG * ff, d))
        (h,) = _mm(c, [(ple_w_gate, layer, 0)],
                   [(h, 'tile'), (y, 'tile'), (p[layer].reshape(t, -1), 'rowtile'),
                    (ple_w_proj[layer], 'kcol')],
                   [F32], _epi_ple, n_cols=d, tm=512, tn=1024)
    out = _rmsnorm(h, norm_final, F32)
    return out.reshape(batch, seq, d)
```

```python
import functools

import numpy as np
import jax
import jax.numpy as jnp
from jax import lax
from jax.experimental import pallas as pl
from jax.experimental.pallas import tpu as pltpu

F32 = jnp.float32
BF16 = jnp.bfloat16

EPS = 1e-6
CHUNK = 128
N_LEVELS = 7
LOG2_E = 1.4426950408889634
LANES = 128
VMEM_LIMIT = 56 * 1024 * 1024

CONV_WIDTH = 3
HGRN_DK = 128
GLA_HEADS = 8
GLA_GATE_RANK = 16
GLA_GATE_TAU = 16.0
GMLP_BLOCK = 128
GMLP_GROUPS = 8
MOE_GROUPS = 4
MOE_EPG = 4
MOE_TOP_K = 2
MOE_PAIRS = 6
MOE_PAIR_A = (0, 0, 0, 1, 1, 2)
MOE_PAIR_B = (1, 2, 3, 2, 3, 3)
MOE_TILE = 256


def _cparams(sem):
    return pltpu.CompilerParams(dimension_semantics=sem, vmem_limit_bytes=VMEM_LIMIT)


def _sigmoid(x):
    return jax.nn.sigmoid(x)


def _silu(x):
    return x * jax.nn.sigmoid(x)


def _mm_kernel(*refs, n_w, n_e, epilogue, kinds, rb):
    a_ref = refs[0]
    w_refs = refs[1:1 + n_w]
    e_refs = refs[1 + n_w:1 + n_w + n_e]
    o_refs = refs[1 + n_w + n_e:-1]
    wbf_ref = refs[-1]

    @pl.when(pl.program_id(1) == 0)
    def _():
        for j, w in enumerate(w_refs):
            wbf_ref[j] = w[...].astype(BF16)

    tm = a_ref.shape[0]
    for r in range(0, tm, rb):
        rows = slice(r, r + rb)
        a = a_ref[rows, :]
        accs = [jnp.dot(a, wbf_ref[j], preferred_element_type=F32) for j in range(n_w)]
        ex = [e[rows, :] if kind in ('tile', 'rowtile') else e[...] for e, kind in zip(e_refs, kinds)]
        res = epilogue(accs, ex)
        for o, val in zip(o_refs, res):
            o[rows, :] = val.astype(o.dtype)


def _mm(a, ws, extras, out_dtypes, epilogue, *, n_cols, tm, tn, rb=256):
    m, k = a.shape
    grid = (n_cols // tn, m // tm)
    in_specs = [pl.BlockSpec((tm, k), lambda n, i: (i, 0))]
    args = [a]
    for w, lead, off in ws:
        in_specs.append(pl.BlockSpec((None, k, tn), lambda n, i, lead=lead, off=off: (lead, 0, n + off),
                                     pipeline_mode=pl.Buffered(1)))
        args.append(w)
    for arr, kind, lead in extras:
        if kind == 'row':
            in_specs.append(pl.BlockSpec((None, 1, tn), lambda n, i, lead=lead: (lead, 0, n)))
        elif kind == 'tile':
            assert lead is None
            in_specs.append(pl.BlockSpec((tm, tn), lambda n, i: (i, n)))
        elif kind == 'rowtile':
            in_specs.append(pl.BlockSpec((None, tm, arr.shape[2]), lambda n, i, lead=lead: (lead, i, 0)))
        elif kind == 'kcol':
            in_specs.append(pl.BlockSpec((None, arr.shape[1], tn), lambda n, i, lead=lead: (lead, 0, n)))
        else:
            raise ValueError(kind)
        args.append(arr)
    out_shape = [jax.ShapeDtypeStruct((m, n_cols), dt) for dt in out_dtypes]
    out_specs = [pl.BlockSpec((tm, tn), lambda n, i: (i, n)) for _ in out_dtypes]
    res = pl.pallas_call(
        functools.partial(_mm_kernel, n_w=len(ws), n_e=len(extras), epilogue=epilogue,
                          kinds=tuple(kind for _, kind, _ in extras), rb=min(rb, tm)),
        grid=grid, in_specs=in_specs, out_specs=out_specs, out_shape=out_shape,
        scratch_shapes=[pltpu.VMEM((len(ws), k, tn), BF16)],
        compiler_params=_cparams(("arbitrary", "arbitrary")),
    )(*args)
    return res


def _rms_rows(x, gain):
    ms = jnp.mean(x * x, axis=-1, keepdims=True)
    return x * lax.rsqrt(ms + EPS) * gain


def _rms_kernel(h_ref, g_ref, o_ref):
    o_ref[...] = _rms_rows(h_ref[...], g_ref[...]).astype(o_ref.dtype)


def _rmsnorm(h, gain, out_dtype, tm=256):
    t, d = h.shape
    return pl.pallas_call(
        _rms_kernel, grid=(t // tm,),
        in_specs=[pl.BlockSpec((tm, d), lambda i: (i, 0)), pl.BlockSpec((1, d), lambda i: (0, 0))],
        out_specs=pl.BlockSpec((tm, d), lambda i: (i, 0)),
        out_shape=jax.ShapeDtypeStruct((t, d), out_dtype),
        compiler_params=_cparams(("parallel",)),
    )(h, gain.reshape(1, d))


def _first_max_index(vals, m):
    idx = jnp.full(m.shape, len(vals) - 1, jnp.int32)
    for j in range(len(vals) - 2, -1, -1):
        idx = jnp.where(vals[j] == m, j, idx)
    return idx


def _route_cols(logits):
    col = lambda j: logits[:, j:j + 1]
    gl = [col(j) for j in range(MOE_GROUPS)]
    gm = functools.reduce(jnp.maximum, gl)
    gsum = functools.reduce(lambda a, b: a + b, [jnp.exp(g - gm) for g in gl])
    group_p = 1.0 / gsum
    gidx = _first_max_index(gl, gm)
    ig = []
    for e in range(MOE_EPG):
        v = col(MOE_GROUPS + (MOE_GROUPS - 1) * MOE_EPG + e)
        for g in range(MOE_GROUPS - 2, -1, -1):
            v = jnp.where(gidx == g, col(MOE_GROUPS + g * MOE_EPG + e), v)
        ig.append(v)
    m1 = functools.reduce(jnp.maximum, ig)
    i1 = _first_max_index(ig, m1)
    ig2 = [jnp.where(i1 == e, -jnp.inf, ig[e]) for e in range(MOE_EPG)]
    m2 = functools.reduce(jnp.maximum, ig2)
    i2 = _first_max_index(ig2, m2)
    t = jnp.exp(m2 - m1)
    w1 = 1.0 / (1.0 + t) * group_p
    w2 = t / (1.0 + t) * group_p
    first = i1 < i2
    ea = jnp.where(first, i1, i2)
    eb = jnp.where(first, i2, i1)
    wa = jnp.where(first, w1, w2)
    wb = jnp.where(first, w2, w1)
    pair = jnp.where(ea == 0, 0, jnp.where(ea == 1, 3, 5)) + (eb - ea - 1)
    seg = gidx * MOE_PAIRS + pair
    lane = lax.broadcasted_iota(jnp.int32, logits.shape, 1)
    return jnp.where(lane == 0, wa, jnp.where(lane == 1, wb, jnp.where(lane == 2, seg.astype(F32), 0.0)))


def _rms_router_kernel(h_ref, g_ref, whi_ref, wlo_ref, c_ref, route_ref):
    y = _rms_rows(h_ref[...], g_ref[...])
    c = y.astype(BF16)
    c_ref[...] = c
    y_lo = (y - c.astype(F32)).astype(BF16)
    whi = whi_ref[...]
    logits = (jnp.dot(c, whi, preferred_element_type=F32)
              + (jnp.dot(c, wlo_ref[...], preferred_element_type=F32)
                 + jnp.dot(y_lo, whi, preferred_element_type=F32)))
    route_ref[...] = _route_cols(logits)


def _rms_router(h, gain, w_router, tm=256):
    t, d = h.shape
    w_hi = w_router.astype(BF16)
    w_lo = (w_router - w_hi.astype(F32)).astype(BF16)
    return pl.pallas_call(
        _rms_router_kernel, grid=(t // tm,),
        in_specs=[pl.BlockSpec((tm, d), lambda i: (i, 0)),
                  pl.BlockSpec((1, d), lambda i: (0, 0)),
                  pl.BlockSpec((d, LANES), lambda i: (0, 0)),
                  pl.BlockSpec((d, LANES), lambda i: (0, 0))],
        out_specs=[pl.BlockSpec((tm, d), lambda i: (i, 0)),
                   pl.BlockSpec((tm, LANES), lambda i: (i, 0))],
        out_shape=[jax.ShapeDtypeStruct((t, d), BF16),
                   jax.ShapeDtypeStruct((t, LANES), F32)],
        compiler_params=_cparams(("parallel",)),
    )(h, gain.reshape(1, d), w_hi, w_lo)


def _gather_kernel(src_ref, nu_ref, h_hbm, gain_ref, x_ref, hbuf, hsem, *, tm):
    i = pl.program_id(0)
    nu = nu_ref[0]

    def issue(tile, slot):
        def body(r, carry):
            tok = src_ref[tile * tm + r]
            pltpu.make_async_copy(h_hbm.at[pl.ds(tok, 1), :], hbuf.at[slot, pl.ds(r, 1), :],
                                  hsem.at[slot]).start()
            return carry
        lax.fori_loop(0, tm, body, 0, unroll=8)

    @pl.when(i == 0)
    def _():
        issue(0, 0)

    @pl.when(i + 1 < nu)
    def _():
        issue(i + 1, (i + 1) % 2)

    @pl.when(i < nu)
    def _():
        slot = i % 2
        pltpu.make_async_copy(h_hbm.at[pl.ds(0, tm), :], hbuf.at[slot], hsem.at[slot]).wait()
        x_ref[...] = _rms_rows(hbuf[slot], gain_ref[...]).astype(x_ref.dtype)

    @pl.when(i >= nu)
    def _():
        x_ref[...] = jnp.zeros_like(x_ref)


def _gather_sorted(h, gain, src, n_used, nt):
    t, d = h.shape
    tm = MOE_TILE
    return pl.pallas_call(
        functools.partial(_gather_kernel, tm=tm),
        grid_spec=pltpu.PrefetchScalarGridSpec(
            num_scalar_prefetch=2, grid=(nt,),
            in_specs=[pl.BlockSpec(memory_space=pl.ANY),
                      pl.BlockSpec((1, d), lambda i, src, nu: (0, 0))],
            out_specs=pl.BlockSpec((tm, d), lambda i, src, nu: (i, 0)),
            scratch_shapes=[pltpu.VMEM((2, tm, d), F32), pltpu.SemaphoreType.DMA((2,))],
        ),
        out_shape=jax.ShapeDtypeStruct((nt * tm, d), BF16),
        compiler_params=_cparams(("arbitrary",)),
    )(src, n_used, h, gain.reshape(1, d))


def _moe_up_kernel(ea_ref, eb_ref, nu_ref, x_ref, comb_ref, wg_ref, wu_ref, o_ref, wbf_ref):
    s = pl.program_id(0)
    i = pl.program_id(1)

    def expert(tile):
        return jnp.where(s == 0, ea_ref[tile], eb_ref[tile])

    @pl.when(i < nu_ref[0])
    def _():
        @pl.when((i == 0) | (expert(i) != expert(jnp.maximum(i - 1, 0))))
        def _():
            wbf_ref[0] = wg_ref[...].astype(BF16)
            wbf_ref[1] = wu_ref[...].astype(BF16)

        x = x_ref[...]
        g = jnp.dot(x, wbf_ref[0], preferred_element_type=F32)
        u = jnp.dot(x, wbf_ref[1], preferred_element_type=F32)
        comb = comb_ref[...]
        lane = lax.broadcasted_iota(jnp.int32, comb.shape, 1)
        sc = jnp.sum(jnp.where(lane == s, comb, 0.0), axis=1, keepdims=True)
        o_ref[...] = (_silu(g) * u * sc).astype(o_ref.dtype)

    @pl.when(i >= nu_ref[0])
    def _():
        o_ref[...] = jnp.zeros_like(o_ref)


def _moe_down_kernel(ea_ref, eb_ref, nu_ref, dst_ref, hid_ref, wda_ref, wdb_ref, y_hbm, obuf, sem,
                     *, tm, nt, t, ff):
    i = pl.program_id(0)
    nu = nu_ref[0]

    def wait_slot(slot):
        pltpu.make_async_copy(obuf.at[slot], y_hbm.at[pl.ds(0, tm), :], sem.at[slot]).wait()

    @pl.when(i == 0)
    def _():
        obuf[0] = jnp.zeros(obuf.shape[1:], F32)
        for half in range(2):
            zero_copy = pltpu.make_async_copy(obuf.at[0], y_hbm.at[pl.ds(t + half * tm, tm), :], sem.at[0])
            zero_copy.start()
            zero_copy.wait()

    @pl.when(i < nu)
    def _():
        slot = i % 2

        @pl.when(i >= 2)
        def _():
            wait_slot(slot)

        obuf[slot] = (jnp.dot(hid_ref[:, :ff], wda_ref[...], preferred_element_type=F32)
                      + jnp.dot(hid_ref[:, ff:], wdb_ref[...], preferred_element_type=F32))

        def body(r, carry):
            pltpu.make_async_copy(obuf.at[slot, pl.ds(r, 1), :],
                                  y_hbm.at[pl.ds(dst_ref[i * tm + r], 1), :], sem.at[slot]).start()
            return carry
        lax.fori_loop(0, tm, body, 0, unroll=8)

    @pl.when(i == nt - 1)
    def _():
        wait_slot((nu - 1) % 2)

        @pl.when(nu >= 2)
        def _():
            wait_slot(nu % 2)


def _moe_sorted(x_sorted, comb_sorted, exp_a, exp_b, n_used, dst, t, w_gate, w_up, w_down):
    tp, d = x_sorted.shape
    ff = w_gate.shape[-1]
    tm = MOE_TILE
    nt = tp // tm
    row = lambda i, nu: jnp.minimum(i, nu[0] - 1)
    wsel = lambda s, i, ea, eb, nu: (jnp.where(s == 0, ea[i], eb[i]), 0, 0)
    hidden = pl.pallas_call(
        _moe_up_kernel,
        grid_spec=pltpu.PrefetchScalarGridSpec(
            num_scalar_prefetch=3, grid=(MOE_TOP_K, nt),
            in_specs=[
                pl.BlockSpec((tm, d), lambda s, i, ea, eb, nu: (row(i, nu), 0)),
                pl.BlockSpec((tm, comb_sorted.shape[1]), lambda s, i, ea, eb, nu: (row(i, nu), 0)),
                pl.BlockSpec((None, d, ff), wsel),
                pl.BlockSpec((None, d, ff), wsel),
            ],
            out_specs=pl.BlockSpec((tm, ff), lambda s, i, ea, eb, nu: (i, s)),
            scratch_shapes=[pltpu.VMEM((2, d, ff), BF16)],
        ),
        out_shape=jax.ShapeDtypeStruct((tp, MOE_TOP_K * ff), BF16),
        compiler_params=_cparams(("arbitrary", "arbitrary")),
    )(exp_a, exp_b, n_used, x_sorted, comb_sorted, w_gate, w_up)
    y = pl.pallas_call(
        functools.partial(_moe_down_kernel, tm=tm, nt=nt, t=t, ff=ff),
        grid_spec=pltpu.PrefetchScalarGridSpec(
            num_scalar_prefetch=4, grid=(nt,),
            in_specs=[
                pl.BlockSpec((tm, MOE_TOP_K * ff), lambda i, ea, eb, nu, dst: (row(i, nu), 0)),
                pl.BlockSpec((None, ff, d), lambda i, ea, eb, nu, dst: (ea[i], 0, 0)),
                pl.BlockSpec((None, ff, d), lambda i, ea, eb, nu, dst: (eb[i], 0, 0)),
            ],
            out_specs=pl.BlockSpec(memory_space=pl.ANY),
            scratch_shapes=[pltpu.VMEM((2, tm, d), F32), pltpu.SemaphoreType.DMA((2,))],
        ),
        out_shape=jax.ShapeDtypeStruct((t + 2 * tm, d), F32),
        compiler_params=_cparams(("arbitrary",)),
    )(exp_a, exp_b, n_used, dst, hidden, w_down, w_down)
    return y


def _moe_layer(h, route, gain, w_gate, w_up, w_down, base):
    t, d = h.shape
    tm = MOE_TILE
    n_seg = MOE_GROUPS * MOE_PAIRS
    nt = t // tm + n_seg
    seg = route[:, 2].astype(jnp.int32)
    onehot = (seg[:, None] == jnp.arange(n_seg, dtype=jnp.int32)[None, :]).astype(jnp.int32)
    csum = jnp.cumsum(onehot, axis=0)
    rank = jnp.sum(csum * onehot, axis=1) - 1
    counts = csum[-1]
    ntiles = (counts + tm - 1) // tm
    tile_end = jnp.cumsum(ntiles)
    tile_start = tile_end - ntiles
    pos = tile_start[seg] * tm + rank
    n_used = tile_end[-1:].astype(jnp.int32)
    tile_seg = jnp.minimum(
        jnp.sum((jnp.arange(nt, dtype=jnp.int32)[:, None] >= tile_end[None, :]).astype(jnp.int32), axis=1),
        n_seg - 1).astype(jnp.int32)
    tile_first = base + (tile_seg // MOE_PAIRS) * MOE_EPG
    exp_a = tile_first + jnp.asarray(MOE_PAIR_A, jnp.int32)[tile_seg % MOE_PAIRS]
    exp_b = tile_first + jnp.asarray(MOE_PAIR_B, jnp.int32)[tile_seg % MOE_PAIRS]
    token = jnp.full((nt * tm,), -1, jnp.int32).at[pos].set(jnp.arange(t, dtype=jnp.int32))
    r = jnp.arange(nt * tm, dtype=jnp.int32)
    src = jnp.maximum(token, 0)
    dst = jnp.where(token >= 0, token, t + (r // tm % 2) * tm + r % tm)
    x_sorted = _gather_sorted(h, gain, src, n_used, nt)
    comb_sorted = jnp.take(route[:, :8], src, axis=0)
    return _moe_sorted(x_sorted, comb_sorted, exp_a, exp_b, n_used, dst, t, w_gate, w_up, w_down)


def _conv_in_kernel(a_ref, wb_ref, wc_ref, wu_ref, cw_ref, o_ref, carry_ref, wbf_ref, *, tm, seq, rb):
    i = pl.program_id(1)

    @pl.when(i == 0)
    def _():
        wbf_ref[0] = wb_ref[...].astype(BF16)
        wbf_ref[1] = wc_ref[...].astype(BF16)
        wbf_ref[2] = wu_ref[...].astype(BF16)

    @pl.when((i * tm) % seq == 0)
    def _():
        carry_ref[...] = jnp.zeros(carry_ref.shape, F32)

    prev = carry_ref[...]
    cw = cw_ref[...]
    for r in range(0, tm, rb):
        rows = slice(r, r + rb)
        a = a_ref[rows, :]
        gate_b = jnp.dot(a, wbf_ref[0], preferred_element_type=F32)
        gate_c = jnp.dot(a, wbf_ref[1], preferred_element_type=F32)
        u = jnp.dot(a, wbf_ref[2], preferred_element_type=F32)
        z = gate_c * u
        rid = lax.broadcasted_iota(jnp.int32, z.shape, 0)
        z1 = jnp.where(rid == 0, prev[7:8], pltpu.roll(z, 1, 0))
        z2 = jnp.where(rid == 0, prev[6:7], jnp.where(rid == 1, prev[7:8], pltpu.roll(z, 2, 0)))
        conv = cw[0:1] * z2 + cw[1:2] * z1 + cw[2:3] * z
        o_ref[rows, :] = (gate_b * conv).astype(o_ref.dtype)
        prev = z[rb - 8:rb]
    carry_ref[...] = prev


def _conv_in(a, w_in, j, conv_w, seq, tm=1024, tn=256, rb=256):
    t, d = a.shape
    nb = d // tn
    return pl.pallas_call(
        functools.partial(_conv_in_kernel, tm=tm, seq=seq, rb=min(rb, tm)),
        grid=(nb, t // tm),
        in_specs=[pl.BlockSpec((tm, d), lambda n, i: (i, 0)),
                  pl.BlockSpec((None, d, tn), lambda n, i: (j, 0, n), pipeline_mode=pl.Buffered(1)),
                  pl.BlockSpec((None, d, tn), lambda n, i: (j, 0, n + nb), pipeline_mode=pl.Buffered(1)),
                  pl.BlockSpec((None, d, tn), lambda n, i: (j, 0, n + 2 * nb), pipeline_mode=pl.Buffered(1)),
                  pl.BlockSpec((CONV_WIDTH, tn), lambda n, i: (0, n))],
        out_specs=pl.BlockSpec((tm, tn), lambda n, i: (i, n)),
        out_shape=jax.ShapeDtypeStruct((t, d), BF16),
        scratch_shapes=[pltpu.VMEM((8, tn), F32), pltpu.VMEM((3, d, tn), BF16)],
        compiler_params=_cparams(("arbitrary", "arbitrary")),
    )(a, w_in, w_in, w_in, conv_w)


def _build_decay_mats():
    t = np.arange(CHUNK)
    i = t[:, None]
    tt = t[None, :]
    mats = [tt <= i, tt > i]
    for l in range(N_LEVELS):
        s = 1 << l
        ref = ((t // (2 * s)) * (2 * s) + s - 1)[:, None]
        upper = ((t % (2 * s)) >= s)[:, None]
        mats.append(np.where(upper, (tt > ref) & (tt <= i), (tt > i) & (tt <= ref)))
    return np.concatenate(mats, axis=0).astype(np.float32)


def _build_level_matrix():
    lv = np.full((CHUNK, CHUNK), -1, np.int32)
    for i in range(CHUNK):
        lv[i, i] = N_LEVELS
        for j in range(i):
            lv[i, j] = int(np.floor(np.log2(i ^ j)))
    return lv


_DECAY_MATS = _build_decay_mats()
_LEVEL_MAT = _build_level_matrix()

_NT = (((1,), (1,)), ((), ()))
_TN = (((0,), (0,)), ((), ()))


def _rec_kernel(q_ref, k_ref, la_ref, v_ref, sg_ref, gain_ref, mall_ref, lvl_ref, o_ref, st_ref,
                *, nh, dk, dv, ts, seq, unroll):
    t = pl.program_id(1)

    @pl.when((t * ts) % seq == 0)
    def _():
        st_ref[...] = jnp.zeros_like(st_ref)

    mall = mall_ref[...]
    lvl = lvl_ref[...]
    gain = gain_ref[...]

    def chunk(c, carry):
        r0 = pl.multiple_of(c * CHUNK, CHUNK)
        rows = pl.ds(r0, CHUNK)
        la = la_ref[rows, :] * LOG2_E
        la_hi = la.astype(BF16)
        la_lo = (la - la_hi.astype(F32)).astype(BF16)
        dall = jnp.dot(mall, jnp.concatenate([la_hi, la_lo], axis=0), preferred_element_type=F32)
        for h in range(nh):
            kc = slice(h * dk, (h + 1) * dk)
            vc = slice(h * dv, (h + 1) * dv)
            q = q_ref[rows, kc].astype(F32)
            k = k_ref[rows, kc].astype(F32)
            g = dall[0:CHUNK, kc]
            suffix = dall[CHUNK:2 * CHUNK, kc]
            scores = jnp.where(
                lvl == N_LEVELS,
                lax.dot_general(q.astype(BF16), k.astype(BF16), _NT, preferred_element_type=F32), 0.0)
            for l in range(N_LEVELS):
                e = jnp.exp2(dall[(2 + l) * CHUNK:(3 + l) * CHUNK, kc])
                r = lax.dot_general((q * e).astype(BF16), (k * e).astype(BF16), _NT,
                                    preferred_element_type=F32)
                scores = jnp.where(lvl == l, r, scores)
            v = v_ref[rows, vc]
            state_t = st_ref[h]
            o = (jnp.dot(scores.astype(BF16), v, preferred_element_type=F32)
                 + lax.dot_general((q * jnp.exp2(g)).astype(BF16), state_t.astype(BF16), _NT,
                                   preferred_element_type=F32))
            k_end = (k * jnp.exp2(suffix)).astype(BF16)
            st_ref[h] = (state_t * jnp.exp2(g[CHUNK - 1:CHUNK])
                         + lax.dot_general(v, k_end, _TN, preferred_element_type=F32))
            on = _rms_rows(o, gain[:, vc]) * sg_ref[rows, vc].astype(F32)
            o_ref[rows, vc] = on.astype(o_ref.dtype)
        return carry

    lax.fori_loop(0, ts // CHUNK, chunk, 0, unroll=unroll)


def _recurrence(q, k, la, v, sg, gain, *, dk, dv, nh, seq, ts=1024, unroll=2):
    t = q.shape[0]
    heads = q.shape[1] // dk
    ts = min(ts, seq)
    mall = jnp.asarray(np.concatenate([_DECAY_MATS, _DECAY_MATS], axis=1), BF16)
    lvl = jnp.asarray(_LEVEL_MAT)
    kspec = pl.BlockSpec((ts, nh * dk), lambda hg, s: (s, hg))
    vspec = pl.BlockSpec((ts, nh * dv), lambda hg, s: (s, hg))
    return pl.pallas_call(
        functools.partial(_rec_kernel, nh=nh, dk=dk, dv=dv, ts=ts, seq=seq, unroll=unroll),
        grid=(heads // nh, t // ts),
        in_specs=[kspec, kspec, kspec, vspec, vspec,
                  pl.BlockSpec((1, nh * dv), lambda hg, s: (0, hg)),
                  pl.BlockSpec(mall.shape, lambda hg, s: (0, 0)),
                  pl.BlockSpec(lvl.shape, lambda hg, s: (0, 0))],
        out_specs=vspec,
        out_shape=jax.ShapeDtypeStruct((t, heads * dv), BF16),
        scratch_shapes=[pltpu.VMEM((nh, dv, dk), F32)],
        compiler_params=_cparams(("parallel", "arbitrary")),
    )(q, k, la, v, sg, gain.reshape(1, -1), mall, lvl)


def _gelu_tanh(x):
    return jax.nn.gelu(x, approximate=True)


def _gmlp_spatial_kernel(u_ref, v_ref, lng_ref, lnb_ref, ws_ref, bs_ref, o_ref, *, groups):
    v = v_ref[...].astype(F32)
    mu = jnp.mean(v, axis=-1, keepdims=True)
    vc = v - mu
    var = jnp.mean(vc * vc, axis=-1, keepdims=True)
    vn = (vc * lax.rsqrt(var + EPS) * lng_ref[...] + lnb_ref[...]).astype(BF16)
    blk = v.shape[0]
    gw = v.shape[1] // groups
    ri = lax.broadcasted_iota(jnp.int32, (blk, blk), 0)
    ci = lax.broadcasted_iota(jnp.int32, (blk, blk), 1)
    for g in range(groups):
        cols = slice(g * gw, (g + 1) * gw)
        wc = jnp.where(ci <= ri, ws_ref[g], 0.0).astype(BF16)
        mixed = jnp.dot(wc, vn[:, cols], preferred_element_type=F32) + bs_ref[g]
        o_ref[:, cols] = (u_ref[:, cols].astype(F32) * mixed).astype(o_ref.dtype)


def _gmlp_spatial(u, v, ln_g, ln_b, w_s, b_s):
    t, w = u.shape
    groups, blk = w_s.shape[0], w_s.shape[1]
    rows = pl.BlockSpec((blk, w), lambda i: (i, 0))
    return pl.pallas_call(
        functools.partial(_gmlp_spatial_kernel, groups=groups),
        grid=(t // blk,),
        in_specs=[rows, rows,
                  pl.BlockSpec((1, w), lambda i: (0, 0)),
                  pl.BlockSpec((1, w), lambda i: (0, 0)),
                  pl.BlockSpec((groups, blk, blk), lambda i: (0, 0, 0)),
                  pl.BlockSpec((groups, blk, 1), lambda i: (0, 0, 0))],
        out_specs=rows,
        out_shape=jax.ShapeDtypeStruct((t, w), BF16),
        compiler_params=_cparams(("parallel",)),
    )(u, v, ln_g.reshape(1, w), ln_b.reshape(1, w), w_s, b_s.reshape(groups, blk, 1))


def _epi_residual(accs, extras):
    return [extras[0] + accs[0]]


def _epi_hgrn(accs, extras):
    q, f, i, g = accs
    lb = extras[0]
    e = jnp.exp(-jnp.abs(f))
    log_sig = jnp.minimum(f, 0.0) - jnp.log1p(e)
    a = jnp.log(lb)
    b = jnp.log1p(-lb) + log_sig
    log_forget = jnp.maximum(a, b) + jnp.log1p(jnp.exp(-jnp.abs(a - b)))
    key = (1.0 - lb) * (jnp.where(f >= 0.0, e, 1.0) / (1.0 + e))
    return [_silu(q), key, log_forget, i, _silu(g)]


def _epi_gelu2(accs, extras):
    return [_gelu_tanh(accs[0]), _gelu_tanh(accs[1])]


def _epi_scale(scale):
    return lambda accs, extras: [accs[0] * scale]


def _epi_identity(accs, extras):
    return [accs[0]]


def _epi_silu(accs, extras):
    return [_silu(accs[0])]


def _epi_gla_gate(accs, extras):
    z = accs[0] + extras[0]
    log_sig = jnp.minimum(z, 0.0) - jnp.log1p(jnp.exp(-jnp.abs(z)))
    return [log_sig / GLA_GATE_TAU]


def _epi_first_lanes(width):
    def epi(accs, extras):
        lane = lax.broadcasted_iota(jnp.int32, accs[0].shape, 1)
        return [jnp.where(lane < width, accs[0], 0.0)]
    return epi


def _epi_ple(accs, extras):
    h, y, p, proj = extras
    emb = jnp.dot(p.astype(BF16), proj.astype(BF16), preferred_element_type=F32)
    return [(h + y) + emb * _sigmoid(accs[0])]


def kernel(x, p, norm_mix, norm_ffn, norm_final, conv_w_in, conv_w, conv_w_out, hgrn_w_in, hgrn_lb, hgrn_norm, hgrn_w_out, gla_w_in, gla_w_a2, gla_b_a, gla_norm, gla_w_out, gmlp_w_in, gmlp_ln_g, gmlp_ln_b, gmlp_w_s, gmlp_b_s, gmlp_w_out, moe_w_group, moe_w_expert, moe_w_gate, moe_w_up, moe_w_down, ple_w_proj, ple_w_gate):
    batch, seq, d = x.shape
    depth = p.shape[0]
    t = batch * seq
    ff = moe_w_gate.shape[-1]
    n_exp = moe_w_gate.shape[1]

    lb_w = jax.nn.softmax(hgrn_lb.astype(F32), axis=0)
    lower_bounds = jnp.cumsum(lb_w, axis=0) - lb_w[0]
    w_gate_all = moe_w_gate.reshape(depth * n_exp, d, ff)
    w_up_all = moe_w_up.reshape(depth * n_exp, d, ff)
    w_down_all = moe_w_down.astype(BF16).reshape(depth * n_exp, ff, d)
    p_all = p.reshape(depth, t, p.shape[-1])

    h = x.reshape(t, d)
    for layer in range(depth):
        kind, j = layer % 4, layer // 4
        a = _rmsnorm(h, norm_mix[layer], BF16)
        if kind == 0:
            gated = _conv_in(a, conv_w_in, j, conv_w[j], seq)
            w_out = conv_w_out
        elif kind == 1:
            nb = d // 256
            qt, key, log_f, val, sg = _mm(
                a, [(hgrn_w_in, j, o * nb) for o in range(4)],
                [(lower_bounds.reshape(depth, 1, d), 'row', layer)],
                [BF16, BF16, F32, BF16, BF16], _epi_hgrn, n_cols=d, tm=1024, tn=256)
            gated = _recurrence(qt, key, log_f, val, sg, hgrn_norm[j],
                                dk=HGRN_DK, dv=HGRN_DK, nh=4, seq=seq)
            w_out = hgrn_w_out
        elif kind == 2:
            kd = d // 2
            dk = kd // GLA_HEADS
            dv = d // GLA_HEADS
            tn = 1024
            (q,) = _mm(a, [(gla_w_in, j, 0)], [], [BF16], _epi_scale(dk ** -0.5),
                       n_cols=kd, tm=512, tn=tn)
            (k,) = _mm(a, [(gla_w_in, j, kd // tn)], [], [BF16], _epi_identity,
                       n_cols=kd, tm=512, tn=tn)
            (v,) = _mm(a, [(gla_w_in, j, 2 * kd // tn)], [], [BF16], _epi_identity,
                       n_cols=d, tm=512, tn=tn)
            (sg,) = _mm(a, [(gla_w_in, j, (2 * kd + d) // tn)], [], [BF16], _epi_silu,
                        n_cols=d, tm=512, tn=tn)
            (a_low,) = _mm(a, [(gla_w_in, j, (2 * kd + 2 * d) // LANES)], [], [BF16],
                           _epi_first_lanes(GLA_GATE_RANK), n_cols=LANES, tm=512, tn=LANES)
            w_a2 = jnp.pad(gla_w_a2, ((0, 0), (0, LANES - GLA_GATE_RANK), (0, 0)))
            (log_a,) = _mm(a_low, [(w_a2, j, 0)], [(gla_b_a.reshape(-1, 1, kd), 'row', j)], [F32],
                           _epi_gla_gate, n_cols=kd, tm=1024, tn=tn)
            gated = _recurrence(q, k, log_a, v, sg, gla_norm[j], dk=dk, dv=dv, nh=1, seq=seq)
            w_out = gla_w_out
        else:
            wd = gmlp_w_in.shape[-1] // 2
            u, v = _mm(a, [(gmlp_w_in, j, 0), (gmlp_w_in, j, wd // 512)], [], [BF16, BF16], _epi_gelu2,
                       n_cols=wd, tm=1024, tn=512)
            gated = _gmlp_spatial(u, v, gmlp_ln_g[j], gmlp_ln_b[j], gmlp_w_s[j], gmlp_b_s[j])
            w_out = gmlp_w_out
        (h,) = _mm(gated, [(w_out, j, 0)], [(h, 'tile', None)], [F32], _epi_residual,
                   n_cols=d, tm=512, tn=512 if gated.shape[1] > d else 1024)

        w_router = jnp.pad(jnp.concatenate([moe_w_group[layer], moe_w_expert[layer]], axis=1),
                           ((0, 0), (0, LANES - MOE_GROUPS - MOE_GROUPS * MOE_EPG)))
        c, route = _rms_router(h, norm_ffn[layer], w_router)
        y = _moe_layer(h, route, norm_ffn[layer], w_gate_all, w_up_all, w_down_all, layer * n_exp)
        (h,) = _mm(c, [(ple_w_gate, layer, 0)],
                   [(h, 'tile', None), (y, 'tile', None), (p_all, 'rowtile', layer),
                    (ple_w_proj, 'kcol', layer)],
                   [F32], _epi_ple, n_cols=d, tm=512, tn=1024)
    out = _rmsnorm(h, norm_final, F32)
    return out.reshape(batch, seq, d)
```

```python
import functools

import numpy as np
import jax
import jax.numpy as jnp
from jax import lax
from jax.experimental import pallas as pl
from jax.experimental.pallas import tpu as pltpu

F32 = jnp.float32
BF16 = jnp.bfloat16

EPS = 1e-6
CHUNK = 128
N_LEVELS = 7
LOG2_E = 1.4426950408889634
LANES = 128
VMEM_LIMIT = 56 * 1024 * 1024

CONV_WIDTH = 3
HGRN_DK = 128
GLA_HEADS = 8
GLA_GATE_RANK = 16
GLA_GATE_TAU = 16.0
GMLP_BLOCK = 128
GMLP_GROUPS = 8
MOE_GROUPS = 4
MOE_EPG = 4
MOE_TOP_K = 2
MOE_PAIRS = 6
MOE_PAIR_A = (0, 0, 0, 1, 1, 2)
MOE_PAIR_B = (1, 2, 3, 2, 3, 3)
MOE_TILE = 256
MOE_SCATTER_SLOTS = 3


def _cparams(sem):
    return pltpu.CompilerParams(dimension_semantics=sem, vmem_limit_bytes=VMEM_LIMIT)


def _sigmoid(x):
    return jax.nn.sigmoid(x)


def _silu(x):
    return x * jax.nn.sigmoid(x)


def _mm_kernel(*refs, n_w, n_e, epilogue, kinds, rb):
    a_ref = refs[0]
    w_refs = refs[1:1 + n_w]
    e_refs = refs[1 + n_w:1 + n_w + n_e]
    o_refs = refs[1 + n_w + n_e:-1]
    wbf_ref = refs[-1]

    @pl.when(pl.program_id(1) == 0)
    def _():
        for j, w in enumerate(w_refs):
            wbf_ref[j] = w[...].astype(BF16)

    tm = a_ref.shape[0]
    for r in range(0, tm, rb):
        rows = slice(r, r + rb)
        a = a_ref[rows, :]
        accs = [jnp.dot(a, wbf_ref[j], preferred_element_type=F32) for j in range(n_w)]
        ex = [e[rows, :] if kind in ('tile', 'rowtile') else e[...] for e, kind in zip(e_refs, kinds)]
        res = epilogue(accs, ex)
        for o, val in zip(o_refs, res):
            o[rows, :] = val.astype(o.dtype)


def _mm(a, ws, extras, out_dtypes, epilogue, *, n_cols, tm, tn, rb=256):
    m, k = a.shape
    grid = (n_cols // tn, m // tm)
    in_specs = [pl.BlockSpec((tm, k), lambda n, i: (i, 0))]
    args = [a]
    for w, lead, off in ws:
        in_specs.append(pl.BlockSpec((None, k, tn), lambda n, i, lead=lead, off=off: (lead, 0, n + off),
                                     pipeline_mode=pl.Buffered(1)))
        args.append(w)
    for arr, kind, lead in extras:
        if kind == 'row':
            in_specs.append(pl.BlockSpec((None, 1, tn), lambda n, i, lead=lead: (lead, 0, n)))
        elif kind == 'tile':
            assert lead is None
            in_specs.append(pl.BlockSpec((tm, tn), lambda n, i: (i, n)))
        elif kind == 'rowtile':
            in_specs.append(pl.BlockSpec((None, tm, arr.shape[2]), lambda n, i, lead=lead: (lead, i, 0)))
        elif kind == 'kcol':
            in_specs.append(pl.BlockSpec((None, arr.shape[1], tn), lambda n, i, lead=lead: (lead, 0, n)))
        else:
            raise ValueError(kind)
        args.append(arr)
    out_shape = [jax.ShapeDtypeStruct((m, n_cols), dt) for dt in out_dtypes]
    out_specs = [pl.BlockSpec((tm, tn), lambda n, i: (i, n)) for _ in out_dtypes]
    res = pl.pallas_call(
        functools.partial(_mm_kernel, n_w=len(ws), n_e=len(extras), epilogue=epilogue,
                          kinds=tuple(kind for _, kind, _ in extras), rb=min(rb, tm)),
        grid=grid, in_specs=in_specs, out_specs=out_specs, out_shape=out_shape,
        scratch_shapes=[pltpu.VMEM((len(ws), k, tn), BF16)],
        compiler_params=_cparams(("arbitrary", "arbitrary")),
    )(*args)
    return res


def _rms_rows(x, gain):
    ms = jnp.mean(x * x, axis=-1, keepdims=True)
    return x * lax.rsqrt(ms + EPS) * gain


def _rms_kernel(h_ref, g_ref, o_ref):
    o_ref[...] = _rms_rows(h_ref[...], g_ref[...]).astype(o_ref.dtype)


def _rmsnorm(h, gain, out_dtype, tm=256):
    t, d = h.shape
    return pl.pallas_call(
        _rms_kernel, grid=(t // tm,),
        in_specs=[pl.BlockSpec((tm, d), lambda i: (i, 0)), pl.BlockSpec((1, d), lambda i: (0, 0))],
        out_specs=pl.BlockSpec((tm, d), lambda i: (i, 0)),
        out_shape=jax.ShapeDtypeStruct((t, d), out_dtype),
        compiler_params=_cparams(("parallel",)),
    )(h, gain.reshape(1, d))


def _first_max_index(vals, m):
    idx = jnp.full(m.shape, len(vals) - 1, jnp.int32)
    for j in range(len(vals) - 2, -1, -1):
        idx = jnp.where(vals[j] == m, j, idx)
    return idx


def _route_cols(logits):
    col = lambda j: logits[:, j:j + 1]
    gl = [col(j) for j in range(MOE_GROUPS)]
    gm = functools.reduce(jnp.maximum, gl)
    gsum = functools.reduce(lambda a, b: a + b, [jnp.exp(g - gm) for g in gl])
    group_p = 1.0 / gsum
    gidx = _first_max_index(gl, gm)
    ig = []
    for e in range(MOE_EPG):
        v = col(MOE_GROUPS + (MOE_GROUPS - 1) * MOE_EPG + e)
        for g in range(MOE_GROUPS - 2, -1, -1):
            v = jnp.where(gidx == g, col(MOE_GROUPS + g * MOE_EPG + e), v)
        ig.append(v)
    m1 = functools.reduce(jnp.maximum, ig)
    i1 = _first_max_index(ig, m1)
    ig2 = [jnp.where(i1 == e, -jnp.inf, ig[e]) for e in range(MOE_EPG)]
    m2 = functools.reduce(jnp.maximum, ig2)
    i2 = _first_max_index(ig2, m2)
    t = jnp.exp(m2 - m1)
    w1 = 1.0 / (1.0 + t) * group_p
    w2 = t / (1.0 + t) * group_p
    first = i1 < i2
    ea = jnp.where(first, i1, i2)
    eb = jnp.where(first, i2, i1)
    wa = jnp.where(first, w1, w2)
    wb = jnp.where(first, w2, w1)
    pair = jnp.where(ea == 0, 0, jnp.where(ea == 1, 3, 5)) + (eb - ea - 1)
    seg = gidx * MOE_PAIRS + pair
    lane = lax.broadcasted_iota(jnp.int32, logits.shape, 1)
    return jnp.where(lane == 0, wa, jnp.where(lane == 1, wb, jnp.where(lane == 2, seg.astype(F32), 0.0)))


def _rms_router_kernel(h_ref, g_ref, whi_ref, wlo_ref, c_ref, route_ref):
    y = _rms_rows(h_ref[...], g_ref[...])
    c = y.astype(BF16)
    c_ref[...] = c
    y_lo = (y - c.astype(F32)).astype(BF16)
    whi = whi_ref[...]
    logits = (jnp.dot(c, whi, preferred_element_type=F32)
              + (jnp.dot(c, wlo_ref[...], preferred_element_type=F32)
                 + jnp.dot(y_lo, whi, preferred_element_type=F32)))
    route_ref[...] = _route_cols(logits)


def _rms_router(h, gain, w_router, tm=256):
    t, d = h.shape
    w_hi = w_router.astype(BF16)
    w_lo = (w_router - w_hi.astype(F32)).astype(BF16)
    return pl.pallas_call(
        _rms_router_kernel, grid=(t // tm,),
        in_specs=[pl.BlockSpec((tm, d), lambda i: (i, 0)),
                  pl.BlockSpec((1, d), lambda i: (0, 0)),
                  pl.BlockSpec((d, LANES), lambda i: (0, 0)),
                  pl.BlockSpec((d, LANES), lambda i: (0, 0))],
        out_specs=[pl.BlockSpec((tm, d), lambda i: (i, 0)),
                   pl.BlockSpec((tm, LANES), lambda i: (i, 0))],
        out_shape=[jax.ShapeDtypeStruct((t, d), BF16),
                   jax.ShapeDtypeStruct((t, LANES), F32)],
        compiler_params=_cparams(("parallel",)),
    )(h, gain.reshape(1, d), w_hi, w_lo)


def _gather_kernel(src_ref, nu_ref, h_hbm, gain_ref, x_ref, hbuf, hsem, *, tm):
    i = pl.program_id(0)
    nu = nu_ref[0]

    def issue(tile, slot):
        def body(r, carry):
            tok = src_ref[tile * tm + r]
            pltpu.make_async_copy(h_hbm.at[pl.ds(tok, 1), :], hbuf.at[slot, pl.ds(r, 1), :],
                                  hsem.at[slot]).start()
            return carry
        lax.fori_loop(0, tm, body, 0, unroll=8)

    @pl.when(i == 0)
    def _():
        issue(0, 0)

    @pl.when(i + 1 < nu)
    def _():
        issue(i + 1, (i + 1) % 2)

    @pl.when(i < nu)
    def _():
        slot = i % 2
        pltpu.make_async_copy(h_hbm.at[pl.ds(0, tm), :], hbuf.at[slot], hsem.at[slot]).wait()
        x_ref[...] = _rms_rows(hbuf[slot], gain_ref[...]).astype(x_ref.dtype)

    @pl.when(i >= nu)
    def _():
        x_ref[...] = jnp.zeros_like(x_ref)


def _gather_sorted(h, gain, src, n_used, nt):
    t, d = h.shape
    tm = MOE_TILE
    return pl.pallas_call(
        functools.partial(_gather_kernel, tm=tm),
        grid_spec=pltpu.PrefetchScalarGridSpec(
            num_scalar_prefetch=2, grid=(nt,),
            in_specs=[pl.BlockSpec(memory_space=pl.ANY),
                      pl.BlockSpec((1, d), lambda i, src, nu: (0, 0))],
            out_specs=pl.BlockSpec((tm, d), lambda i, src, nu: (i, 0)),
            scratch_shapes=[pltpu.VMEM((2, tm, d), F32), pltpu.SemaphoreType.DMA((2,))],
        ),
        out_shape=jax.ShapeDtypeStruct((nt * tm, d), BF16),
        compiler_params=_cparams(("arbitrary",)),
    )(src, n_used, h, gain.reshape(1, d))


def _moe_up_kernel(ea_ref, eb_ref, nu_ref, x_ref, comb_ref, wg_ref, wu_ref, o_ref, wbf_ref):
    s = pl.program_id(0)
    i = pl.program_id(1)

    def expert(tile):
        return jnp.where(s == 0, ea_ref[tile], eb_ref[tile])

    @pl.when(i < nu_ref[0])
    def _():
        @pl.when((i == 0) | (expert(i) != expert(jnp.maximum(i - 1, 0))))
        def _():
            wbf_ref[0] = wg_ref[...].astype(BF16)
            wbf_ref[1] = wu_ref[...].astype(BF16)

        x = x_ref[...]
        g = jnp.dot(x, wbf_ref[0], preferred_element_type=F32)
        u = jnp.dot(x, wbf_ref[1], preferred_element_type=F32)
        comb = comb_ref[...]
        lane = lax.broadcasted_iota(jnp.int32, comb.shape, 1)
        sc = jnp.sum(jnp.where(lane == s, comb, 0.0), axis=1, keepdims=True)
        o_ref[...] = (_silu(g) * u * sc).astype(o_ref.dtype)

    @pl.when(i >= nu_ref[0])
    def _():
        o_ref[...] = jnp.zeros_like(o_ref)


def _moe_down_kernel(ea_ref, eb_ref, nu_ref, dst_ref, hid_ref, wda_ref, wdb_ref, y_hbm, obuf, wbf_ref, sem,
                     *, tm, nt, t, ff):
    i = pl.program_id(0)
    nu = nu_ref[0]
    n_slots = obuf.shape[0]

    def wait_slot(slot):
        pltpu.make_async_copy(obuf.at[slot], y_hbm.at[pl.ds(0, tm), :], sem.at[slot]).wait()

    def compute(slot):
        prev = jnp.maximum(i - 1, 0)

        @pl.when((i == 0) | (ea_ref[i] != ea_ref[prev]))
        def _():
            wbf_ref[0] = wda_ref[...].astype(BF16)

        @pl.when((i == 0) | (eb_ref[i] != eb_ref[prev]))
        def _():
            wbf_ref[1] = wdb_ref[...].astype(BF16)

        obuf[slot] = (jnp.dot(hid_ref[:, :ff], wbf_ref[0], preferred_element_type=F32)
                      + jnp.dot(hid_ref[:, ff:], wbf_ref[1], preferred_element_type=F32))

    def scatter(tile, slot):
        for r in range(tm):
            pltpu.make_async_copy(obuf.at[slot, pl.ds(r, 1), :],
                                  y_hbm.at[pl.ds(dst_ref[tile * tm + r], 1), :], sem.at[slot]).start()

    @pl.when(i == 0)
    def _():
        obuf[0] = jnp.zeros(obuf.shape[1:], F32)
        for part in range(n_slots):
            zero_copy = pltpu.make_async_copy(obuf.at[0], y_hbm.at[pl.ds(t + part * tm, tm), :], sem.at[0])
            zero_copy.start()
            zero_copy.wait()
        compute(0)

    @pl.when((i >= 1) & (i < nu))
    def _():
        slot = i % n_slots

        @pl.when(i >= n_slots)
        def _():
            wait_slot(slot)

        compute(slot)
        scatter(i - 1, (i - 1) % n_slots)

    @pl.when(i == nu)
    def _():
        scatter(nu - 1, (nu - 1) % n_slots)

    @pl.when(i == nt - 1)
    def _():
        for back in range(1, n_slots + 1):
            wait_slot((nu - back) % n_slots)


def _moe_sorted(x_sorted, comb_sorted, exp_a, exp_b, n_used, dst, t, w_gate, w_up, w_down):
    tp, d = x_sorted.shape
    ff = w_gate.shape[-1]
    tm = MOE_TILE
    nt = tp // tm
    row = lambda i, nu: jnp.minimum(i, nu[0] - 1)
    wsel = lambda s, i, ea, eb, nu: (jnp.where(s == 0, ea[i], eb[i]), 0, 0)
    hidden = pl.pallas_call(
        _moe_up_kernel,
        grid_spec=pltpu.PrefetchScalarGridSpec(
            num_scalar_prefetch=3, grid=(MOE_TOP_K, nt),
            in_specs=[
                pl.BlockSpec((tm, d), lambda s, i, ea, eb, nu: (row(i, nu), 0)),
                pl.BlockSpec((tm, comb_sorted.shape[1]), lambda s, i, ea, eb, nu: (row(i, nu), 0)),
                pl.BlockSpec((None, d, ff), wsel),
                pl.BlockSpec((None, d, ff), wsel),
            ],
            out_specs=pl.BlockSpec((tm, ff), lambda s, i, ea, eb, nu: (i, s)),
            scratch_shapes=[pltpu.VMEM((2, d, ff), BF16)],
        ),
        out_shape=jax.ShapeDtypeStruct((tp, MOE_TOP_K * ff), BF16),
        compiler_params=_cparams(("arbitrary", "arbitrary")),
    )(exp_a, exp_b, n_used, x_sorted, comb_sorted, w_gate, w_up)
    y = pl.pallas_call(
        functools.partial(_moe_down_kernel, tm=tm, nt=nt, t=t, ff=ff),
        grid_spec=pltpu.PrefetchScalarGridSpec(
            num_scalar_prefetch=4, grid=(nt,),
            in_specs=[
                pl.BlockSpec((tm, MOE_TOP_K * ff), lambda i, ea, eb, nu, dst: (row(i, nu), 0)),
                pl.BlockSpec((None, ff, d), lambda i, ea, eb, nu, dst: (ea[i], 0, 0)),
                pl.BlockSpec((None, ff, d), lambda i, ea, eb, nu, dst: (eb[i], 0, 0)),
            ],
            out_specs=pl.BlockSpec(memory_space=pl.ANY),
            scratch_shapes=[pltpu.VMEM((MOE_SCATTER_SLOTS, tm, d), F32), pltpu.VMEM((MOE_TOP_K, ff, d), BF16),
                            pltpu.SemaphoreType.DMA((MOE_SCATTER_SLOTS,))],
        ),
        out_shape=jax.ShapeDtypeStruct((t + MOE_SCATTER_SLOTS * tm, d), F32),
        compiler_params=_cparams(("arbitrary",)),
    )(exp_a, exp_b, n_used, dst, hidden, w_down, w_down)
    return y


def _moe_layer(h, route, gain, w_gate, w_up, w_down, base):
    t, d = h.shape
    tm = MOE_TILE
    n_seg = MOE_GROUPS * MOE_PAIRS
    nt = t // tm + n_seg
    seg = route[:, 2].astype(jnp.int32)
    onehot = (seg[:, None] == jnp.arange(n_seg, dtype=jnp.int32)[None, :]).astype(jnp.int32)
    csum = jnp.cumsum(onehot, axis=0)
    rank = jnp.sum(csum * onehot, axis=1) - 1
    counts = csum[-1]
    ntiles = (counts + tm - 1) // tm
    tile_end = jnp.cumsum(ntiles)
    tile_start = tile_end - ntiles
    pos = tile_start[seg] * tm + rank
    n_used = tile_end[-1:].astype(jnp.int32)
    tile_seg = jnp.minimum(
        jnp.sum((jnp.arange(nt, dtype=jnp.int32)[:, None] >= tile_end[None, :]).astype(jnp.int32), axis=1),
        n_seg - 1).astype(jnp.int32)
    tile_first = base + (tile_seg // MOE_PAIRS) * MOE_EPG
    exp_a = tile_first + jnp.asarray(MOE_PAIR_A, jnp.int32)[tile_seg % MOE_PAIRS]
    exp_b = tile_first + jnp.asarray(MOE_PAIR_B, jnp.int32)[tile_seg % MOE_PAIRS]
    token = jnp.full((nt * tm,), -1, jnp.int32).at[pos].set(jnp.arange(t, dtype=jnp.int32))
    r = jnp.arange(nt * tm, dtype=jnp.int32)
    src = jnp.where(token >= 0, token, r % t)
    dst = jnp.where(token >= 0, token, t + (r // tm % MOE_SCATTER_SLOTS) * tm + r % tm)
    x_sorted = _gather_sorted(h, gain, src, n_used, nt)
    comb_sorted = jnp.take(route[:, :8], src, axis=0)
    return _moe_sorted(x_sorted, comb_sorted, exp_a, exp_b, n_used, dst, t, w_gate, w_up, w_down)


def _conv_in_kernel(a_ref, wb_ref, wc_ref, wu_ref, cw_ref, o_ref, carry_ref, wbf_ref, *, tm, seq, rb):
    i = pl.program_id(1)

    @pl.when(i == 0)
    def _():
        wbf_ref[0] = wb_ref[...].astype(BF16)
        wbf_ref[1] = wc_ref[...].astype(BF16)
        wbf_ref[2] = wu_ref[...].astype(BF16)

    @pl.when((i * tm) % seq == 0)
    def _():
        carry_ref[...] = jnp.zeros(carry_ref.shape, F32)

    prev = carry_ref[...]
    cw = cw_ref[...]
    for r in range(0, tm, rb):
        rows = slice(r, r + rb)
        a = a_ref[rows, :]
        gate_b = jnp.dot(a, wbf_ref[0], preferred_element_type=F32)
        gate_c = jnp.dot(a, wbf_ref[1], preferred_element_type=F32)
        u = jnp.dot(a, wbf_ref[2], preferred_element_type=F32)
        z = gate_c * u
        rid = lax.broadcasted_iota(jnp.int32, z.shape, 0)
        z1 = jnp.where(rid == 0, prev[7:8], pltpu.roll(z, 1, 0))
        z2 = jnp.where(rid == 0, prev[6:7], jnp.where(rid == 1, prev[7:8], pltpu.roll(z, 2, 0)))
        conv = cw[0:1] * z2 + cw[1:2] * z1 + cw[2:3] * z
        o_ref[rows, :] = (gate_b * conv).astype(o_ref.dtype)
        prev = z[rb - 8:rb]
    carry_ref[...] = prev


def _conv_in(a, w_in, j, conv_w, seq, tm=1024, tn=256, rb=256):
    t, d = a.shape
    nb = d // tn
    return pl.pallas_call(
        functools.partial(_conv_in_kernel, tm=tm, seq=seq, rb=min(rb, tm)),
        grid=(nb, t // tm),
        in_specs=[pl.BlockSpec((tm, d), lambda n, i: (i, 0)),
                  pl.BlockSpec((None, d, tn), lambda n, i: (j, 0, n), pipeline_mode=pl.Buffered(1)),
                  pl.BlockSpec((None, d, tn), lambda n, i: (j, 0, n + nb), pipeline_mode=pl.Buffered(1)),
                  pl.BlockSpec((None, d, tn), lambda n, i: (j, 0, n + 2 * nb), pipeline_mode=pl.Buffered(1)),
                  pl.BlockSpec((CONV_WIDTH, tn), lambda n, i: (0, n))],
        out_specs=pl.BlockSpec((tm, tn), lambda n, i: (i, n)),
        out_shape=jax.ShapeDtypeStruct((t, d), BF16),
        scratch_shapes=[pltpu.VMEM((8, tn), F32), pltpu.VMEM((3, d, tn), BF16)],
        compiler_params=_cparams(("arbitrary", "arbitrary")),
    )(a, w_in, w_in, w_in, conv_w)


def _build_decay_mats():
    t = np.arange(CHUNK)
    i = t[:, None]
    tt = t[None, :]
    mats = [tt <= i, tt > i]
    for l in range(N_LEVELS):
        s = 1 << l
        ref = ((t // (2 * s)) * (2 * s) + s - 1)[:, None]
        upper = ((t % (2 * s)) >= s)[:, None]
        mats.append(np.where(upper, (tt > ref) & (tt <= i), (tt > i) & (tt <= ref)))
    return np.concatenate(mats, axis=0).astype(np.float32)


def _build_level_matrix():
    lv = np.full((CHUNK, CHUNK), -1, np.int32)
    for i in range(CHUNK):
        lv[i, i] = N_LEVELS
        for j in range(i):
            lv[i, j] = int(np.floor(np.log2(i ^ j)))
    return lv


_DECAY_MATS = _build_decay_mats()
_LEVEL_MAT = _build_level_matrix()

_NT = (((1,), (1,)), ((), ()))
_TN = (((0,), (0,)), ((), ()))


def _rec_kernel(q_ref, k_ref, la_ref, v_ref, sg_ref, gain_ref, mall_ref, lvl_ref, o_ref, st_ref,
                *, nh, dk, dv, ts, seq, unroll):
    t = pl.program_id(1)

    @pl.when((t * ts) % seq == 0)
    def _():
        st_ref[...] = jnp.zeros_like(st_ref)

    mall = mall_ref[...]
    lvl = lvl_ref[...]
    gain = gain_ref[...]

    def chunk(c, carry):
        r0 = pl.multiple_of(c * CHUNK, CHUNK)
        rows = pl.ds(r0, CHUNK)
        la = la_ref[rows, :] * LOG2_E
        la_hi = la.astype(BF16)
        la_lo = (la - la_hi.astype(F32)).astype(BF16)
        dall = jnp.dot(mall, jnp.concatenate([la_hi, la_lo], axis=0), preferred_element_type=F32)
        for h in range(nh):
            kc = slice(h * dk, (h + 1) * dk)
            vc = slice(h * dv, (h + 1) * dv)
            q = q_ref[rows, kc].astype(F32)
            k = k_ref[rows, kc].astype(F32)
            g = dall[0:CHUNK, kc]
            suffix = dall[CHUNK:2 * CHUNK, kc]
            scores = jnp.where(
                lvl == N_LEVELS,
                lax.dot_general(q.astype(BF16), k.astype(BF16), _NT, preferred_element_type=F32), 0.0)
            for l in range(N_LEVELS):
                e = jnp.exp2(dall[(2 + l) * CHUNK:(3 + l) * CHUNK, kc])
                r = lax.dot_general((q * e).astype(BF16), (k * e).astype(BF16), _NT,
                                    preferred_element_type=F32)
                scores = jnp.where(lvl == l, r, scores)
            v = v_ref[rows, vc]
            state_t = st_ref[h]
            o = (jnp.dot(scores.astype(BF16), v, preferred_element_type=F32)
                 + lax.dot_general((q * jnp.exp2(g)).astype(BF16), state_t.astype(BF16), _NT,
                                   preferred_element_type=F32))
            k_end = (k * jnp.exp2(suffix)).astype(BF16)
            st_ref[h] = (state_t * jnp.exp2(g[CHUNK - 1:CHUNK])
                         + lax.dot_general(v, k_end, _TN, preferred_element_type=F32))
            on = _rms_rows(o, gain[:, vc]) * sg_ref[rows, vc].astype(F32)
            o_ref[rows, vc] = on.astype(o_ref.dtype)
        return carry

    lax.fori_loop(0, ts // CHUNK, chunk, 0, unroll=unroll)


def _recurrence(q, k, la, v, sg, gain, *, dk, dv, nh, seq, ts=1024, unroll=2):
    t = q.shape[0]
    heads = q.shape[1] // dk
    ts = min(ts, seq)
    mall = jnp.asarray(np.concatenate([_DECAY_MATS, _DECAY_MATS], axis=1), BF16)
    lvl = jnp.asarray(_LEVEL_MAT)
    kspec = pl.BlockSpec((ts, nh * dk), lambda hg, s: (s, hg))
    vspec = pl.BlockSpec((ts, nh * dv), lambda hg, s: (s, hg))
    return pl.pallas_call(
        functools.partial(_rec_kernel, nh=nh, dk=dk, dv=dv, ts=ts, seq=seq, unroll=unroll),
        grid=(heads // nh, t // ts),
        in_specs=[kspec, kspec, kspec, vspec, vspec,
                  pl.BlockSpec((1, nh * dv), lambda hg, s: (0, hg)),
                  pl.BlockSpec(mall.shape, lambda hg, s: (0, 0)),
                  pl.BlockSpec(lvl.shape, lambda hg, s: (0, 0))],
        out_specs=vspec,
        out_shape=jax.ShapeDtypeStruct((t, heads * dv), BF16),
        scratch_shapes=[pltpu.VMEM((nh, dv, dk), F32)],
        compiler_params=_cparams(("parallel", "arbitrary")),
    )(q, k, la, v, sg, gain.reshape(1, -1), mall, lvl)


def _gelu_tanh(x):
    return jax.nn.gelu(x, approximate=True)


def _gmlp_spatial_kernel(u_ref, v_ref, lng_ref, lnb_ref, ws_ref, bs_ref, o_ref, *, groups):
    v = v_ref[...].astype(F32)
    mu = jnp.mean(v, axis=-1, keepdims=True)
    vc = v - mu
    var = jnp.mean(vc * vc, axis=-1, keepdims=True)
    vn = (vc * lax.rsqrt(var + EPS) * lng_ref[...] + lnb_ref[...]).astype(BF16)
    blk = v.shape[0]
    gw = v.shape[1] // groups
    ri = lax.broadcasted_iota(jnp.int32, (blk, blk), 0)
    ci = lax.broadcasted_iota(jnp.int32, (blk, blk), 1)
    for g in range(groups):
        cols = slice(g * gw, (g + 1) * gw)
        wc = jnp.where(ci <= ri, ws_ref[g], 0.0).astype(BF16)
        mixed = jnp.dot(wc, vn[:, cols], preferred_element_type=F32) + bs_ref[g]
        o_ref[:, cols] = (u_ref[:, cols].astype(F32) * mixed).astype(o_ref.dtype)


def _gmlp_spatial(u, v, ln_g, ln_b, w_s, b_s):
    t, w = u.shape
    groups, blk = w_s.shape[0], w_s.shape[1]
    rows = pl.BlockSpec((blk, w), lambda i: (i, 0))
    return pl.pallas_call(
        functools.partial(_gmlp_spatial_kernel, groups=groups),
        grid=(t // blk,),
        in_specs=[rows, rows,
                  pl.BlockSpec((1, w), lambda i: (0, 0)),
                  pl.BlockSpec((1, w), lambda i: (0, 0)),
                  pl.BlockSpec((groups, blk, blk), lambda i: (0, 0, 0)),
                  pl.BlockSpec((groups, blk, 1), lambda i: (0, 0, 0))],
        out_specs=rows,
        out_shape=jax.ShapeDtypeStruct((t, w), BF16),
        compiler_params=_cparams(("parallel",)),
    )(u, v, ln_g.reshape(1, w), ln_b.reshape(1, w), w_s, b_s.reshape(groups, blk, 1))


def _epi_residual(accs, extras):
    return [extras[0] + accs[0]]


def _epi_hgrn(accs, extras):
    q, f, i, g = accs
    lb = extras[0]
    e = jnp.exp(-jnp.abs(f))
    log_sig = jnp.minimum(f, 0.0) - jnp.log1p(e)
    a = jnp.log(lb)
    b = jnp.log1p(-lb) + log_sig
    log_forget = jnp.maximum(a, b) + jnp.log1p(jnp.exp(-jnp.abs(a - b)))
    key = (1.0 - lb) * (jnp.where(f >= 0.0, e, 1.0) / (1.0 + e))
    return [_silu(q), key, log_forget, i, _silu(g)]


def _epi_gelu2(accs, extras):
    return [_gelu_tanh(accs[0]), _gelu_tanh(accs[1])]


def _epi_scale(scale):
    return lambda accs, extras: [accs[0] * scale]


def _epi_identity(accs, extras):
    return [accs[0]]


def _epi_silu(accs, extras):
    return [_silu(accs[0])]


def _epi_gla_gate(accs, extras):
    z = accs[0] + extras[0]
    log_sig = jnp.minimum(z, 0.0) - jnp.log1p(jnp.exp(-jnp.abs(z)))
    return [log_sig / GLA_GATE_TAU]


def _epi_first_lanes(width):
    def epi(accs, extras):
        lane = lax.broadcasted_iota(jnp.int32, accs[0].shape, 1)
        return [jnp.where(lane < width, accs[0], 0.0)]
    return epi


def _epi_ple(accs, extras):
    h, y, p, proj = extras
    emb = jnp.dot(p.astype(BF16), proj.astype(BF16), preferred_element_type=F32)
    return [(h + y) + emb * _sigmoid(accs[0])]


def kernel(x, p, norm_mix, norm_ffn, norm_final, conv_w_in, conv_w, conv_w_out, hgrn_w_in, hgrn_lb, hgrn_norm, hgrn_w_out, gla_w_in, gla_w_a2, gla_b_a, gla_norm, gla_w_out, gmlp_w_in, gmlp_ln_g, gmlp_ln_b, gmlp_w_s, gmlp_b_s, gmlp_w_out, moe_w_group, moe_w_expert, moe_w_gate, moe_w_up, moe_w_down, ple_w_proj, ple_w_gate):
    batch, seq, d = x.shape
    depth = p.shape[0]
    t = batch * seq
    ff = moe_w_gate.shape[-1]
    n_exp = moe_w_gate.shape[1]

    lb_w = jax.nn.softmax(hgrn_lb.astype(F32), axis=0)
    lower_bounds = jnp.cumsum(lb_w, axis=0) - lb_w[0]
    w_gate_all = moe_w_gate.reshape(depth * n_exp, d, ff)
    w_up_all = moe_w_up.reshape(depth * n_exp, d, ff)
    w_down_all = moe_w_down.reshape(depth * n_exp, ff, d)
    p_all = p.reshape(depth, t, p.shape[-1])

    h = x.reshape(t, d)
    for layer in range(depth):
        kind, j = layer % 4, layer // 4
        a = _rmsnorm(h, norm_mix[layer], BF16)
        if kind == 0:
            gated = _conv_in(a, conv_w_in, j, conv_w[j], seq)
            w_out = conv_w_out
        elif kind == 1:
            nb = d // 256
            qt, key, log_f, val, sg = _mm(
                a, [(hgrn_w_in, j, o * nb) for o in range(4)],
                [(lower_bounds.reshape(depth, 1, d), 'row', layer)],
                [BF16, BF16, F32, BF16, BF16], _epi_hgrn, n_cols=d, tm=1024, tn=256)
            gated = _recurrence(qt, key, log_f, val, sg, hgrn_norm[j],
                                dk=HGRN_DK, dv=HGRN_DK, nh=4, seq=seq)
            w_out = hgrn_w_out
        elif kind == 2:
            kd = d // 2
            dk = kd // GLA_HEADS
            dv = d // GLA_HEADS
            tn = 1024
            (q,) = _mm(a, [(gla_w_in, j, 0)], [], [BF16], _epi_scale(dk ** -0.5),
                       n_cols=kd, tm=512, tn=tn)
            (k,) = _mm(a, [(gla_w_in, j, kd // tn)], [], [BF16], _epi_identity,
                       n_cols=kd, tm=512, tn=tn)
            (v,) = _mm(a, [(gla_w_in, j, 2 * kd // tn)], [], [BF16], _epi_identity,
                       n_cols=d, tm=512, tn=tn)
            (sg,) = _mm(a, [(gla_w_in, j, (2 * kd + d) // tn)], [], [BF16], _epi_silu,
                        n_cols=d, tm=512, tn=tn)
            (a_low,) = _mm(a, [(gla_w_in, j, (2 * kd + 2 * d) // LANES)], [], [BF16],
                           _epi_first_lanes(GLA_GATE_RANK), n_cols=LANES, tm=512, tn=LANES)
            w_a2 = jnp.pad(gla_w_a2, ((0, 0), (0, LANES - GLA_GATE_RANK), (0, 0)))
            (log_a,) = _mm(a_low, [(w_a2, j, 0)], [(gla_b_a.reshape(-1, 1, kd), 'row', j)], [F32],
                           _epi_gla_gate, n_cols=kd, tm=1024, tn=tn)
            gated = _recurrence(q, k, log_a, v, sg, gla_norm[j], dk=dk, dv=dv, nh=1, seq=seq)
            w_out = gla_w_out
        else:
            wd = gmlp_w_in.shape[-1] // 2
            u, v = _mm(a, [(gmlp_w_in, j, 0), (gmlp_w_in, j, wd // 512)], [], [BF16, BF16], _epi_gelu2,
                       n_cols=wd, tm=1024, tn=512)
            gated = _gmlp_spatial(u, v, gmlp_ln_g[j], gmlp_ln_b[j], gmlp_w_s[j], gmlp_b_s[j])
            w_out = gmlp_w_out
        (h,) = _mm(gated, [(w_out, j, 0)], [(h, 'tile', None)], [F32], _epi_residual,
                   n_cols=d, tm=512, tn=512 if gated.shape[1] > d else 1024)

        w_router = jnp.pad(jnp.concatenate([moe_w_group[layer], moe_w_expert[layer]], axis=1),
                           ((0, 0), (0, LANES - MOE_GROUPS - MOE_GROUPS * MOE_EPG)))
        c, route = _rms_router(h, norm_ffn[layer], w_router)
        y = _moe_layer(h, route, norm_ffn[layer], w_gate_all, w_up_all, w_down_all, layer * n_exp)
        (h,) = _mm(c, [(ple_w_gate, layer, 0)],
                   [(h, 'tile', None), (y, 'tile', None), (p_all, 'rowtile', layer),
                    (ple_w_proj, 'kcol', layer)],
                   [F32], _epi_ple, n_cols=d, tm=512, tn=1024)
    out = _rmsnorm(h, norm_final, F32)
    return out.reshape(batch, seq, d)
```

```python
import functools

import numpy as np
import jax
import jax.numpy as jnp
from jax import lax
from jax.experimental import pallas as pl
from jax.experimental.pallas import tpu as pltpu

F32 = jnp.float32
BF16 = jnp.bfloat16

EPS = 1e-6
CHUNK = 128
N_LEVELS = 7
LOG2_E = 1.4426950408889634
LANES = 128
VMEM_LIMIT = 56 * 1024 * 1024

CONV_WIDTH = 3
HGRN_DK = 128
GLA_HEADS = 8
GLA_GATE_RANK = 16
GLA_GATE_TAU = 16.0
GMLP_BLOCK = 128
GMLP_GROUPS = 8
MOE_GROUPS = 4
MOE_EPG = 4
MOE_TOP_K = 2
MOE_PAIRS = 6
MOE_PAIR_A = (0, 0, 0, 1, 1, 2)
MOE_PAIR_B = (1, 2, 3, 2, 3, 3)
MOE_TILE = 256
MOE_SCATTER_SLOTS = 3


def _cparams(sem):
    return pltpu.CompilerParams(dimension_semantics=sem, vmem_limit_bytes=VMEM_LIMIT)


def _sigmoid(x):
    return jax.nn.sigmoid(x)


def _silu(x):
    return x * jax.nn.sigmoid(x)


def _mm_kernel(*refs, n_w, n_e, epilogue, kinds, rb):
    a_ref = refs[0]
    w_refs = refs[1:1 + n_w]
    e_refs = refs[1 + n_w:1 + n_w + n_e]
    o_refs = refs[1 + n_w + n_e:-1]
    wbf_ref = refs[-1]

    @pl.when(pl.program_id(1) == 0)
    def _():
        for j, w in enumerate(w_refs):
            wbf_ref[j] = w[...].astype(BF16)

    tm = a_ref.shape[0]
    for r in range(0, tm, rb):
        rows = slice(r, r + rb)
        a = a_ref[rows, :]
        accs = [jnp.dot(a, wbf_ref[j], preferred_element_type=F32) for j in range(n_w)]
        ex = [e[rows, :] if kind in ('tile', 'rowtile') else e[...] for e, kind in zip(e_refs, kinds)]
        res = epilogue(accs, ex)
        for o, val in zip(o_refs, res):
            o[rows, :] = val.astype(o.dtype)


def _mm(a, ws, extras, out_dtypes, epilogue, *, n_cols, tm, tn, rb=256):
    m, k = a.shape
    grid = (n_cols // tn, m // tm)
    in_specs = [pl.BlockSpec((tm, k), lambda n, i: (i, 0))]
    args = [a]
    for w, lead, off in ws:
        in_specs.append(pl.BlockSpec((None, k, tn), lambda n, i, lead=lead, off=off: (lead, 0, n + off),
                                     pipeline_mode=pl.Buffered(1)))
        args.append(w)
    for arr, kind, lead in extras:
        if kind == 'row':
            in_specs.append(pl.BlockSpec((None, 1, tn), lambda n, i, lead=lead: (lead, 0, n)))
        elif kind == 'tile':
            assert lead is None
            in_specs.append(pl.BlockSpec((tm, tn), lambda n, i: (i, n)))
        elif kind == 'rowtile':
            in_specs.append(pl.BlockSpec((None, tm, arr.shape[2]), lambda n, i, lead=lead: (lead, i, 0)))
        elif kind == 'kcol':
            in_specs.append(pl.BlockSpec((None, arr.shape[1], tn), lambda n, i, lead=lead: (lead, 0, n)))
        else:
            raise ValueError(kind)
        args.append(arr)
    out_shape = [jax.ShapeDtypeStruct((m, n_cols), dt) for dt in out_dtypes]
    out_specs = [pl.BlockSpec((tm, tn), lambda n, i: (i, n)) for _ in out_dtypes]
    res = pl.pallas_call(
        functools.partial(_mm_kernel, n_w=len(ws), n_e=len(extras), epilogue=epilogue,
                          kinds=tuple(kind for _, kind, _ in extras), rb=min(rb, tm)),
        grid=grid, in_specs=in_specs, out_specs=out_specs, out_shape=out_shape,
        scratch_shapes=[pltpu.VMEM((len(ws), k, tn), BF16)],
        compiler_params=_cparams(("arbitrary", "arbitrary")),
    )(*args)
    return res


def _rms_rows(x, gain):
    ms = jnp.mean(x * x, axis=-1, keepdims=True)
    return x * lax.rsqrt(ms + EPS) * gain


def _rms_kernel(h_ref, g_ref, o_ref):
    o_ref[...] = _rms_rows(h_ref[...], g_ref[...]).astype(o_ref.dtype)


def _rmsnorm(h, gain, out_dtype, tm=256):
    t, d = h.shape
    return pl.pallas_call(
        _rms_kernel, grid=(t // tm,),
        in_specs=[pl.BlockSpec((tm, d), lambda i: (i, 0)), pl.BlockSpec((1, d), lambda i: (0, 0))],
        out_specs=pl.BlockSpec((tm, d), lambda i: (i, 0)),
        out_shape=jax.ShapeDtypeStruct((t, d), out_dtype),
        compiler_params=_cparams(("parallel",)),
    )(h, gain.reshape(1, d))


def _first_max_index(vals, m):
    idx = jnp.full(m.shape, len(vals) - 1, jnp.int32)
    for j in range(len(vals) - 2, -1, -1):
        idx = jnp.where(vals[j] == m, j, idx)
    return idx


def _route_cols(logits):
    col = lambda j: logits[:, j:j + 1]
    gl = [col(j) for j in range(MOE_GROUPS)]
    gm = functools.reduce(jnp.maximum, gl)
    gsum = functools.reduce(lambda a, b: a + b, [jnp.exp(g - gm) for g in gl])
    group_p = 1.0 / gsum
    gidx = _first_max_index(gl, gm)
    ig = []
    for e in range(MOE_EPG):
        v = col(MOE_GROUPS + (MOE_GROUPS - 1) * MOE_EPG + e)
        for g in range(MOE_GROUPS - 2, -1, -1):
            v = jnp.where(gidx == g, col(MOE_GROUPS + g * MOE_EPG + e), v)
        ig.append(v)
    m1 = functools.reduce(jnp.maximum, ig)
    i1 = _first_max_index(ig, m1)
    ig2 = [jnp.where(i1 == e, -jnp.inf, ig[e]) for e in range(MOE_EPG)]
    m2 = functools.reduce(jnp.maximum, ig2)
    i2 = _first_max_index(ig2, m2)
    t = jnp.exp(m2 - m1)
    w1 = 1.0 / (1.0 + t) * group_p
    w2 = t / (1.0 + t) * group_p
    first = i1 < i2
    ea = jnp.where(first, i1, i2)
    eb = jnp.where(first, i2, i1)
    wa = jnp.where(first, w1, w2)
    wb = jnp.where(first, w2, w1)
    pair = jnp.where(ea == 0, 0, jnp.where(ea == 1, 3, 5)) + (eb - ea - 1)
    seg = gidx * MOE_PAIRS + pair
    lane = lax.broadcasted_iota(jnp.int32, logits.shape, 1)
    return jnp.where(lane == 0, wa, jnp.where(lane == 1, wb, jnp.where(lane == 2, seg.astype(F32), 0.0)))


def _rms_router_kernel(h_ref, g_ref, whi_ref, wlo_ref, c_ref, route_ref):
    y = _rms_rows(h_ref[...], g_ref[...])
    c = y.astype(BF16)
    c_ref[...] = c
    y_lo = (y - c.astype(F32)).astype(BF16)
    whi = whi_ref[...]
    logits = (jnp.dot(c, whi, preferred_element_type=F32)
              + (jnp.dot(c, wlo_ref[...], preferred_element_type=F32)
                 + jnp.dot(y_lo, whi, preferred_element_type=F32)))
    route_ref[...] = _route_cols(logits)


def _rms_router(h, gain, w_router, tm=256):
    t, d = h.shape
    w_hi = w_router.astype(BF16)
    w_lo = (w_router - w_hi.astype(F32)).astype(BF16)
    return pl.pallas_call(
        _rms_router_kernel, grid=(t // tm,),
        in_specs=[pl.BlockSpec((tm, d), lambda i: (i, 0)),
                  pl.BlockSpec((1, d), lambda i: (0, 0)),
                  pl.BlockSpec((d, LANES), lambda i: (0, 0)),
                  pl.BlockSpec((d, LANES), lambda i: (0, 0))],
        out_specs=[pl.BlockSpec((tm, d), lambda i: (i, 0)),
                   pl.BlockSpec((tm, LANES), lambda i: (i, 0))],
        out_shape=[jax.ShapeDtypeStruct((t, d), BF16),
                   jax.ShapeDtypeStruct((t, LANES), F32)],
        compiler_params=_cparams(("parallel",)),
    )(h, gain.reshape(1, d), w_hi, w_lo)


def _gather_kernel(src_ref, nu_ref, h_hbm, gain_ref, x_ref, hbuf, hsem, *, tm):
    i = pl.program_id(0)
    nu = nu_ref[0]

    def issue(tile, slot):
        def body(r, carry):
            tok = src_ref[tile * tm + r]
            pltpu.make_async_copy(h_hbm.at[pl.ds(tok, 1), :], hbuf.at[slot, pl.ds(r, 1), :],
                                  hsem.at[slot]).start()
            return carry
        lax.fori_loop(0, tm, body, 0, unroll=8)

    @pl.when(i == 0)
    def _():
        issue(0, 0)

    @pl.when(i + 1 < nu)
    def _():
        issue(i + 1, (i + 1) % 2)

    @pl.when(i < nu)
    def _():
        slot = i % 2
        pltpu.make_async_copy(h_hbm.at[pl.ds(0, tm), :], hbuf.at[slot], hsem.at[slot]).wait()
        x_ref[...] = _rms_rows(hbuf[slot], gain_ref[...]).astype(x_ref.dtype)

    @pl.when(i >= nu)
    def _():
        x_ref[...] = jnp.zeros_like(x_ref)


def _gather_sorted(h, gain, src, n_used, nt):
    t, d = h.shape
    tm = MOE_TILE
    return pl.pallas_call(
        functools.partial(_gather_kernel, tm=tm),
        grid_spec=pltpu.PrefetchScalarGridSpec(
            num_scalar_prefetch=2, grid=(nt,),
            in_specs=[pl.BlockSpec(memory_space=pl.ANY),
                      pl.BlockSpec((1, d), lambda i, src, nu: (0, 0))],
            out_specs=pl.BlockSpec((tm, d), lambda i, src, nu: (i, 0)),
            scratch_shapes=[pltpu.VMEM((2, tm, d), F32), pltpu.SemaphoreType.DMA((2,))],
        ),
        out_shape=jax.ShapeDtypeStruct((nt * tm, d), BF16),
        compiler_params=_cparams(("arbitrary",)),
    )(src, n_used, h, gain.reshape(1, d))


def _moe_up_kernel(ea_ref, eb_ref, nu_ref, x_ref, comb_ref, wg_ref, wu_ref, o_ref, wbf_ref):
    s = pl.program_id(0)
    i = pl.program_id(1)

    def expert(tile):
        return jnp.where(s == 0, ea_ref[tile], eb_ref[tile])

    @pl.when(i < nu_ref[0])
    def _():
        @pl.when((i == 0) | (expert(i) != expert(jnp.maximum(i - 1, 0))))
        def _():
            wbf_ref[0] = wg_ref[...].astype(BF16)
            wbf_ref[1] = wu_ref[...].astype(BF16)

        x = x_ref[...]
        g = jnp.dot(x, wbf_ref[0], preferred_element_type=F32)
        u = jnp.dot(x, wbf_ref[1], preferred_element_type=F32)
        comb = comb_ref[...]
        lane = lax.broadcasted_iota(jnp.int32, comb.shape, 1)
        sc = jnp.sum(jnp.where(lane == s, comb, 0.0), axis=1, keepdims=True)
        o_ref[...] = (_silu(g) * u * sc).astype(o_ref.dtype)

    @pl.when(i >= nu_ref[0])
    def _():
        o_ref[...] = jnp.zeros_like(o_ref)


def _moe_down_kernel(ea_ref, eb_ref, nu_ref, dst_ref, hid_ref, wda_ref, wdb_ref, y_hbm, obuf, wbf_ref, sem,
                     *, tm, nt, t, ff):
    i = pl.program_id(0)
    nu = nu_ref[0]
    n_slots = obuf.shape[0]

    def wait_slot(slot):
        pltpu.make_async_copy(obuf.at[slot], y_hbm.at[pl.ds(0, tm), :], sem.at[slot]).wait()

    def compute(slot):
        prev = jnp.maximum(i - 1, 0)

        @pl.when((i == 0) | (ea_ref[i] != ea_ref[prev]))
        def _():
            wbf_ref[0] = wda_ref[...].astype(BF16)

        @pl.when((i == 0) | (eb_ref[i] != eb_ref[prev]))
        def _():
            wbf_ref[1] = wdb_ref[...].astype(BF16)

        obuf[slot] = (jnp.dot(hid_ref[:, :ff], wbf_ref[0], preferred_element_type=F32)
                      + jnp.dot(hid_ref[:, ff:], wbf_ref[1], preferred_element_type=F32))

    def scatter(tile, slot):
        for r in range(tm):
            pltpu.make_async_copy(obuf.at[slot, pl.ds(r, 1), :],
                                  y_hbm.at[pl.ds(dst_ref[tile * tm + r], 1), :], sem.at[slot]).start()

    @pl.when(i == 0)
    def _():
        obuf[0] = jnp.zeros(obuf.shape[1:], F32)
        for part in range(n_slots):
            zero_copy = pltpu.make_async_copy(obuf.at[0], y_hbm.at[pl.ds(t + part * tm, tm), :], sem.at[0])
            zero_copy.start()
            zero_copy.wait()
        compute(0)

    @pl.when((i >= 1) & (i < nu))
    def _():
        slot = i % n_slots

        @pl.when(i >= n_slots)
        def _():
            wait_slot(slot)

        compute(slot)
        scatter(i - 1, (i - 1) % n_slots)

    @pl.when(i == nu)
    def _():
        scatter(nu - 1, (nu - 1) % n_slots)

    @pl.when(i == nt - 1)
    def _():
        for back in range(1, n_slots + 1):
            wait_slot((nu - back) % n_slots)


def _moe_sorted(x_sorted, comb_sorted, exp_a, exp_b, n_used, dst, t, w_gate, w_up, w_down):
    tp, d = x_sorted.shape
    ff = w_gate.shape[-1]
    tm = MOE_TILE
    nt = tp // tm
    row = lambda i, nu: jnp.minimum(i, nu[0] - 1)
    wsel = lambda s, i, ea, eb, nu: (jnp.where(s == 0, ea[i], eb[i]), 0, 0)
    hidden = pl.pallas_call(
        _moe_up_kernel,
        grid_spec=pltpu.PrefetchScalarGridSpec(
            num_scalar_prefetch=3, grid=(MOE_TOP_K, nt),
            in_specs=[
                pl.BlockSpec((tm, d), lambda s, i, ea, eb, nu: (row(i, nu), 0)),
                pl.BlockSpec((tm, comb_sorted.shape[1]), lambda s, i, ea, eb, nu: (row(i, nu), 0)),
                pl.BlockSpec((None, d, ff), wsel),
                pl.BlockSpec((None, d, ff), wsel),
            ],
            out_specs=pl.BlockSpec((tm, ff), lambda s, i, ea, eb, nu: (i, s)),
            scratch_shapes=[pltpu.VMEM((2, d, ff), BF16)],
        ),
        out_shape=jax.ShapeDtypeStruct((tp, MOE_TOP_K * ff), BF16),
        compiler_params=_cparams(("arbitrary", "arbitrary")),
    )(exp_a, exp_b, n_used, x_sorted, comb_sorted, w_gate, w_up)
    y = pl.pallas_call(
        functools.partial(_moe_down_kernel, tm=tm, nt=nt, t=t, ff=ff),
        grid_spec=pltpu.PrefetchScalarGridSpec(
            num_scalar_prefetch=4, grid=(nt,),
            in_specs=[
                pl.BlockSpec((tm, MOE_TOP_K * ff), lambda i, ea, eb, nu, dst: (row(i, nu), 0)),
                pl.BlockSpec((None, ff, d), lambda i, ea, eb, nu, dst: (ea[i], 0, 0)),
                pl.BlockSpec((None, ff, d), lambda i, ea, eb, nu, dst: (eb[i], 0, 0)),
            ],
            out_specs=pl.BlockSpec(memory_space=pl.ANY),
            scratch_shapes=[pltpu.VMEM((MOE_SCATTER_SLOTS, tm, d), F32), pltpu.VMEM((MOE_TOP_K, ff, d), BF16),
                            pltpu.SemaphoreType.DMA((MOE_SCATTER_SLOTS,))],
        ),
        out_shape=jax.ShapeDtypeStruct((t + MOE_SCATTER_SLOTS * tm, d), F32),
        compiler_params=_cparams(("arbitrary",)),
    )(exp_a, exp_b, n_used, dst, hidden, w_down, w_down)
    return y


def _moe_layer(h, route, gain, w_gate, w_up, w_down, base):
    t, d = h.shape
    tm = MOE_TILE
    n_seg = MOE_GROUPS * MOE_PAIRS
    nt = t // tm + n_seg
    seg = route[:, 2].astype(jnp.int32)
    onehot = (seg[:, None] == jnp.arange(n_seg, dtype=jnp.int32)[None, :]).astype(jnp.int32)
    csum = jnp.cumsum(onehot, axis=0)
    rank = jnp.sum(csum * onehot, axis=1) - 1
    counts = csum[-1]
    ntiles = (counts + tm - 1) // tm
    tile_end = jnp.cumsum(ntiles)
    tile_start = tile_end - ntiles
    pos = tile_start[seg] * tm + rank
    n_used = tile_end[-1:].astype(jnp.int32)
    tile_seg = jnp.minimum(
        jnp.sum((jnp.arange(nt, dtype=jnp.int32)[:, None] >= tile_end[None, :]).astype(jnp.int32), axis=1),
        n_seg - 1).astype(jnp.int32)
    tile_first = base + (tile_seg // MOE_PAIRS) * MOE_EPG
    exp_a = tile_first + jnp.asarray(MOE_PAIR_A, jnp.int32)[tile_seg % MOE_PAIRS]
    exp_b = tile_first + jnp.asarray(MOE_PAIR_B, jnp.int32)[tile_seg % MOE_PAIRS]
    token = jnp.full((nt * tm,), -1, jnp.int32).at[pos].set(jnp.arange(t, dtype=jnp.int32))
    r = jnp.arange(nt * tm, dtype=jnp.int32)
    src = jnp.where(token >= 0, token, r % t)
    dst = jnp.where(token >= 0, token, t + (r // tm % MOE_SCATTER_SLOTS) * tm + r % tm)
    x_sorted = _gather_sorted(h, gain, src, n_used, nt)
    comb_sorted = jnp.take(route[:, :8], src, axis=0)
    return _moe_sorted(x_sorted, comb_sorted, exp_a, exp_b, n_used, dst, t, w_gate, w_up, w_down)


def _conv_in_kernel(a_ref, wb_ref, wc_ref, wu_ref, cw_ref, o_ref, carry_ref, wbf_ref, *, tm, seq, rb):
    i = pl.program_id(1)

    @pl.when(i == 0)
    def _():
        wbf_ref[0] = wb_ref[...].astype(BF16)
        wbf_ref[1] = wc_ref[...].astype(BF16)
        wbf_ref[2] = wu_ref[...].astype(BF16)

    @pl.when((i * tm) % seq == 0)
    def _():
        carry_ref[...] = jnp.zeros(carry_ref.shape, F32)

    prev = carry_ref[...]
    cw = cw_ref[...]
    for r in range(0, tm, rb):
        rows = slice(r, r + rb)
        a = a_ref[rows, :]
        gate_b = jnp.dot(a, wbf_ref[0], preferred_element_type=F32)
        gate_c = jnp.dot(a, wbf_ref[1], preferred_element_type=F32)
        u = jnp.dot(a, wbf_ref[2], preferred_element_type=F32)
        z = gate_c * u
        rid = lax.broadcasted_iota(jnp.int32, z.shape, 0)
        z1 = jnp.where(rid == 0, prev[7:8], pltpu.roll(z, 1, 0))
        z2 = jnp.where(rid == 0, prev[6:7], jnp.where(rid == 1, prev[7:8], pltpu.roll(z, 2, 0)))
        conv = cw[0:1] * z2 + cw[1:2] * z1 + cw[2:3] * z
        o_ref[rows, :] = (gate_b * conv).astype(o_ref.dtype)
        prev = z[rb - 8:rb]
    carry_ref[...] = prev


def _conv_in(a, w_in, j, conv_w, seq, tm=1024, tn=256, rb=256):
    t, d = a.shape
    nb = d // tn
    return pl.pallas_call(
        functools.partial(_conv_in_kernel, tm=tm, seq=seq, rb=min(rb, tm)),
        grid=(nb, t // tm),
        in_specs=[pl.BlockSpec((tm, d), lambda n, i: (i, 0)),
                  pl.BlockSpec((None, d, tn), lambda n, i: (j, 0, n), pipeline_mode=pl.Buffered(1)),
                  pl.BlockSpec((None, d, tn), lambda n, i: (j, 0, n + nb), pipeline_mode=pl.Buffered(1)),
                  pl.BlockSpec((None, d, tn), lambda n, i: (j, 0, n + 2 * nb), pipeline_mode=pl.Buffered(1)),
                  pl.BlockSpec((CONV_WIDTH, tn), lambda n, i: (0, n))],
        out_specs=pl.BlockSpec((tm, tn), lambda n, i: (i, n)),
        out_shape=jax.ShapeDtypeStruct((t, d), BF16),
        scratch_shapes=[pltpu.VMEM((8, tn), F32), pltpu.VMEM((3, d, tn), BF16)],
        compiler_params=_cparams(("arbitrary", "arbitrary")),
    )(a, w_in, w_in, w_in, conv_w)


def _build_decay_mats():
    t = np.arange(CHUNK)
    i = t[:, None]
    tt = t[None, :]
    mats = [tt <= i, tt > i]
    for l in range(N_LEVELS):
        s = 1 << l
        ref = ((t // (2 * s)) * (2 * s) + s - 1)[:, None]
        upper = ((t % (2 * s)) >= s)[:, None]
        mats.append(np.where(upper, (tt > ref) & (tt <= i), (tt > i) & (tt <= ref)))
    return np.concatenate(mats, axis=0).astype(np.float32)


def _build_level_matrix():
    lv = np.full((CHUNK, CHUNK), -1, np.int32)
    for i in range(CHUNK):
        lv[i, i] = N_LEVELS
        for j in range(i):
            lv[i, j] = int(np.floor(np.log2(i ^ j)))
    return lv


_DECAY_MATS = _build_decay_mats()
_LEVEL_MAT = _build_level_matrix()

FAST_BLOCK = CHUNK // 2
FAST_LIMIT = 96.0


def _build_fast_mats():
    t = np.arange(CHUNK)
    i = t[:, None]
    tt = t[None, :]
    ref = ((t // FAST_BLOCK) * FAST_BLOCK + FAST_BLOCK // 2 - 1)[:, None]
    mid = ((tt > ref) & (tt <= i)).astype(np.float32) - ((tt > i) & (tt <= ref)).astype(np.float32)
    top = _DECAY_MATS[(1 + N_LEVELS) * CHUNK:(2 + N_LEVELS) * CHUNK]
    return np.concatenate([_DECAY_MATS[:2 * CHUNK], top, mid], axis=0).astype(np.float32)


def _build_fast_level_matrix():
    i = np.arange(CHUNK)[:, None]
    j = np.arange(CHUNK)[None, :]
    same = (i // FAST_BLOCK) == (j // FAST_BLOCK)
    return np.where(same & (i >= j), 0, np.where((i // FAST_BLOCK) > (j // FAST_BLOCK), 1, -1)).astype(np.int32)


_FAST_MATS = _build_fast_mats()
_FAST_LEVEL_MAT = _build_fast_level_matrix()

_NT = (((1,), (1,)), ((), ()))
_TN = (((0,), (0,)), ((), ()))


def _rec_kernel(q_ref, k_ref, la_ref, v_ref, sg_ref, gain_ref, mall_ref, lvl_ref, mfast_ref, lvlf_ref,
                o_ref, st_ref, *, nh, dk, dv, ts, seq, unroll):
    t = pl.program_id(1)

    @pl.when((t * ts) % seq == 0)
    def _():
        st_ref[...] = jnp.zeros_like(st_ref)

    gain = gain_ref[...]

    def decays(mats, rows):
        la = la_ref[rows, :] * LOG2_E
        la_hi = la.astype(BF16)
        la_lo = (la - la_hi.astype(F32)).astype(BF16)
        return jnp.dot(mats, jnp.concatenate([la_hi, la_lo], axis=0), preferred_element_type=F32)

    def finish(h, rows, q, k, g, suffix, scores):
        vc = slice(h * dv, (h + 1) * dv)
        v = v_ref[rows, vc]
        state_t = st_ref[h]
        o = (jnp.dot(scores.astype(BF16), v, preferred_element_type=F32)
             + lax.dot_general((q * jnp.exp2(g)).astype(BF16), state_t.astype(BF16), _NT,
                               preferred_element_type=F32))
        k_end = (k * jnp.exp2(suffix)).astype(BF16)
        st_ref[h] = (state_t * jnp.exp2(g[CHUNK - 1:CHUNK])
                     + lax.dot_general(v, k_end, _TN, preferred_element_type=F32))
        on = _rms_rows(o, gain[:, vc]) * sg_ref[rows, vc].astype(F32)
        o_ref[rows, vc] = on.astype(o_ref.dtype)

    def fast_chunk(c, carry):
        rows = pl.ds(pl.multiple_of(c * CHUNK, CHUNK), CHUNK)
        lvl = lvlf_ref[...]
        dall = decays(mfast_ref[...], rows)
        for h in range(nh):
            kc = slice(h * dk, (h + 1) * dk)
            q = q_ref[rows, kc].astype(F32)
            k = k_ref[rows, kc].astype(F32)
            e_top = jnp.exp2(dall[2 * CHUNK:3 * CHUNK, kc])
            d_mid = dall[3 * CHUNK:4 * CHUNK, kc]
            r_top = lax.dot_general((q * e_top).astype(BF16), (k * e_top).astype(BF16), _NT,
                                    preferred_element_type=F32)
            r_mid = lax.dot_general((q * jnp.exp2(d_mid)).astype(BF16), (k * jnp.exp2(-d_mid)).astype(BF16),
                                    _NT, preferred_element_type=F32)
            scores = jnp.where(lvl == 1, r_top, jnp.where(lvl == 0, r_mid, 0.0))
            finish(h, rows, q, k, dall[0:CHUNK, kc], dall[CHUNK:2 * CHUNK, kc], scores)
        return carry

    def exact_chunk(c, carry):
        rows = pl.ds(pl.multiple_of(c * CHUNK, CHUNK), CHUNK)
        lvl = lvl_ref[...]
        dall = decays(mall_ref[...], rows)
        for h in range(nh):
            kc = slice(h * dk, (h + 1) * dk)
            q = q_ref[rows, kc].astype(F32)
            k = k_ref[rows, kc].astype(F32)
            scores = jnp.where(
                lvl == N_LEVELS,
                lax.dot_general(q.astype(BF16), k.astype(BF16), _NT, preferred_element_type=F32), 0.0)
            for l in range(N_LEVELS):
                e = jnp.exp2(dall[(2 + l) * CHUNK:(3 + l) * CHUNK, kc])
                r = lax.dot_general((q * e).astype(BF16), (k * e).astype(BF16), _NT,
                                    preferred_element_type=F32)
                scores = jnp.where(lvl == l, r, scores)
            finish(h, rows, q, k, dall[0:CHUNK, kc], dall[CHUNK:2 * CHUNK, kc], scores)
        return carry

    half = FAST_BLOCK // 2
    mass = jnp.sum(jnp.abs(la_ref[...]).reshape(ts // half, half, nh * dk), axis=1)
    safe = jnp.max(mass) * LOG2_E < FAST_LIMIT

    @pl.when(safe)
    def _():
        lax.fori_loop(0, ts // CHUNK, fast_chunk, 0, unroll=2 * unroll)

    @pl.when(jnp.logical_not(safe))
    def _():
        lax.fori_loop(0, ts // CHUNK, exact_chunk, 0, unroll=unroll)


def _recurrence(q, k, la, v, sg, gain, *, dk, dv, nh, seq, ts=1024, unroll=2):
    t = q.shape[0]
    heads = q.shape[1] // dk
    ts = min(ts, seq)
    dup = lambda m: jnp.asarray(np.concatenate([m, m], axis=1), BF16)
    mall, mfast = dup(_DECAY_MATS), dup(_FAST_MATS)
    lvl, lvlf = jnp.asarray(_LEVEL_MAT), jnp.asarray(_FAST_LEVEL_MAT)
    kspec = pl.BlockSpec((ts, nh * dk), lambda hg, s: (s, hg))
    vspec = pl.BlockSpec((ts, nh * dv), lambda hg, s: (s, hg))
    const = lambda arr: pl.BlockSpec(arr.shape, lambda hg, s: (0, 0))
    return pl.pallas_call(
        functools.partial(_rec_kernel, nh=nh, dk=dk, dv=dv, ts=ts, seq=seq, unroll=unroll),
        grid=(heads // nh, t // ts),
        in_specs=[kspec, kspec, kspec, vspec, vspec,
                  pl.BlockSpec((1, nh * dv), lambda hg, s: (0, hg)),
                  const(mall), const(lvl), const(mfast), const(lvlf)],
        out_specs=vspec,
        out_shape=jax.ShapeDtypeStruct((t, heads * dv), BF16),
        scratch_shapes=[pltpu.VMEM((nh, dv, dk), F32)],
        compiler_params=_cparams(("parallel", "arbitrary")),
    )(q, k, la, v, sg, gain.reshape(1, -1), mall, lvl, mfast, lvlf)


def _gelu_tanh(x):
    return jax.nn.gelu(x, approximate=True)


def _gmlp_spatial_kernel(u_ref, v_ref, lng_ref, lnb_ref, ws_ref, bs_ref, o_ref, *, groups):
    v = v_ref[...].astype(F32)
    mu = jnp.mean(v, axis=-1, keepdims=True)
    vc = v - mu
    var = jnp.mean(vc * vc, axis=-1, keepdims=True)
    vn = (vc * lax.rsqrt(var + EPS) * lng_ref[...] + lnb_ref[...]).astype(BF16)
    blk = v.shape[0]
    gw = v.shape[1] // groups
    ri = lax.broadcasted_iota(jnp.int32, (blk, blk), 0)
    ci = lax.broadcasted_iota(jnp.int32, (blk, blk), 1)
    for g in range(groups):
        cols = slice(g * gw, (g + 1) * gw)
        wc = jnp.where(ci <= ri, ws_ref[g], 0.0).astype(BF16)
        mixed = jnp.dot(wc, vn[:, cols], preferred_element_type=F32) + bs_ref[g]
        o_ref[:, cols] = (u_ref[:, cols].astype(F32) * mixed).astype(o_ref.dtype)


def _gmlp_spatial(u, v, ln_g, ln_b, w_s, b_s):
    t, w = u.shape
    groups, blk = w_s.shape[0], w_s.shape[1]
    rows = pl.BlockSpec((blk, w), lambda i: (i, 0))
    return pl.pallas_call(
        functools.partial(_gmlp_spatial_kernel, groups=groups),
        grid=(t // blk,),
        in_specs=[rows, rows,
                  pl.BlockSpec((1, w), lambda i: (0, 0)),
                  pl.BlockSpec((1, w), lambda i: (0, 0)),
                  pl.BlockSpec((groups, blk, blk), lambda i: (0, 0, 0)),
                  pl.BlockSpec((groups, blk, 1), lambda i: (0, 0, 0))],
        out_specs=rows,
        out_shape=jax.ShapeDtypeStruct((t, w), BF16),
        compiler_params=_cparams(("parallel",)),
    )(u, v, ln_g.reshape(1, w), ln_b.reshape(1, w), w_s, b_s.reshape(groups, blk, 1))


def _epi_residual(accs, extras):
    return [extras[0] + accs[0]]


def _epi_hgrn(accs, extras):
    q, f, i, g = accs
    lb = extras[0]
    e = jnp.exp(-jnp.abs(f))
    log_sig = jnp.minimum(f, 0.0) - jnp.log1p(e)
    a = jnp.log(lb)
    b = jnp.log1p(-lb) + log_sig
    log_forget = jnp.maximum(a, b) + jnp.log1p(jnp.exp(-jnp.abs(a - b)))
    key = (1.0 - lb) * (jnp.where(f >= 0.0, e, 1.0) / (1.0 + e))
    return [_silu(q), key, log_forget, i, _silu(g)]


def _epi_gelu2(accs, extras):
    return [_gelu_tanh(accs[0]), _gelu_tanh(accs[1])]


def _epi_scale(scale):
    return lambda accs, extras: [accs[0] * scale]


def _epi_identity(accs, extras):
    return [accs[0]]


def _epi_silu(accs, extras):
    return [_silu(accs[0])]


def _epi_gla_gate(accs, extras):
    z = accs[0] + extras[0]
    log_sig = jnp.minimum(z, 0.0) - jnp.log1p(jnp.exp(-jnp.abs(z)))
    return [log_sig / GLA_GATE_TAU]


def _epi_first_lanes(width):
    def epi(accs, extras):
        lane = lax.broadcasted_iota(jnp.int32, accs[0].shape, 1)
        return [jnp.where(lane < width, accs[0], 0.0)]
    return epi


def _epi_ple(accs, extras):
    h, y, p, proj = extras
    emb = jnp.dot(p.astype(BF16), proj.astype(BF16), preferred_element_type=F32)
    return [(h + y) + emb * _sigmoid(accs[0])]


def kernel(x, p, norm_mix, norm_ffn, norm_final, conv_w_in, conv_w, conv_w_out, hgrn_w_in, hgrn_lb, hgrn_norm, hgrn_w_out, gla_w_in, gla_w_a2, gla_b_a, gla_norm, gla_w_out, gmlp_w_in, gmlp_ln_g, gmlp_ln_b, gmlp_w_s, gmlp_b_s, gmlp_w_out, moe_w_group, moe_w_expert, moe_w_gate, moe_w_up, moe_w_down, ple_w_proj, ple_w_gate):
    batch, seq, d = x.shape
    depth = p.shape[0]
    t = batch * seq
    ff = moe_w_gate.shape[-1]
    n_exp = moe_w_gate.shape[1]

    lb_w = jax.nn.softmax(hgrn_lb.astype(F32), axis=0)
    lower_bounds = jnp.cumsum(lb_w, axis=0) - lb_w[0]
    w_gate_all = moe_w_gate.reshape(depth * n_exp, d, ff)
    w_up_all = moe_w_up.reshape(depth * n_exp, d, ff)
    w_down_all = moe_w_down.reshape(depth * n_exp, ff, d)
    p_all = p.reshape(depth, t, p.shape[-1])

    h = x.reshape(t, d)
    for layer in range(depth):
        kind, j = layer % 4, layer // 4
        a = _rmsnorm(h, norm_mix[layer], BF16)
        if kind == 0:
            gated = _conv_in(a, conv_w_in, j, conv_w[j], seq)
            w_out = conv_w_out
        elif kind == 1:
            nb = d // 256
            qt, key, log_f, val, sg = _mm(
                a, [(hgrn_w_in, j, o * nb) for o in range(4)],
                [(lower_bounds.reshape(depth, 1, d), 'row', layer)],
                [BF16, BF16, F32, BF16, BF16], _epi_hgrn, n_cols=d, tm=1024, tn=256)
            gated = _recurrence(qt, key, log_f, val, sg, hgrn_norm[j],
                                dk=HGRN_DK, dv=HGRN_DK, nh=4, seq=seq)
            w_out = hgrn_w_out
        elif kind == 2:
            kd = d // 2
            dk = kd // GLA_HEADS
            dv = d // GLA_HEADS
            tn = 1024
            (q,) = _mm(a, [(gla_w_in, j, 0)], [], [BF16], _epi_scale(dk ** -0.5),
                       n_cols=kd, tm=512, tn=tn)
            (k,) = _mm(a, [(gla_w_in, j, kd // tn)], [], [BF16], _epi_identity,
                       n_cols=kd, tm=512, tn=tn)
            (v,) = _mm(a, [(gla_w_in, j, 2 * kd // tn)], [], [BF16], _epi_identity,
                       n_cols=d, tm=512, tn=tn)
            (sg,) = _mm(a, [(gla_w_in, j, (2 * kd + d) // tn)], [], [BF16], _epi_silu,
                        n_cols=d, tm=512, tn=tn)
            (a_low,) = _mm(a, [(gla_w_in, j, (2 * kd + 2 * d) // LANES)], [], [BF16],
                           _epi_first_lanes(GLA_GATE_RANK), n_cols=LANES, tm=512, tn=LANES)
            w_a2 = jnp.pad(gla_w_a2, ((0, 0), (0, LANES - GLA_GATE_RANK), (0, 0)))
            (log_a,) = _mm(a_low, [(w_a2, j, 0)], [(gla_b_a.reshape(-1, 1, kd), 'row', j)], [F32],
                           _epi_gla_gate, n_cols=kd, tm=1024, tn=tn)
            gated = _recurrence(q, k, log_a, v, sg, gla_norm[j], dk=dk, dv=dv, nh=1, seq=seq)
            w_out = gla_w_out
        else:
            wd = gmlp_w_in.shape[-1] // 2
            u, v = _mm(a, [(gmlp_w_in, j, 0), (gmlp_w_in, j, wd // 512)], [], [BF16, BF16], _epi_gelu2,
                       n_cols=wd, tm=1024, tn=512)
            gated = _gmlp_spatial(u, v, gmlp_ln_g[j], gmlp_ln_b[j], gmlp_w_s[j], gmlp_b_s[j])
            w_out = gmlp_w_out
        (h,) = _mm(gated, [(w_out, j, 0)], [(h, 'tile', None)], [F32], _epi_residual,
                   n_cols=d, tm=512, tn=512 if gated.shape[1] > d else 1024)

        w_router = jnp.pad(jnp.concatenate([moe_w_group[layer], moe_w_expert[layer]], axis=1),
                           ((0, 0), (0, LANES - MOE_GROUPS - MOE_GROUPS * MOE_EPG)))
        c, route = _rms_router(h, norm_ffn[layer], w_router)
        y = _moe_layer(h, route, norm_ffn[layer], w_gate_all, w_up_all, w_down_all, layer * n_exp)
        (h,) = _mm(c, [(ple_w_gate, layer, 0)],
                   [(h, 'tile', None), (y, 'tile', None), (p_all, 'rowtile', layer),
                    (ple_w_proj, 'kcol', layer)],
                   [F32], _epi_ple, n_cols=d, tm=512, tn=1024)
    out = _rmsnorm(h, norm_final, F32)
    return out.reshape(batch, seq, d)
```

```python
import functools

import numpy as np
import jax
import jax.numpy as jnp
from jax import lax
from jax.experimental import pallas as pl
from jax.experimental.pallas import tpu as pltpu

F32 = jnp.float32
BF16 = jnp.bfloat16

EPS = 1e-6
CHUNK = 128
N_LEVELS = 7
LOG2_E = 1.4426950408889634
LANES = 128
VMEM_LIMIT = 56 * 1024 * 1024

CONV_WIDTH = 3
HGRN_DK = 128
GLA_HEADS = 8
GLA_GATE_RANK = 16
GLA_GATE_TAU = 16.0
GMLP_BLOCK = 128
GMLP_GROUPS = 8
MOE_GROUPS = 4
MOE_EPG = 4
MOE_TOP_K = 2
MOE_PAIRS = 6
MOE_PAIR_A = (0, 0, 0, 1, 1, 2)
MOE_PAIR_B = (1, 2, 3, 2, 3, 3)
MOE_TILE = 256
MOE_SCATTER_SLOTS = 3


def _cparams(sem):
    return pltpu.CompilerParams(dimension_semantics=sem, vmem_limit_bytes=VMEM_LIMIT)


def _sigmoid(x):
    return jax.nn.sigmoid(x)


def _silu(x):
    return x * jax.nn.sigmoid(x)


def _mm_kernel(*refs, n_w, n_e, epilogue, kinds, rb):
    a_ref = refs[0]
    w_refs = refs[1:1 + n_w]
    e_refs = refs[1 + n_w:1 + n_w + n_e]
    o_refs = refs[1 + n_w + n_e:-1]
    wbf_ref = refs[-1]

    @pl.when(pl.program_id(1) == 0)
    def _():
        for j, w in enumerate(w_refs):
            wbf_ref[j] = w[...].astype(BF16)

    tm = a_ref.shape[0]
    for r in range(0, tm, rb):
        rows = slice(r, r + rb)
        a = a_ref[rows, :]
        accs = [jnp.dot(a, wbf_ref[j], preferred_element_type=F32) for j in range(n_w)]
        ex = [e[rows, :] if kind in ('tile', 'rowtile') else e[...] for e, kind in zip(e_refs, kinds)]
        res = epilogue(accs, ex)
        for o, val in zip(o_refs, res):
            o[rows, :] = val.astype(o.dtype)


def _mm(a, ws, extras, out_dtypes, epilogue, *, n_cols, tm, tn, rb=128):
    m, k = a.shape
    grid = (n_cols // tn, m // tm)
    in_specs = [pl.BlockSpec((tm, k), lambda n, i: (i, 0))]
    args = [a]
    for w, lead, off in ws:
        in_specs.append(pl.BlockSpec((None, k, tn), lambda n, i, lead=lead, off=off: (lead, 0, n + off),
                                     pipeline_mode=pl.Buffered(1)))
        args.append(w)
    for arr, kind, lead in extras:
        if kind == 'row':
            in_specs.append(pl.BlockSpec((None, 1, tn), lambda n, i, lead=lead: (lead, 0, n)))
        elif kind == 'tile':
            assert lead is None
            in_specs.append(pl.BlockSpec((tm, tn), lambda n, i: (i, n)))
        elif kind == 'rowtile':
            in_specs.append(pl.BlockSpec((None, tm, arr.shape[2]), lambda n, i, lead=lead: (lead, i, 0)))
        elif kind == 'kcol':
            in_specs.append(pl.BlockSpec((None, arr.shape[1], tn), lambda n, i, lead=lead: (lead, 0, n)))
        else:
            raise ValueError(kind)
        args.append(arr)
    out_shape = [jax.ShapeDtypeStruct((m, n_cols), dt) for dt in out_dtypes]
    out_specs = [pl.BlockSpec((tm, tn), lambda n, i: (i, n)) for _ in out_dtypes]
    res = pl.pallas_call(
        functools.partial(_mm_kernel, n_w=len(ws), n_e=len(extras), epilogue=epilogue,
                          kinds=tuple(kind for _, kind, _ in extras), rb=min(rb, tm)),
        grid=grid, in_specs=in_specs, out_specs=out_specs, out_shape=out_shape,
        scratch_shapes=[pltpu.VMEM((len(ws), k, tn), BF16)],
        compiler_params=_cparams(("arbitrary", "arbitrary")),
    )(*args)
    return res


def _rms_rows(x, gain):
    ms = jnp.mean(x * x, axis=-1, keepdims=True)
    return x * lax.rsqrt(ms + EPS) * gain


def _rms_kernel(h_ref, g_ref, o_ref):
    o_ref[...] = _rms_rows(h_ref[...], g_ref[...]).astype(o_ref.dtype)


def _rmsnorm(h, gain, out_dtype, tm=256):
    t, d = h.shape
    return pl.pallas_call(
        _rms_kernel, grid=(t // tm,),
        in_specs=[pl.BlockSpec((tm, d), lambda i: (i, 0)), pl.BlockSpec((1, d), lambda i: (0, 0))],
        out_specs=pl.BlockSpec((tm, d), lambda i: (i, 0)),
        out_shape=jax.ShapeDtypeStruct((t, d), out_dtype),
        compiler_params=_cparams(("parallel",)),
    )(h, gain.reshape(1, d))


def _first_max_index(vals, m):
    idx = jnp.full(m.shape, len(vals) - 1, jnp.int32)
    for j in range(len(vals) - 2, -1, -1):
        idx = jnp.where(vals[j] == m, j, idx)
    return idx


def _route_cols(logits):
    col = lambda j: logits[:, j:j + 1]
    gl = [col(j) for j in range(MOE_GROUPS)]
    gm = functools.reduce(jnp.maximum, gl)
    gsum = functools.reduce(lambda a, b: a + b, [jnp.exp(g - gm) for g in gl])
    group_p = 1.0 / gsum
    gidx = _first_max_index(gl, gm)
    ig = []
    for e in range(MOE_EPG):
        v = col(MOE_GROUPS + (MOE_GROUPS - 1) * MOE_EPG + e)
        for g in range(MOE_GROUPS - 2, -1, -1):
            v = jnp.where(gidx == g, col(MOE_GROUPS + g * MOE_EPG + e), v)
        ig.append(v)
    m1 = functools.reduce(jnp.maximum, ig)
    i1 = _first_max_index(ig, m1)
    ig2 = [jnp.where(i1 == e, -jnp.inf, ig[e]) for e in range(MOE_EPG)]
    m2 = functools.reduce(jnp.maximum, ig2)
    i2 = _first_max_index(ig2, m2)
    t = jnp.exp(m2 - m1)
    w1 = 1.0 / (1.0 + t) * group_p
    w2 = t / (1.0 + t) * group_p
    first = i1 < i2
    ea = jnp.where(first, i1, i2)
    eb = jnp.where(first, i2, i1)
    wa = jnp.where(first, w1, w2)
    wb = jnp.where(first, w2, w1)
    pair = jnp.where(ea == 0, 0, jnp.where(ea == 1, 3, 5)) + (eb - ea - 1)
    seg = gidx * MOE_PAIRS + pair
    lane = lax.broadcasted_iota(jnp.int32, logits.shape, 1)
    return jnp.where(lane == 0, wa, jnp.where(lane == 1, wb, jnp.where(lane == 2, seg.astype(F32), 0.0)))


def _rms_router_kernel(h_ref, g_ref, whi_ref, wlo_ref, c_ref, route_ref, counts_ref, carry_ref):
    @pl.when(pl.program_id(0) == 0)
    def _():
        carry_ref[...] = jnp.zeros_like(carry_ref)

    y = _rms_rows(h_ref[...], g_ref[...])
    c = y.astype(BF16)
    c_ref[...] = c
    y_lo = (y - c.astype(F32)).astype(BF16)
    whi = whi_ref[...]
    logits = (jnp.dot(c, whi, preferred_element_type=F32)
              + (jnp.dot(c, wlo_ref[...], preferred_element_type=F32)
                 + jnp.dot(y_lo, whi, preferred_element_type=F32)))
    route = _route_cols(logits)

    tm = route.shape[0]
    lane = lax.broadcasted_iota(jnp.int32, route.shape, 1)
    onehot = (lane == route[:, 2:3].astype(jnp.int32)).astype(F32)
    ri = lax.broadcasted_iota(jnp.int32, (tm, tm), 0)
    ci = lax.broadcasted_iota(jnp.int32, (tm, tm), 1)
    earlier = jnp.where(ci < ri, 1.0, 0.0).astype(BF16)
    before = jnp.dot(earlier, onehot.astype(BF16), preferred_element_type=F32) + carry_ref[0:1]
    rank = jnp.sum(onehot * before, axis=1, keepdims=True)
    route_ref[...] = jnp.where(lane == 3, rank, route)
    carry_ref[...] = carry_ref[...] + jnp.sum(onehot, axis=0, keepdims=True)
    counts_ref[...] = carry_ref[...]


def _rms_router(h, gain, w_router, tm=512):
    t, d = h.shape
    w_hi = w_router.astype(BF16)
    w_lo = (w_router - w_hi.astype(F32)).astype(BF16)
    return pl.pallas_call(
        _rms_router_kernel, grid=(t // tm,),
        in_specs=[pl.BlockSpec((tm, d), lambda i: (i, 0)),
                  pl.BlockSpec((1, d), lambda i: (0, 0)),
                  pl.BlockSpec((d, LANES), lambda i: (0, 0)),
                  pl.BlockSpec((d, LANES), lambda i: (0, 0))],
        out_specs=[pl.BlockSpec((tm, d), lambda i: (i, 0)),
                   pl.BlockSpec((tm, LANES), lambda i: (i, 0)),
                   pl.BlockSpec((8, LANES), lambda i: (0, 0))],
        out_shape=[jax.ShapeDtypeStruct((t, d), BF16),
                   jax.ShapeDtypeStruct((t, LANES), F32),
                   jax.ShapeDtypeStruct((8, LANES), F32)],
        scratch_shapes=[pltpu.VMEM((8, LANES), F32)],
        compiler_params=_cparams(("arbitrary",)),
    )(h, gain.reshape(1, d), w_hi, w_lo)


def _gather_kernel(src_ref, nu_ref, h_hbm, gain_ref, x_ref, hbuf, hsem, *, tm):
    i = pl.program_id(0)
    nu = nu_ref[0]

    def issue(tile, slot):
        def body(r, carry):
            tok = src_ref[tile * tm + r]
            pltpu.make_async_copy(h_hbm.at[pl.ds(tok, 1), :], hbuf.at[slot, pl.ds(r, 1), :],
                                  hsem.at[slot]).start()
            return carry
        lax.fori_loop(0, tm, body, 0, unroll=8)

    @pl.when(i == 0)
    def _():
        issue(0, 0)

    @pl.when(i + 1 < nu)
    def _():
        issue(i + 1, (i + 1) % 2)

    @pl.when(i < nu)
    def _():
        slot = i % 2
        pltpu.make_async_copy(h_hbm.at[pl.ds(0, tm), :], hbuf.at[slot], hsem.at[slot]).wait()
        x_ref[...] = _rms_rows(hbuf[slot], gain_ref[...]).astype(x_ref.dtype)

    @pl.when(i >= nu)
    def _():
        x_ref[...] = jnp.zeros_like(x_ref)


def _gather_sorted(h, gain, src, n_used, nt):
    t, d = h.shape
    tm = MOE_TILE
    return pl.pallas_call(
        functools.partial(_gather_kernel, tm=tm),
        grid_spec=pltpu.PrefetchScalarGridSpec(
            num_scalar_prefetch=2, grid=(nt,),
            in_specs=[pl.BlockSpec(memory_space=pl.ANY),
                      pl.BlockSpec((1, d), lambda i, src, nu: (0, 0))],
            out_specs=pl.BlockSpec((tm, d), lambda i, src, nu: (i, 0)),
            scratch_shapes=[pltpu.VMEM((2, tm, d), F32), pltpu.SemaphoreType.DMA((2,))],
        ),
        out_shape=jax.ShapeDtypeStruct((nt * tm, d), BF16),
        compiler_params=_cparams(("arbitrary",)),
    )(src, n_used, h, gain.reshape(1, d))


def _moe_up_kernel(ea_ref, eb_ref, nu_ref, x_ref, comb_ref, wg_ref, wu_ref, o_ref, wbf_ref):
    s = pl.program_id(0)
    i = pl.program_id(1)

    def expert(tile):
        return jnp.where(s == 0, ea_ref[tile], eb_ref[tile])

    @pl.when(i < nu_ref[0])
    def _():
        @pl.when((i == 0) | (expert(i) != expert(jnp.maximum(i - 1, 0))))
        def _():
            wbf_ref[0] = wg_ref[...].astype(BF16)
            wbf_ref[1] = wu_ref[...].astype(BF16)

        x = x_ref[...]
        g = jnp.dot(x, wbf_ref[0], preferred_element_type=F32)
        u = jnp.dot(x, wbf_ref[1], preferred_element_type=F32)
        comb = comb_ref[...]
        lane = lax.broadcasted_iota(jnp.int32, comb.shape, 1)
        sc = jnp.sum(jnp.where(lane == s, comb, 0.0), axis=1, keepdims=True)
        o_ref[...] = (_silu(g) * u * sc).astype(o_ref.dtype)

    @pl.when(i >= nu_ref[0])
    def _():
        o_ref[...] = jnp.zeros_like(o_ref)


def _moe_down_kernel(ea_ref, eb_ref, nu_ref, dst_ref, hid_ref, wda_ref, wdb_ref, y_hbm, obuf, wbf_ref, sem,
                     *, tm, nt, t, ff):
    i = pl.program_id(0)
    nu = nu_ref[0]
    n_slots = obuf.shape[0]

    def wait_slot(slot):
        pltpu.make_async_copy(obuf.at[slot], y_hbm.at[pl.ds(0, tm), :], sem.at[slot]).wait()

    def compute(slot):
        prev = jnp.maximum(i - 1, 0)

        @pl.when((i == 0) | (ea_ref[i] != ea_ref[prev]))
        def _():
            wbf_ref[0] = wda_ref[...].astype(BF16)

        @pl.when((i == 0) | (eb_ref[i] != eb_ref[prev]))
        def _():
            wbf_ref[1] = wdb_ref[...].astype(BF16)

        obuf[slot] = (jnp.dot(hid_ref[:, :ff], wbf_ref[0], preferred_element_type=F32)
                      + jnp.dot(hid_ref[:, ff:], wbf_ref[1], preferred_element_type=F32))

    def scatter(tile, slot):
        for r in range(tm):
            pltpu.make_async_copy(obuf.at[slot, pl.ds(r, 1), :],
                                  y_hbm.at[pl.ds(dst_ref[tile * tm + r], 1), :], sem.at[slot]).start()

    @pl.when(i == 0)
    def _():
        obuf[0] = jnp.zeros(obuf.shape[1:], F32)
        for part in range(n_slots):
            zero_copy = pltpu.make_async_copy(obuf.at[0], y_hbm.at[pl.ds(t + part * tm, tm), :], sem.at[0])
            zero_copy.start()
            zero_copy.wait()
        compute(0)

    @pl.when((i >= 1) & (i < nu))
    def _():
        slot = i % n_slots

        @pl.when(i >= n_slots)
        def _():
            wait_slot(slot)

        compute(slot)
        scatter(i - 1, (i - 1) % n_slots)

    @pl.when(i == nu)
    def _():
        scatter(nu - 1, (nu - 1) % n_slots)

    @pl.when(i == nt - 1)
    def _():
        for back in range(1, n_slots + 1):
            wait_slot((nu - back) % n_slots)


def _moe_sorted(x_sorted, comb_sorted, exp_a, exp_b, n_used, dst, t, w_gate, w_up, w_down):
    tp, d = x_sorted.shape
    ff = w_gate.shape[-1]
    tm = MOE_TILE
    nt = tp // tm
    row = lambda i, nu: jnp.minimum(i, nu[0] - 1)
    wsel = lambda s, i, ea, eb, nu: (jnp.where(s == 0, ea[i], eb[i]), 0, 0)
    hidden = pl.pallas_call(
        _moe_up_kernel,
        grid_spec=pltpu.PrefetchScalarGridSpec(
            num_scalar_prefetch=3, grid=(MOE_TOP_K, nt),
            in_specs=[
                pl.BlockSpec((tm, d), lambda s, i, ea, eb, nu: (row(i, nu), 0)),
                pl.BlockSpec((tm, comb_sorted.shape[1]), lambda s, i, ea, eb, nu: (row(i, nu), 0)),
                pl.BlockSpec((None, d, ff), wsel),
                pl.BlockSpec((None, d, ff), wsel),
            ],
            out_specs=pl.BlockSpec((tm, ff), lambda s, i, ea, eb, nu: (i, s)),
            scratch_shapes=[pltpu.VMEM((2, d, ff), BF16)],
        ),
        out_shape=jax.ShapeDtypeStruct((tp, MOE_TOP_K * ff), BF16),
        compiler_params=_cparams(("arbitrary", "arbitrary")),
    )(exp_a, exp_b, n_used, x_sorted, comb_sorted, w_gate, w_up)
    y = pl.pallas_call(
        functools.partial(_moe_down_kernel, tm=tm, nt=nt, t=t, ff=ff),
        grid_spec=pltpu.PrefetchScalarGridSpec(
            num_scalar_prefetch=4, grid=(nt,),
            in_specs=[
                pl.BlockSpec((tm, MOE_TOP_K * ff), lambda i, ea, eb, nu, dst: (row(i, nu), 0)),
                pl.BlockSpec((None, ff, d), lambda i, ea, eb, nu, dst: (ea[i], 0, 0)),
                pl.BlockSpec((None, ff, d), lambda i, ea, eb, nu, dst: (eb[i], 0, 0)),
            ],
            out_specs=pl.BlockSpec(memory_space=pl.ANY),
            scratch_shapes=[pltpu.VMEM((MOE_SCATTER_SLOTS, tm, d), F32), pltpu.VMEM((MOE_TOP_K, ff, d), BF16),
                            pltpu.SemaphoreType.DMA((MOE_SCATTER_SLOTS,))],
        ),
        out_shape=jax.ShapeDtypeStruct((t + MOE_SCATTER_SLOTS * tm, d), F32),
        compiler_params=_cparams(("arbitrary",)),
    )(exp_a, exp_b, n_used, dst, hidden, w_down, w_down)
    return y


def _moe_layer(h, route, seg_counts, gain, w_gate, w_up, w_down, base):
    t, d = h.shape
    tm = MOE_TILE
    n_seg = MOE_GROUPS * MOE_PAIRS
    nt = t // tm + n_seg
    seg = route[:, 2].astype(jnp.int32)
    rank = route[:, 3].astype(jnp.int32)
    counts = seg_counts[0, :n_seg].astype(jnp.int32)
    ntiles = (counts + tm - 1) // tm
    tile_end = jnp.cumsum(ntiles)
    tile_start = tile_end - ntiles
    pos = tile_start[seg] * tm + rank
    n_used = tile_end[-1:].astype(jnp.int32)
    tile_seg = jnp.minimum(
        jnp.sum((jnp.arange(nt, dtype=jnp.int32)[:, None] >= tile_end[None, :]).astype(jnp.int32), axis=1),
        n_seg - 1).astype(jnp.int32)
    tile_first = base + (tile_seg // MOE_PAIRS) * MOE_EPG
    exp_a = tile_first + jnp.asarray(MOE_PAIR_A, jnp.int32)[tile_seg % MOE_PAIRS]
    exp_b = tile_first + jnp.asarray(MOE_PAIR_B, jnp.int32)[tile_seg % MOE_PAIRS]
    token = jnp.full((nt * tm,), -1, jnp.int32).at[pos].set(jnp.arange(t, dtype=jnp.int32))
    r = jnp.arange(nt * tm, dtype=jnp.int32)
    src = jnp.where(token >= 0, token, r % t)
    dst = jnp.where(token >= 0, token, t + (r // tm % MOE_SCATTER_SLOTS) * tm + r % tm)
    x_sorted = _gather_sorted(h, gain, src, n_used, nt)
    comb_sorted = jnp.take(route[:, :8], src, axis=0)
    return _moe_sorted(x_sorted, comb_sorted, exp_a, exp_b, n_used, dst, t, w_gate, w_up, w_down)


def _conv_in_kernel(a_ref, wb_ref, wc_ref, wu_ref, cw_ref, o_ref, carry_ref, wbf_ref, *, tm, seq, rb):
    i = pl.program_id(1)

    @pl.when(i == 0)
    def _():
        wbf_ref[0] = wb_ref[...].astype(BF16)
        wbf_ref[1] = wc_ref[...].astype(BF16)
        wbf_ref[2] = wu_ref[...].astype(BF16)

    @pl.when((i * tm) % seq == 0)
    def _():
        carry_ref[...] = jnp.zeros(carry_ref.shape, F32)

    prev = carry_ref[...]
    cw = cw_ref[...]
    for r in range(0, tm, rb):
        rows = slice(r, r + rb)
        a = a_ref[rows, :]
        gate_b = jnp.dot(a, wbf_ref[0], preferred_element_type=F32)
        gate_c = jnp.dot(a, wbf_ref[1], preferred_element_type=F32)
        u = jnp.dot(a, wbf_ref[2], preferred_element_type=F32)
        z = gate_c * u
        rid = lax.broadcasted_iota(jnp.int32, z.shape, 0)
        z1 = jnp.where(rid == 0, prev[7:8], pltpu.roll(z, 1, 0))
        z2 = jnp.where(rid == 0, prev[6:7], jnp.where(rid == 1, prev[7:8], pltpu.roll(z, 2, 0)))
        conv = cw[0:1] * z2 + cw[1:2] * z1 + cw[2:3] * z
        o_ref[rows, :] = (gate_b * conv).astype(o_ref.dtype)
        prev = z[rb - 8:rb]
    carry_ref[...] = prev


def _conv_in(a, w_in, j, conv_w, seq, tm=1024, tn=256, rb=256):
    t, d = a.shape
    nb = d // tn
    return pl.pallas_call(
        functools.partial(_conv_in_kernel, tm=tm, seq=seq, rb=min(rb, tm)),
        grid=(nb, t // tm),
        in_specs=[pl.BlockSpec((tm, d), lambda n, i: (i, 0)),
                  pl.BlockSpec((None, d, tn), lambda n, i: (j, 0, n)),
                  pl.BlockSpec((None, d, tn), lambda n, i: (j, 0, n + nb)),
                  pl.BlockSpec((None, d, tn), lambda n, i: (j, 0, n + 2 * nb)),
                  pl.BlockSpec((CONV_WIDTH, tn), lambda n, i: (0, n))],
        out_specs=pl.BlockSpec((tm, tn), lambda n, i: (i, n)),
        out_shape=jax.ShapeDtypeStruct((t, d), BF16),
        scratch_shapes=[pltpu.VMEM((8, tn), F32), pltpu.VMEM((3, d, tn), BF16)],
        compiler_params=_cparams(("arbitrary", "arbitrary")),
    )(a, w_in, w_in, w_in, conv_w)


def _build_decay_mats():
    t = np.arange(CHUNK)
    i = t[:, None]
    tt = t[None, :]
    mats = [tt <= i, tt > i]
    for l in range(N_LEVELS):
        s = 1 << l
        ref = ((t // (2 * s)) * (2 * s) + s - 1)[:, None]
        upper = ((t % (2 * s)) >= s)[:, None]
        mats.append(np.where(upper, (tt > ref) & (tt <= i), (tt > i) & (tt <= ref)))
    return np.concatenate(mats, axis=0).astype(np.float32)


def _build_level_matrix():
    lv = np.full((CHUNK, CHUNK), -1, np.int32)
    for i in range(CHUNK):
        lv[i, i] = N_LEVELS
        for j in range(i):
            lv[i, j] = int(np.floor(np.log2(i ^ j)))
    return lv


_DECAY_MATS = _build_decay_mats()
_LEVEL_MAT = _build_level_matrix()

FAST_BLOCK = CHUNK // 2
FAST_LIMIT = 96.0


def _build_fast_mats():
    t = np.arange(CHUNK)
    i = t[:, None]
    tt = t[None, :]
    ref = ((t // FAST_BLOCK) * FAST_BLOCK + FAST_BLOCK // 2 - 1)[:, None]
    mid = ((tt > ref) & (tt <= i)).astype(np.float32) - ((tt > i) & (tt <= ref)).astype(np.float32)
    top = _DECAY_MATS[(1 + N_LEVELS) * CHUNK:(2 + N_LEVELS) * CHUNK]
    return np.concatenate([_DECAY_MATS[:2 * CHUNK], top, mid], axis=0).astype(np.float32)


def _build_fast_level_matrix():
    i = np.arange(CHUNK)[:, None]
    j = np.arange(CHUNK)[None, :]
    same = (i // FAST_BLOCK) == (j // FAST_BLOCK)
    return np.where(same & (i >= j), 0, np.where((i // FAST_BLOCK) > (j // FAST_BLOCK), 1, -1)).astype(np.int32)


_FAST_MATS = _build_fast_mats()
_FAST_LEVEL_MAT = _build_fast_level_matrix()

_NT = (((1,), (1,)), ((), ()))
_TN = (((0,), (0,)), ((), ()))


def _rec_kernel(q_ref, k_ref, la_ref, v_ref, sg_ref, gain_ref, mall_ref, lvl_ref, mfast_ref, lvlf_ref,
                o_ref, st_ref, *, nh, dk, dv, ts, seq, unroll):
    t = pl.program_id(1)

    @pl.when((t * ts) % seq == 0)
    def _():
        st_ref[...] = jnp.zeros_like(st_ref)

    gain = gain_ref[...]

    def decays(mats, rows):
        la = la_ref[rows, :] * LOG2_E
        la_hi = la.astype(BF16)
        la_lo = (la - la_hi.astype(F32)).astype(BF16)
        return jnp.dot(mats, jnp.concatenate([la_hi, la_lo], axis=0), preferred_element_type=F32)

    def finish(h, rows, q, k, g, suffix, scores):
        vc = slice(h * dv, (h + 1) * dv)
        v = v_ref[rows, vc]
        state_t = st_ref[h]
        o = (jnp.dot(scores.astype(BF16), v, preferred_element_type=F32)
             + lax.dot_general((q * jnp.exp2(g)).astype(BF16), state_t.astype(BF16), _NT,
                               preferred_element_type=F32))
        k_end = (k * jnp.exp2(suffix)).astype(BF16)
        st_ref[h] = (state_t * jnp.exp2(g[CHUNK - 1:CHUNK])
                     + lax.dot_general(v, k_end, _TN, preferred_element_type=F32))
        on = _rms_rows(o, gain[:, vc]) * sg_ref[rows, vc].astype(F32)
        o_ref[rows, vc] = on.astype(o_ref.dtype)

    def fast_chunk(c, carry):
        rows = pl.ds(pl.multiple_of(c * CHUNK, CHUNK), CHUNK)
        lvl = lvlf_ref[...]
        dall = decays(mfast_ref[...], rows)
        for h in range(nh):
            kc = slice(h * dk, (h + 1) * dk)
            q = q_ref[rows, kc].astype(F32)
            k = k_ref[rows, kc].astype(F32)
            e_top = jnp.exp2(dall[2 * CHUNK:3 * CHUNK, kc])
            d_mid = dall[3 * CHUNK:4 * CHUNK, kc]
            r_top = lax.dot_general((q * e_top).astype(BF16), (k * e_top).astype(BF16), _NT,
                                    preferred_element_type=F32)
            r_mid = lax.dot_general((q * jnp.exp2(d_mid)).astype(BF16), (k * jnp.exp2(-d_mid)).astype(BF16),
                                    _NT, preferred_element_type=F32)
            scores = jnp.where(lvl == 1, r_top, jnp.where(lvl == 0, r_mid, 0.0))
            finish(h, rows, q, k, dall[0:CHUNK, kc], dall[CHUNK:2 * CHUNK, kc], scores)
        return carry

    def exact_chunk(c, carry):
        rows = pl.ds(pl.multiple_of(c * CHUNK, CHUNK), CHUNK)
        lvl = lvl_ref[...]
        dall = decays(mall_ref[...], rows)
        for h in range(nh):
            kc = slice(h * dk, (h + 1) * dk)
            q = q_ref[rows, kc].astype(F32)
            k = k_ref[rows, kc].astype(F32)
            scores = jnp.where(
                lvl == N_LEVELS,
                lax.dot_general(q.astype(BF16), k.astype(BF16), _NT, preferred_element_type=F32), 0.0)
            for l in range(N_LEVELS):
                e = jnp.exp2(dall[(2 + l) * CHUNK:(3 + l) * CHUNK, kc])
                r = lax.dot_general((q * e).astype(BF16), (k * e).astype(BF16), _NT,
                                    preferred_element_type=F32)
                scores = jnp.where(lvl == l, r, scores)
            finish(h, rows, q, k, dall[0:CHUNK, kc], dall[CHUNK:2 * CHUNK, kc], scores)
        return carry

    half = FAST_BLOCK // 2
    mass = jnp.sum(jnp.abs(la_ref[...]).reshape(ts // half, half, nh * dk), axis=1)
    safe = jnp.max(mass) * LOG2_E < FAST_LIMIT

    @pl.when(safe)
    def _():
        lax.fori_loop(0, ts // CHUNK, fast_chunk, 0, unroll=2 * unroll)

    @pl.when(jnp.logical_not(safe))
    def _():
        lax.fori_loop(0, ts // CHUNK, exact_chunk, 0, unroll=unroll)


def _recurrence(q, k, la, v, sg, gain, *, dk, dv, nh, seq, ts=1024, unroll=2):
    t = q.shape[0]
    heads = q.shape[1] // dk
    ts = min(ts, seq)
    dup = lambda m: jnp.asarray(np.concatenate([m, m], axis=1), BF16)
    mall, mfast = dup(_DECAY_MATS), dup(_FAST_MATS)
    lvl, lvlf = jnp.asarray(_LEVEL_MAT), jnp.asarray(_FAST_LEVEL_MAT)
    kspec = pl.BlockSpec((ts, nh * dk), lambda hg, s: (s, hg))
    vspec = pl.BlockSpec((ts, nh * dv), lambda hg, s: (s, hg))
    const = lambda arr: pl.BlockSpec(arr.shape, lambda hg, s: (0, 0))
    return pl.pallas_call(
        functools.partial(_rec_kernel, nh=nh, dk=dk, dv=dv, ts=ts, seq=seq, unroll=unroll),
        grid=(heads // nh, t // ts),
        in_specs=[kspec, kspec, kspec, vspec, vspec,
                  pl.BlockSpec((1, nh * dv), lambda hg, s: (0, hg)),
                  const(mall), const(lvl), const(mfast), const(lvlf)],
        out_specs=vspec,
        out_shape=jax.ShapeDtypeStruct((t, heads * dv), BF16),
        scratch_shapes=[pltpu.VMEM((nh, dv, dk), F32)],
        compiler_params=_cparams(("parallel", "arbitrary")),
    )(q, k, la, v, sg, gain.reshape(1, -1), mall, lvl, mfast, lvlf)


def _gelu_tanh(x):
    return jax.nn.gelu(x, approximate=True)


def _gmlp_spatial_kernel(u_ref, v_ref, lng_ref, lnb_ref, ws_ref, bs_ref, o_ref, *, groups):
    v = v_ref[...].astype(F32)
    mu = jnp.mean(v, axis=-1, keepdims=True)
    vc = v - mu
    var = jnp.mean(vc * vc, axis=-1, keepdims=True)
    vn = (vc * lax.rsqrt(var + EPS) * lng_ref[...] + lnb_ref[...]).astype(BF16)
    blk = v.shape[0]
    gw = v.shape[1] // groups
    ri = lax.broadcasted_iota(jnp.int32, (blk, blk), 0)
    ci = lax.broadcasted_iota(jnp.int32, (blk, blk), 1)
    for g in range(groups):
        cols = slice(g * gw, (g + 1) * gw)
        wc = jnp.where(ci <= ri, ws_ref[g], 0.0).astype(BF16)
        mixed = jnp.dot(wc, vn[:, cols], preferred_element_type=F32) + bs_ref[g]
        o_ref[:, cols] = (u_ref[:, cols].astype(F32) * mixed).astype(o_ref.dtype)


def _gmlp_spatial(u, v, ln_g, ln_b, w_s, b_s):
    t, w = u.shape
    groups, blk = w_s.shape[0], w_s.shape[1]
    rows = pl.BlockSpec((blk, w), lambda i: (i, 0))
    return pl.pallas_call(
        functools.partial(_gmlp_spatial_kernel, groups=groups),
        grid=(t // blk,),
        in_specs=[rows, rows,
                  pl.BlockSpec((1, w), lambda i: (0, 0)),
                  pl.BlockSpec((1, w), lambda i: (0, 0)),
                  pl.BlockSpec((groups, blk, blk), lambda i: (0, 0, 0)),
                  pl.BlockSpec((groups, blk, 1), lambda i: (0, 0, 0))],
        out_specs=rows,
        out_shape=jax.ShapeDtypeStruct((t, w), BF16),
        compiler_params=_cparams(("parallel",)),
    )(u, v, ln_g.reshape(1, w), ln_b.reshape(1, w), w_s, b_s.reshape(groups, blk, 1))


def _epi_residual(accs, extras):
    return [extras[0] + accs[0]]


def _epi_hgrn(accs, extras):
    q, f, i, g = accs
    lb = extras[0]
    e = jnp.exp(-jnp.abs(f))
    log_sig = jnp.minimum(f, 0.0) - jnp.log1p(e)
    a = jnp.log(lb)
    b = jnp.log1p(-lb) + log_sig
    log_forget = jnp.maximum(a, b) + jnp.log1p(jnp.exp(-jnp.abs(a - b)))
    key = (1.0 - lb) * (jnp.where(f >= 0.0, e, 1.0) / (1.0 + e))
    return [_silu(q), key, log_forget, i, _silu(g)]


def _epi_gelu2(accs, extras):
    return [_gelu_tanh(accs[0]), _gelu_tanh(accs[1])]


def _epi_scale(scale):
    return lambda accs, extras: [accs[0] * scale]


def _epi_identity(accs, extras):
    return [accs[0]]


def _epi_silu(accs, extras):
    return [_silu(accs[0])]


def _epi_gla_gate(accs, extras):
    z = accs[0] + extras[0]
    log_sig = jnp.minimum(z, 0.0) - jnp.log1p(jnp.exp(-jnp.abs(z)))
    return [log_sig / GLA_GATE_TAU]


def _epi_first_lanes(width):
    def epi(accs, extras):
        lane = lax.broadcasted_iota(jnp.int32, accs[0].shape, 1)
        return [jnp.where(lane < width, accs[0], 0.0)]
    return epi


def _epi_ple(accs, extras):
    h, y, p, proj = extras
    emb = jnp.dot(p.astype(BF16), proj.astype(BF16), preferred_element_type=F32)
    return [(h + y) + emb * _sigmoid(accs[0])]


def kernel(x, p, norm_mix, norm_ffn, norm_final, conv_w_in, conv_w, conv_w_out, hgrn_w_in, hgrn_lb, hgrn_norm, hgrn_w_out, gla_w_in, gla_w_a2, gla_b_a, gla_norm, gla_w_out, gmlp_w_in, gmlp_ln_g, gmlp_ln_b, gmlp_w_s, gmlp_b_s, gmlp_w_out, moe_w_group, moe_w_expert, moe_w_gate, moe_w_up, moe_w_down, ple_w_proj, ple_w_gate):
    batch, seq, d = x.shape
    depth = p.shape[0]
    t = batch * seq
    ff = moe_w_gate.shape[-1]
    n_exp = moe_w_gate.shape[1]

    lb_w = jax.nn.softmax(hgrn_lb.astype(F32), axis=0)
    lower_bounds = jnp.cumsum(lb_w, axis=0) - lb_w[0]
    w_gate_all = moe_w_gate.reshape(depth * n_exp, d, ff)
    w_up_all = moe_w_up.reshape(depth * n_exp, d, ff)
    w_down_all = moe_w_down.reshape(depth * n_exp, ff, d)
    p_all = p.reshape(depth, t, p.shape[-1])

    h = x.reshape(t, d)
    for layer in range(depth):
        kind, j = layer % 4, layer // 4
        a = _rmsnorm(h, norm_mix[layer], BF16)
        if kind == 0:
            gated = _conv_in(a, conv_w_in, j, conv_w[j], seq)
            w_out = conv_w_out
        elif kind == 1:
            nb = d // 256
            qt, key, log_f, val, sg = _mm(
                a, [(hgrn_w_in, j, o * nb) for o in range(4)],
                [(lower_bounds.reshape(depth, 1, d), 'row', layer)],
                [BF16, BF16, F32, BF16, BF16], _epi_hgrn, n_cols=d, tm=1024, tn=256)
            gated = _recurrence(qt, key, log_f, val, sg, hgrn_norm[j],
                                dk=HGRN_DK, dv=HGRN_DK, nh=4, seq=seq)
            w_out = hgrn_w_out
        elif kind == 2:
            kd = d // 2
            dk = kd // GLA_HEADS
            dv = d // GLA_HEADS
            tn = 1024
            (q,) = _mm(a, [(gla_w_in, j, 0)], [], [BF16], _epi_scale(dk ** -0.5),
                       n_cols=kd, tm=512, tn=tn)
            (k,) = _mm(a, [(gla_w_in, j, kd // tn)], [], [BF16], _epi_identity,
                       n_cols=kd, tm=512, tn=tn)
            (v,) = _mm(a, [(gla_w_in, j, 2 * kd // tn)], [], [BF16], _epi_identity,
                       n_cols=d, tm=512, tn=tn)
            (sg,) = _mm(a, [(gla_w_in, j, (2 * kd + d) // tn)], [], [BF16], _epi_silu,
                        n_cols=d, tm=512, tn=tn)
            (a_low,) = _mm(a, [(gla_w_in, j, (2 * kd + 2 * d) // LANES)], [], [BF16],
                           _epi_first_lanes(GLA_GATE_RANK), n_cols=LANES, tm=512, tn=LANES)
            w_a2 = jnp.pad(gla_w_a2, ((0, 0), (0, LANES - GLA_GATE_RANK), (0, 0)))
            (log_a,) = _mm(a_low, [(w_a2, j, 0)], [(gla_b_a.reshape(-1, 1, kd), 'row', j)], [F32],
                           _epi_gla_gate, n_cols=kd, tm=1024, tn=tn)
            gated = _recurrence(q, k, log_a, v, sg, gla_norm[j], dk=dk, dv=dv, nh=1, seq=seq)
            w_out = gla_w_out
        else:
            wd = gmlp_w_in.shape[-1] // 2
            u, v = _mm(a, [(gmlp_w_in, j, 0), (gmlp_w_in, j, wd // 512)], [], [BF16, BF16], _epi_gelu2,
                       n_cols=wd, tm=1024, tn=512)
            gated = _gmlp_spatial(u, v, gmlp_ln_g[j], gmlp_ln_b[j], gmlp_w_s[j], gmlp_b_s[j])
            w_out = gmlp_w_out
        (h,) = _mm(gated, [(w_out, j, 0)], [(h, 'tile', None)], [F32], _epi_residual,
                   n_cols=d, tm=512, tn=512 if gated.shape[1] > d else 1024)

        w_router = jnp.pad(jnp.concatenate([moe_w_group[layer], moe_w_expert[layer]], axis=1),
                           ((0, 0), (0, LANES - MOE_GROUPS - MOE_GROUPS * MOE_EPG)))
        c, route, seg_counts = _rms_router(h, norm_ffn[layer], w_router)
        y = _moe_layer(h, route, seg_counts, norm_ffn[layer], w_gate_all, w_up_all, w_down_all, layer * n_exp)
        (h,) = _mm(c, [(ple_w_gate, layer, 0)],
                   [(h, 'tile', None), (y, 'tile', None), (p_all, 'rowtile', layer),
                    (ple_w_proj, 'kcol', layer)],
                   [F32], _epi_ple, n_cols=d, tm=512, tn=1024)
    out = _rmsnorm(h, norm_final, F32)
    return out.reshape(batch, seq, d)
```

```python
import functools

import numpy as np
import jax
import jax.numpy as jnp
from jax import lax
from jax.experimental import pallas as pl
from jax.experimental.pallas import tpu as pltpu

F32 = jnp.float32
BF16 = jnp.bfloat16

EPS = 1e-6
CHUNK = 128
N_LEVELS = 7
LOG2_E = 1.4426950408889634
LANES = 128
VMEM_LIMIT = 56 * 1024 * 1024

CONV_WIDTH = 3
HGRN_DK = 128
GLA_HEADS = 8
GLA_GATE_RANK = 16
GLA_GATE_TAU = 16.0
GMLP_BLOCK = 128
GMLP_GROUPS = 8
MOE_GROUPS = 4
MOE_EPG = 4
MOE_TOP_K = 2
MOE_PAIRS = 6
MOE_PAIR_A = (0, 0, 0, 1, 1, 2)
MOE_PAIR_B = (1, 2, 3, 2, 3, 3)
MOE_TILE = 256
MOE_SCATTER_SLOTS = 3


def _cparams(sem):
    return pltpu.CompilerParams(dimension_semantics=sem, vmem_limit_bytes=VMEM_LIMIT)


def _sigmoid(x):
    return jax.nn.sigmoid(x)


def _silu(x):
    return x * jax.nn.sigmoid(x)


def _mm_kernel(*refs, n_w, n_e, epilogue, kinds, rb):
    a_ref = refs[0]
    w_refs = refs[1:1 + n_w]
    e_refs = refs[1 + n_w:1 + n_w + n_e]
    o_refs = refs[1 + n_w + n_e:-1]
    wbf_ref = refs[-1]

    @pl.when(pl.program_id(1) == 0)
    def _():
        for j, w in enumerate(w_refs):
            wbf_ref[j] = w[...].astype(BF16)

    tm = a_ref.shape[0]
    for r in range(0, tm, rb):
        rows = slice(r, r + rb)
        a = a_ref[rows, :]
        accs = [jnp.dot(a, wbf_ref[j], preferred_element_type=F32) for j in range(n_w)]
        ex = [e[rows, :] if kind in ('tile', 'rowtile') else e[...] for e, kind in zip(e_refs, kinds)]
        res = epilogue(accs, ex)
        for o, val in zip(o_refs, res):
            o[rows, :] = val.astype(o.dtype)


def _mm(a, ws, extras, out_dtypes, epilogue, *, n_cols, tm, tn, rb=256):
    m, k = a.shape
    grid = (n_cols // tn, m // tm)
    in_specs = [pl.BlockSpec((tm, k), lambda n, i: (i, 0))]
    args = [a]
    for w, lead, off in ws:
        in_specs.append(pl.BlockSpec((None, k, tn), lambda n, i, lead=lead, off=off: (lead, 0, n + off),
                                     pipeline_mode=pl.Buffered(1)))
        args.append(w)
    for arr, kind, lead in extras:
        if kind == 'row':
            in_specs.append(pl.BlockSpec((None, 1, tn), lambda n, i, lead=lead: (lead, 0, n)))
        elif kind == 'tile':
            assert lead is None
            in_specs.append(pl.BlockSpec((tm, tn), lambda n, i: (i, n)))
        elif kind == 'rowtile':
            in_specs.append(pl.BlockSpec((None, tm, arr.shape[2]), lambda n, i, lead=lead: (lead, i, 0)))
        elif kind == 'kcol':
            in_specs.append(pl.BlockSpec((None, arr.shape[1], tn), lambda n, i, lead=lead: (lead, 0, n)))
        else:
            raise ValueError(kind)
        args.append(arr)
    out_shape = [jax.ShapeDtypeStruct((m, n_cols), dt) for dt in out_dtypes]
    out_specs = [pl.BlockSpec((tm, tn), lambda n, i: (i, n)) for _ in out_dtypes]
    res = pl.pallas_call(
        functools.partial(_mm_kernel, n_w=len(ws), n_e=len(extras), epilogue=epilogue,
                          kinds=tuple(kind for _, kind, _ in extras), rb=min(rb, tm)),
        grid=grid, in_specs=in_specs, out_specs=out_specs, out_shape=out_shape,
        scratch_shapes=[pltpu.VMEM((len(ws), k, tn), BF16)],
        compiler_params=_cparams(("arbitrary", "arbitrary")),
    )(*args)
    return res


def _rms_rows(x, gain):
    ms = jnp.mean(x * x, axis=-1, keepdims=True)
    return x * lax.rsqrt(ms + EPS) * gain


def _rms_kernel(h_ref, g_ref, o_ref):
    o_ref[...] = _rms_rows(h_ref[...], g_ref[...]).astype(o_ref.dtype)


def _rmsnorm(h, gain, out_dtype, tm=256):
    t, d = h.shape
    return pl.pallas_call(
        _rms_kernel, grid=(t // tm,),
        in_specs=[pl.BlockSpec((tm, d), lambda i: (i, 0)), pl.BlockSpec((1, d), lambda i: (0, 0))],
        out_specs=pl.BlockSpec((tm, d), lambda i: (i, 0)),
        out_shape=jax.ShapeDtypeStruct((t, d), out_dtype),
        compiler_params=_cparams(("parallel",)),
    )(h, gain.reshape(1, d))


def _first_max_index(vals, m):
    idx = jnp.full(m.shape, len(vals) - 1, jnp.int32)
    for j in range(len(vals) - 2, -1, -1):
        idx = jnp.where(vals[j] == m, j, idx)
    return idx


def _route_cols(logits):
    col = lambda j: logits[:, j:j + 1]
    gl = [col(j) for j in range(MOE_GROUPS)]
    gm = functools.reduce(jnp.maximum, gl)
    gsum = functools.reduce(lambda a, b: a + b, [jnp.exp(g - gm) for g in gl])
    group_p = 1.0 / gsum
    gidx = _first_max_index(gl, gm)
    ig = []
    for e in range(MOE_EPG):
        v = col(MOE_GROUPS + (MOE_GROUPS - 1) * MOE_EPG + e)
        for g in range(MOE_GROUPS - 2, -1, -1):
            v = jnp.where(gidx == g, col(MOE_GROUPS + g * MOE_EPG + e), v)
        ig.append(v)
    m1 = functools.reduce(jnp.maximum, ig)
    i1 = _first_max_index(ig, m1)
    ig2 = [jnp.where(i1 == e, -jnp.inf, ig[e]) for e in range(MOE_EPG)]
    m2 = functools.reduce(jnp.maximum, ig2)
    i2 = _first_max_index(ig2, m2)
    t = jnp.exp(m2 - m1)
    w1 = 1.0 / (1.0 + t) * group_p
    w2 = t / (1.0 + t) * group_p
    first = i1 < i2
    ea = jnp.where(first, i1, i2)
    eb = jnp.where(first, i2, i1)
    wa = jnp.where(first, w1, w2)
    wb = jnp.where(first, w2, w1)
    pair = jnp.where(ea == 0, 0, jnp.where(ea == 1, 3, 5)) + (eb - ea - 1)
    seg = gidx * MOE_PAIRS + pair
    lane = lax.broadcasted_iota(jnp.int32, logits.shape, 1)
    return jnp.where(lane == 0, wa, jnp.where(lane == 1, wb, jnp.where(lane == 2, seg.astype(F32), 0.0)))


def _rms_router_kernel(h_ref, g_ref, whi_ref, wlo_ref, c_ref, route_ref, counts_ref, carry_ref):
    @pl.when(pl.program_id(0) == 0)
    def _():
        carry_ref[...] = jnp.zeros_like(carry_ref)

    y = _rms_rows(h_ref[...], g_ref[...])
    c = y.astype(BF16)
    c_ref[...] = c
    y_lo = (y - c.astype(F32)).astype(BF16)
    whi = whi_ref[...]
    logits = (jnp.dot(c, whi, preferred_element_type=F32)
              + (jnp.dot(c, wlo_ref[...], preferred_element_type=F32)
                 + jnp.dot(y_lo, whi, preferred_element_type=F32)))
    route = _route_cols(logits)

    tm = route.shape[0]
    lane = lax.broadcasted_iota(jnp.int32, route.shape, 1)
    onehot = (lane == route[:, 2:3].astype(jnp.int32)).astype(F32)
    ri = lax.broadcasted_iota(jnp.int32, (tm, tm), 0)
    ci = lax.broadcasted_iota(jnp.int32, (tm, tm), 1)
    earlier = jnp.where(ci < ri, 1.0, 0.0).astype(BF16)
    before = jnp.dot(earlier, onehot.astype(BF16), preferred_element_type=F32) + carry_ref[0:1]
    rank = jnp.sum(onehot * before, axis=1, keepdims=True)
    route_ref[...] = jnp.where(lane == 3, rank, route)
    carry_ref[...] = carry_ref[...] + jnp.sum(onehot, axis=0, keepdims=True)
    counts_ref[...] = carry_ref[...]


def _rms_router(h, gain, w_router, tm=512):
    t, d = h.shape
    w_hi = w_router.astype(BF16)
    w_lo = (w_router - w_hi.astype(F32)).astype(BF16)
    return pl.pallas_call(
        _rms_router_kernel, grid=(t // tm,),
        in_specs=[pl.BlockSpec((tm, d), lambda i: (i, 0)),
                  pl.BlockSpec((1, d), lambda i: (0, 0)),
                  pl.BlockSpec((d, LANES), lambda i: (0, 0)),
                  pl.BlockSpec((d, LANES), lambda i: (0, 0))],
        out_specs=[pl.BlockSpec((tm, d), lambda i: (i, 0)),
                   pl.BlockSpec((tm, LANES), lambda i: (i, 0)),
                   pl.BlockSpec((8, LANES), lambda i: (0, 0))],
        out_shape=[jax.ShapeDtypeStruct((t, d), BF16),
                   jax.ShapeDtypeStruct((t, LANES), F32),
                   jax.ShapeDtypeStruct((8, LANES), F32)],
        scratch_shapes=[pltpu.VMEM((8, LANES), F32)],
        compiler_params=_cparams(("arbitrary",)),
    )(h, gain.reshape(1, d), w_hi, w_lo)


def _gather_kernel(src_ref, nu_ref, h_hbm, gain_ref, x_ref, hbuf, hsem, *, tm):
    i = pl.program_id(0)
    nu = nu_ref[0]

    def issue(tile, slot):
        group = 8

        def body(b, carry):
            for j in range(group):
                r = b * group + j
                tok = src_ref[tile * tm + r]
                pltpu.make_async_copy(h_hbm.at[pl.ds(tok, 1), :], hbuf.at[slot, pl.ds(r, 1), :],
                                      hsem.at[slot]).start(priority=j % 2)
            return carry
        lax.fori_loop(0, tm // group, body, 0)

    @pl.when(i == 0)
    def _():
        issue(0, 0)

    @pl.when(i + 1 < nu)
    def _():
        issue(i + 1, (i + 1) % 2)

    @pl.when(i < nu)
    def _():
        slot = i % 2
        pltpu.make_async_copy(h_hbm.at[pl.ds(0, tm), :], hbuf.at[slot], hsem.at[slot]).wait()
        x_ref[...] = _rms_rows(hbuf[slot], gain_ref[...]).astype(x_ref.dtype)

    @pl.when(i >= nu)
    def _():
        x_ref[...] = jnp.zeros_like(x_ref)


def _gather_sorted(h, gain, src, n_used, nt):
    t, d = h.shape
    tm = MOE_TILE
    return pl.pallas_call(
        functools.partial(_gather_kernel, tm=tm),
        grid_spec=pltpu.PrefetchScalarGridSpec(
            num_scalar_prefetch=2, grid=(nt,),
            in_specs=[pl.BlockSpec(memory_space=pl.ANY),
                      pl.BlockSpec((1, d), lambda i, src, nu: (0, 0))],
            out_specs=pl.BlockSpec((tm, d), lambda i, src, nu: (i, 0)),
            scratch_shapes=[pltpu.VMEM((2, tm, d), F32), pltpu.SemaphoreType.DMA((2,))],
        ),
        out_shape=jax.ShapeDtypeStruct((nt * tm, d), BF16),
        compiler_params=_cparams(("arbitrary",)),
    )(src, n_used, h, gain.reshape(1, d))


def _moe_up_kernel(ea_ref, eb_ref, nu_ref, x_ref, comb_ref, wg_hbm, wu_hbm, o_ref,
                   wf32_ref, wbf_ref, wsem, run_ref):
    s = pl.program_id(0)
    i = pl.program_id(1)
    nu = nu_ref[0]

    def expert(slot, tile):
        return jnp.where(slot == 0, ea_ref[tile], eb_ref[tile])

    def weight_copies(e, buf):
        return (pltpu.make_async_copy(wg_hbm.at[e], wf32_ref.at[buf, 0], wsem.at[buf]),
                pltpu.make_async_copy(wu_hbm.at[e], wf32_ref.at[buf, 1], wsem.at[buf]))

    @pl.when(i < nu)
    def _():
        cur = expert(s, i)
        first = (s == 0) & (i == 0)
        prev = jnp.where(i > 0, expert(s, jnp.maximum(i - 1, 0)), expert(0, nu - 1))

        @pl.when(first)
        def _():
            run_ref[0] = 0
            for cp in weight_copies(cur, 0):
                cp.start()

        @pl.when(first | (cur != prev))
        def _():
            buf = run_ref[0] % 2
            for cp in weight_copies(cur, buf):
                cp.wait()
            wbf_ref[0] = wf32_ref[buf, 0].astype(BF16)
            wbf_ref[1] = wf32_ref[buf, 1].astype(BF16)

            def expert_at(k):
                slot = (k >= nu).astype(jnp.int32)
                return expert(slot, k - slot * nu)

            end = MOE_TOP_K * nu
            nxt = lax.while_loop(lambda k: (k < end) & (expert_at(jnp.minimum(k, end - 1)) == cur),
                                 lambda k: k + 1, s * nu + i + 1)

            @pl.when(nxt < end)
            def _():
                for cp in weight_copies(expert_at(nxt), 1 - buf):
                    cp.start()

            run_ref[0] = run_ref[0] + 1

        x = x_ref[...]
        g = jnp.dot(x, wbf_ref[0], preferred_element_type=F32)
        u = jnp.dot(x, wbf_ref[1], preferred_element_type=F32)
        comb = comb_ref[...]
        lane = lax.broadcasted_iota(jnp.int32, comb.shape, 1)
        sc = jnp.sum(jnp.where(lane == s, comb, 0.0), axis=1, keepdims=True)
        o_ref[...] = (_silu(g) * u * sc).astype(o_ref.dtype)

    @pl.when(i >= nu_ref[0])
    def _():
        o_ref[...] = jnp.zeros_like(o_ref)


def _moe_down_kernel(ea_ref, eb_ref, nu_ref, dst_ref, hid_ref, wda_ref, wdb_ref, y_hbm, obuf, wbf_ref, sem,
                     *, tm, nt, t, ff):
    i = pl.program_id(0)
    nu = nu_ref[0]
    n_slots = obuf.shape[0]

    def wait_slot(slot):
        pltpu.make_async_copy(obuf.at[slot], y_hbm.at[pl.ds(0, tm), :], sem.at[slot]).wait()

    def compute(slot):
        prev = jnp.maximum(i - 1, 0)

        @pl.when((i == 0) | (ea_ref[i] != ea_ref[prev]))
        def _():
            wbf_ref[0] = wda_ref[...].astype(BF16)

        @pl.when((i == 0) | (eb_ref[i] != eb_ref[prev]))
        def _():
            wbf_ref[1] = wdb_ref[...].astype(BF16)

        obuf[slot] = (jnp.dot(hid_ref[:, :ff], wbf_ref[0], preferred_element_type=F32)
                      + jnp.dot(hid_ref[:, ff:], wbf_ref[1], preferred_element_type=F32))

    def scatter(tile, slot):
        for r in range(tm):
            pltpu.make_async_copy(obuf.at[slot, pl.ds(r, 1), :],
                                  y_hbm.at[pl.ds(dst_ref[tile * tm + r], 1), :], sem.at[slot]).start(priority=r % 2)

    @pl.when(i == 0)
    def _():
        obuf[0] = jnp.zeros(obuf.shape[1:], F32)
        for part in range(n_slots):
            zero_copy = pltpu.make_async_copy(obuf.at[0], y_hbm.at[pl.ds(t + part * tm, tm), :], sem.at[0])
            zero_copy.start()
            zero_copy.wait()
        compute(0)

    @pl.when((i >= 1) & (i < nu))
    def _():
        slot = i % n_slots

        @pl.when(i >= n_slots)
        def _():
            wait_slot(slot)

        compute(slot)
        scatter(i - 1, (i - 1) % n_slots)

    @pl.when(i == nu)
    def _():
        scatter(nu - 1, (nu - 1) % n_slots)

    @pl.when(i == nt - 1)
    def _():
        for back in range(1, n_slots + 1):
            wait_slot((nu - back) % n_slots)


def _moe_sorted(x_sorted, comb_sorted, exp_a, exp_b, n_used, dst, t, w_gate, w_up, w_down):
    tp, d = x_sorted.shape
    ff = w_gate.shape[-1]
    tm = MOE_TILE
    nt = tp // tm
    row = lambda i, nu: jnp.minimum(i, nu[0] - 1)
    hidden = pl.pallas_call(
        _moe_up_kernel,
        grid_spec=pltpu.PrefetchScalarGridSpec(
            num_scalar_prefetch=3, grid=(MOE_TOP_K, nt),
            in_specs=[
                pl.BlockSpec((tm, d), lambda s, i, ea, eb, nu: (row(i, nu), 0)),
                pl.BlockSpec((tm, comb_sorted.shape[1]), lambda s, i, ea, eb, nu: (row(i, nu), 0)),
                pl.BlockSpec(memory_space=pl.ANY),
                pl.BlockSpec(memory_space=pl.ANY),
            ],
            out_specs=pl.BlockSpec((tm, ff), lambda s, i, ea, eb, nu: (i, s)),
            scratch_shapes=[pltpu.VMEM((2, 2, d, ff), F32), pltpu.VMEM((2, d, ff), BF16),
                            pltpu.SemaphoreType.DMA((2,)), pltpu.SMEM((1,), jnp.int32)],
        ),
        out_shape=jax.ShapeDtypeStruct((tp, MOE_TOP_K * ff), BF16),
        compiler_params=_cparams(("arbitrary", "arbitrary")),
    )(exp_a, exp_b, n_used, x_sorted, comb_sorted, w_gate, w_up)
    y = pl.pallas_call(
        functools.partial(_moe_down_kernel, tm=tm, nt=nt, t=t, ff=ff),
        grid_spec=pltpu.PrefetchScalarGridSpec(
            num_scalar_prefetch=4, grid=(nt,),
            in_specs=[
                pl.BlockSpec((tm, MOE_TOP_K * ff), lambda i, ea, eb, nu, dst: (row(i, nu), 0)),
                pl.BlockSpec((None, ff, d), lambda i, ea, eb, nu, dst: (ea[i], 0, 0)),
                pl.BlockSpec((None, ff, d), lambda i, ea, eb, nu, dst: (eb[i], 0, 0)),
            ],
            out_specs=pl.BlockSpec(memory_space=pl.ANY),
            scratch_shapes=[pltpu.VMEM((MOE_SCATTER_SLOTS, tm, d), F32), pltpu.VMEM((MOE_TOP_K, ff, d), BF16),
                            pltpu.SemaphoreType.DMA((MOE_SCATTER_SLOTS,))],
        ),
        out_shape=jax.ShapeDtypeStruct((t + MOE_SCATTER_SLOTS * tm, d), F32),
        compiler_params=_cparams(("arbitrary",)),
    )(exp_a, exp_b, n_used, dst, hidden, w_down, w_down)
    return y


def _moe_layer(h, route, seg_counts, gain, w_gate, w_up, w_down, base):
    t, d = h.shape
    tm = MOE_TILE
    n_seg = MOE_GROUPS * MOE_PAIRS
    nt = t // tm + n_seg
    seg = route[:, 2].astype(jnp.int32)
    rank = route[:, 3].astype(jnp.int32)
    counts = seg_counts[0, :n_seg].astype(jnp.int32)
    ntiles = (counts + tm - 1) // tm
    tile_end = jnp.cumsum(ntiles)
    tile_start = tile_end - ntiles
    onehot = (seg[:, None] == jnp.arange(n_seg, dtype=jnp.int32)[None, :]).astype(F32)
    seg_start = jnp.dot(onehot, tile_start.astype(F32), precision=lax.Precision.HIGHEST)
    pos = jnp.round(seg_start).astype(jnp.int32) * tm + rank
    n_used = tile_end[-1:].astype(jnp.int32)
    tile_seg = jnp.minimum(
        jnp.sum((jnp.arange(nt, dtype=jnp.int32)[:, None] >= tile_end[None, :]).astype(jnp.int32), axis=1),
        n_seg - 1).astype(jnp.int32)
    tile_first = base + (tile_seg // MOE_PAIRS) * MOE_EPG
    exp_a = tile_first + jnp.asarray(MOE_PAIR_A, jnp.int32)[tile_seg % MOE_PAIRS]
    exp_b = tile_first + jnp.asarray(MOE_PAIR_B, jnp.int32)[tile_seg % MOE_PAIRS]
    token = jnp.full((nt * tm,), -1, jnp.int32).at[pos].set(jnp.arange(t, dtype=jnp.int32))
    r = jnp.arange(nt * tm, dtype=jnp.int32)
    src = jnp.where(token >= 0, token, r % t)
    dst = jnp.where(token >= 0, token, t + (r // tm % MOE_SCATTER_SLOTS) * tm + r % tm)
    x_sorted = _gather_sorted(h, gain, src, n_used, nt)
    comb_sorted = jnp.take(route[:, :8], src, axis=0, mode='clip')
    return _moe_sorted(x_sorted, comb_sorted, exp_a, exp_b, n_used, dst, t, w_gate, w_up, w_down)


def _conv_in_kernel(a_ref, wb_ref, wc_ref, wu_ref, cw_ref, o_ref, carry_ref, wbf_ref, *, tm, seq, rb):
    i = pl.program_id(1)

    @pl.when(i == 0)
    def _():
        wbf_ref[0] = wb_ref[...].astype(BF16)
        wbf_ref[1] = wc_ref[...].astype(BF16)
        wbf_ref[2] = wu_ref[...].astype(BF16)

    @pl.when((i * tm) % seq == 0)
    def _():
        carry_ref[...] = jnp.zeros(carry_ref.shape, F32)

    prev = carry_ref[...]
    cw = cw_ref[...]
    for r in range(0, tm, rb):
        rows = slice(r, r + rb)
        a = a_ref[rows, :]
        gate_b = jnp.dot(a, wbf_ref[0], preferred_element_type=F32)
        gate_c = jnp.dot(a, wbf_ref[1], preferred_element_type=F32)
        u = jnp.dot(a, wbf_ref[2], preferred_element_type=F32)
        z = gate_c * u
        rid = lax.broadcasted_iota(jnp.int32, z.shape, 0)
        z1 = jnp.where(rid == 0, prev[7:8], pltpu.roll(z, 1, 0))
        z2 = jnp.where(rid == 0, prev[6:7], jnp.where(rid == 1, prev[7:8], pltpu.roll(z, 2, 0)))
        conv = cw[0:1] * z2 + cw[1:2] * z1 + cw[2:3] * z
        o_ref[rows, :] = (gate_b * conv).astype(o_ref.dtype)
        prev = z[rb - 8:rb]
    carry_ref[...] = prev


def _conv_in(a, w_in, j, conv_w, seq, tm=1024, tn=256, rb=256):
    t, d = a.shape
    nb = d // tn
    return pl.pallas_call(
        functools.partial(_conv_in_kernel, tm=tm, seq=seq, rb=min(rb, tm)),
        grid=(nb, t // tm),
        in_specs=[pl.BlockSpec((tm, d), lambda n, i: (i, 0)),
                  pl.BlockSpec((None, d, tn), lambda n, i: (j, 0, n)),
                  pl.BlockSpec((None, d, tn), lambda n, i: (j, 0, n + nb)),
                  pl.BlockSpec((None, d, tn), lambda n, i: (j, 0, n + 2 * nb)),
                  pl.BlockSpec((CONV_WIDTH, tn), lambda n, i: (0, n))],
        out_specs=pl.BlockSpec((tm, tn), lambda n, i: (i, n)),
        out_shape=jax.ShapeDtypeStruct((t, d), BF16),
        scratch_shapes=[pltpu.VMEM((8, tn), F32), pltpu.VMEM((3, d, tn), BF16)],
        compiler_params=_cparams(("arbitrary", "arbitrary")),
    )(a, w_in, w_in, w_in, conv_w)


def _build_decay_mats():
    t = np.arange(CHUNK)
    i = t[:, None]
    tt = t[None, :]
    mats = [tt <= i, tt > i]
    for l in range(N_LEVELS):
        s = 1 << l
        ref = ((t // (2 * s)) * (2 * s) + s - 1)[:, None]
        upper = ((t % (2 * s)) >= s)[:, None]
        mats.append(np.where(upper, (tt > ref) & (tt <= i), (tt > i) & (tt <= ref)))
    return np.concatenate(mats, axis=0).astype(np.float32)


def _build_level_matrix():
    lv = np.full((CHUNK, CHUNK), -1, np.int32)
    for i in range(CHUNK):
        lv[i, i] = N_LEVELS
        for j in range(i):
            lv[i, j] = int(np.floor(np.log2(i ^ j)))
    return lv


_DECAY_MATS = _build_decay_mats()
_LEVEL_MAT = _build_level_matrix()

FAST_BLOCK = CHUNK // 2
FAST_LIMIT = 96.0


def _build_fast_mats():
    t = np.arange(CHUNK)
    i = t[:, None]
    tt = t[None, :]
    ref = ((t // FAST_BLOCK) * FAST_BLOCK + FAST_BLOCK // 2 - 1)[:, None]
    mid = ((tt > ref) & (tt <= i)).astype(np.float32) - ((tt > i) & (tt <= ref)).astype(np.float32)
    top = _DECAY_MATS[(1 + N_LEVELS) * CHUNK:(2 + N_LEVELS) * CHUNK]
    return np.concatenate([_DECAY_MATS[:2 * CHUNK], top, mid], axis=0).astype(np.float32)


def _build_fast_level_matrix():
    i = np.arange(CHUNK)[:, None]
    j = np.arange(CHUNK)[None, :]
    same = (i // FAST_BLOCK) == (j // FAST_BLOCK)
    return np.where(same & (i >= j), 0, np.where((i // FAST_BLOCK) > (j // FAST_BLOCK), 1, -1)).astype(np.int32)


_FAST_MATS = _build_fast_mats()
_FAST_LEVEL_MAT = _build_fast_level_matrix()

_NT = (((1,), (1,)), ((), ()))
_TN = (((0,), (0,)), ((), ()))


def _rec_kernel(q_ref, k_ref, la_ref, v_ref, sg_ref, gain_ref, mall_ref, lvl_ref, mfast_ref, lvlf_ref,
                o_ref, st_ref, *, nh, dk, dv, ts, seq, unroll):
    t = pl.program_id(1)

    @pl.when((t * ts) % seq == 0)
    def _():
        st_ref[...] = jnp.zeros_like(st_ref)

    gain = gain_ref[...]

    def decays(mats, rows):
        la = la_ref[rows, :] * LOG2_E
        la_hi = la.astype(BF16)
        la_lo = (la - la_hi.astype(F32)).astype(BF16)
        return jnp.dot(mats, jnp.concatenate([la_hi, la_lo], axis=0), preferred_element_type=F32)

    def finish(h, rows, q, k, g, suffix, scores):
        vc = slice(h * dv, (h + 1) * dv)
        v = v_ref[rows, vc]
        state_t = st_ref[h]
        o = (jnp.dot(scores.astype(BF16), v, preferred_element_type=F32)
             + lax.dot_general((q * jnp.exp2(g)).astype(BF16), state_t.astype(BF16), _NT,
                               preferred_element_type=F32))
        k_end = (k * jnp.exp2(suffix)).astype(BF16)
        st_ref[h] = (state_t * jnp.exp2(g[CHUNK - 1:CHUNK])
                     + lax.dot_general(v, k_end, _TN, preferred_element_type=F32))
        on = _rms_rows(o, gain[:, vc]) * sg_ref[rows, vc].astype(F32)
        o_ref[rows, vc] = on.astype(o_ref.dtype)

    def fast_chunk(c, carry):
        rows = pl.ds(pl.multiple_of(c * CHUNK, CHUNK), CHUNK)
        lvl = lvlf_ref[...]
        dall = decays(mfast_ref[...], rows)
        for h in range(nh):
            kc = slice(h * dk, (h + 1) * dk)
            q = q_ref[rows, kc].astype(F32)
            k = k_ref[rows, kc].astype(F32)
            e_top = jnp.exp2(dall[2 * CHUNK:3 * CHUNK, kc])
            d_mid = dall[3 * CHUNK:4 * CHUNK, kc]
            r_top = lax.dot_general((q * e_top).astype(BF16), (k * e_top).astype(BF16), _NT,
                                    preferred_element_type=F32)
            r_mid = lax.dot_general((q * jnp.exp2(d_mid)).astype(BF16), (k * jnp.exp2(-d_mid)).astype(BF16),
                                    _NT, preferred_element_type=F32)
            scores = jnp.where(lvl == 1, r_top, jnp.where(lvl == 0, r_mid, 0.0))
            finish(h, rows, q, k, dall[0:CHUNK, kc], dall[CHUNK:2 * CHUNK, kc], scores)
        return carry

    def exact_chunk(c, carry):
        rows = pl.ds(pl.multiple_of(c * CHUNK, CHUNK), CHUNK)
        lvl = lvl_ref[...]
        dall = decays(mall_ref[...], rows)
        for h in range(nh):
            kc = slice(h * dk, (h + 1) * dk)
            q = q_ref[rows, kc].astype(F32)
            k = k_ref[rows, kc].astype(F32)
            scores = jnp.where(
                lvl == N_LEVELS,
                lax.dot_general(q.astype(BF16), k.astype(BF16), _NT, preferred_element_type=F32), 0.0)
            for l in range(N_LEVELS):
                e = jnp.exp2(dall[(2 + l) * CHUNK:(3 + l) * CHUNK, kc])
                r = lax.dot_general((q * e).astype(BF16), (k * e).astype(BF16), _NT,
                                    preferred_element_type=F32)
                scores = jnp.where(lvl == l, r, scores)
            finish(h, rows, q, k, dall[0:CHUNK, kc], dall[CHUNK:2 * CHUNK, kc], scores)
        return carry

    half = FAST_BLOCK // 2
    mass = jnp.sum(jnp.abs(la_ref[...]).reshape(ts // half, half, nh * dk), axis=1)
    safe = jnp.max(mass) * LOG2_E < FAST_LIMIT

    @pl.when(safe)
    def _():
        lax.fori_loop(0, ts // CHUNK, fast_chunk, 0, unroll=2 * unroll)

    @pl.when(jnp.logical_not(safe))
    def _():
        lax.fori_loop(0, ts // CHUNK, exact_chunk, 0, unroll=unroll)


def _recurrence(q, k, la, v, sg, gain, *, dk, dv, nh, seq, ts=1024, unroll=2):
    t = q.shape[0]
    heads = q.shape[1] // dk
    ts = min(ts, seq)
    dup = lambda m: jnp.asarray(np.concatenate([m, m], axis=1), BF16)
    mall, mfast = dup(_DECAY_MATS), dup(_FAST_MATS)
    lvl, lvlf = jnp.asarray(_LEVEL_MAT), jnp.asarray(_FAST_LEVEL_MAT)
    kspec = pl.BlockSpec((ts, nh * dk), lambda hg, s: (s, hg))
    vspec = pl.BlockSpec((ts, nh * dv), lambda hg, s: (s, hg))
    const = lambda arr: pl.BlockSpec(arr.shape, lambda hg, s: (0, 0))
    return pl.pallas_call(
        functools.partial(_rec_kernel, nh=nh, dk=dk, dv=dv, ts=ts, seq=seq, unroll=unroll),
        grid=(heads // nh, t // ts),
        in_specs=[kspec, kspec, kspec, vspec, vspec,
                  pl.BlockSpec((1, nh * dv), lambda hg, s: (0, hg)),
                  const(mall), const(lvl), const(mfast), const(lvlf)],
        out_specs=vspec,
        out_shape=jax.ShapeDtypeStruct((t, heads * dv), BF16),
        scratch_shapes=[pltpu.VMEM((nh, dv, dk), F32)],
        compiler_params=_cparams(("parallel", "arbitrary")),
    )(q, k, la, v, sg, gain.reshape(1, -1), mall, lvl, mfast, lvlf)


def _gelu_tanh(x):
    return jax.nn.gelu(x, approximate=True)


def _gmlp_spatial_kernel(u_ref, v_ref, lng_ref, lnb_ref, ws_ref, bs_ref, o_ref, *, groups):
    v = v_ref[...].astype(F32)
    mu = jnp.mean(v, axis=-1, keepdims=True)
    vc = v - mu
    var = jnp.mean(vc * vc, axis=-1, keepdims=True)
    vn = (vc * lax.rsqrt(var + EPS) * lng_ref[...] + lnb_ref[...]).astype(BF16)
    blk = v.shape[0]
    gw = v.shape[1] // groups
    ri = lax.broadcasted_iota(jnp.int32, (blk, blk), 0)
    ci = lax.broadcasted_iota(jnp.int32, (blk, blk), 1)
    for g in range(groups):
        cols = slice(g * gw, (g + 1) * gw)
        wc = jnp.where(ci <= ri, ws_ref[g], 0.0).astype(BF16)
        mixed = jnp.dot(wc, vn[:, cols], preferred_element_type=F32) + bs_ref[g]
        o_ref[:, cols] = (u_ref[:, cols].astype(F32) * mixed).astype(o_ref.dtype)


def _gmlp_spatial(u, v, ln_g, ln_b, w_s, b_s):
    t, w = u.shape
    groups, blk = w_s.shape[0], w_s.shape[1]
    rows = pl.BlockSpec((blk, w), lambda i: (i, 0))
    return pl.pallas_call(
        functools.partial(_gmlp_spatial_kernel, groups=groups),
        grid=(t // blk,),
        in_specs=[rows, rows,
                  pl.BlockSpec((1, w), lambda i: (0, 0)),
                  pl.BlockSpec((1, w), lambda i: (0, 0)),
                  pl.BlockSpec((groups, blk, blk), lambda i: (0, 0, 0)),
                  pl.BlockSpec((groups, blk, 1), lambda i: (0, 0, 0))],
        out_specs=rows,
        out_shape=jax.ShapeDtypeStruct((t, w), BF16),
        compiler_params=_cparams(("parallel",)),
    )(u, v, ln_g.reshape(1, w), ln_b.reshape(1, w), w_s, b_s.reshape(groups, blk, 1))


def _epi_residual(accs, extras):
    return [extras[0] + accs[0]]


def _epi_hgrn(accs, extras):
    q, f, i, g = accs
    lb = extras[0]
    e = jnp.exp(-jnp.abs(f))
    log_sig = jnp.minimum(f, 0.0) - jnp.log1p(e)
    a = jnp.log(lb)
    b = jnp.log1p(-lb) + log_sig
    log_forget = jnp.maximum(a, b) + jnp.log1p(jnp.exp(-jnp.abs(a - b)))
    key = (1.0 - lb) * (jnp.where(f >= 0.0, e, 1.0) / (1.0 + e))
    return [_silu(q), key, log_forget, i, _silu(g)]


def _epi_gelu2(accs, extras):
    return [_gelu_tanh(accs[0]), _gelu_tanh(accs[1])]


def _epi_scale(scale):
    return lambda accs, extras: [accs[0] * scale]


def _epi_identity(accs, extras):
    return [accs[0]]


def _epi_silu(accs, extras):
    return [_silu(accs[0])]


def _epi_gla_gate(accs, extras):
    z = accs[0] + extras[0]
    log_sig = jnp.minimum(z, 0.0) - jnp.log1p(jnp.exp(-jnp.abs(z)))
    return [log_sig / GLA_GATE_TAU]


def _epi_first_lanes(width):
    def epi(accs, extras):
        lane = lax.broadcasted_iota(jnp.int32, accs[0].shape, 1)
        return [jnp.where(lane < width, accs[0], 0.0)]
    return epi


def _epi_ple(accs, extras):
    h, y, p, proj = extras
    emb = jnp.dot(p.astype(BF16), proj.astype(BF16), preferred_element_type=F32)
    return [(h + y) + emb * _sigmoid(accs[0])]


def kernel(x, p, norm_mix, norm_ffn, norm_final, conv_w_in, conv_w, conv_w_out, hgrn_w_in, hgrn_lb, hgrn_norm, hgrn_w_out, gla_w_in, gla_w_a2, gla_b_a, gla_norm, gla_w_out, gmlp_w_in, gmlp_ln_g, gmlp_ln_b, gmlp_w_s, gmlp_b_s, gmlp_w_out, moe_w_group, moe_w_expert, moe_w_gate, moe_w_up, moe_w_down, ple_w_proj, ple_w_gate):
    batch, seq, d = x.shape
    depth = p.shape[0]
    t = batch * seq
    ff = moe_w_gate.shape[-1]
    n_exp = moe_w_gate.shape[1]

    lb_w = jax.nn.softmax(hgrn_lb.astype(F32), axis=0)
    lower_bounds = jnp.cumsum(lb_w, axis=0) - lb_w[0]
    w_gate_all = moe_w_gate.reshape(depth * n_exp, d, ff)
    w_up_all = moe_w_up.reshape(depth * n_exp, d, ff)
    w_down_all = moe_w_down.reshape(depth * n_exp, ff, d)
    p_all = p.reshape(depth, t, p.shape[-1])

    h = x.reshape(t, d)
    for layer in range(depth):
        kind, j = layer % 4, layer // 4
        a = _rmsnorm(h, norm_mix[layer], BF16)
        if kind == 0:
            gated = _conv_in(a, conv_w_in, j, conv_w[j], seq)
            w_out = conv_w_out
        elif kind == 1:
            nb = d // 256
            qt, key, log_f, val, sg = _mm(
                a, [(hgrn_w_in, j, o * nb) for o in range(4)],
                [(lower_bounds.reshape(depth, 1, d), 'row', layer)],
                [BF16, BF16, F32, BF16, BF16], _epi_hgrn, n_cols=d, tm=1024, tn=256, rb=128)
            gated = _recurrence(qt, key, log_f, val, sg, hgrn_norm[j],
                                dk=HGRN_DK, dv=HGRN_DK, nh=4, seq=seq)
            w_out = hgrn_w_out
        elif kind == 2:
            kd = d // 2
            dk = kd // GLA_HEADS
            dv = d // GLA_HEADS
            tn = 1024
            (q,) = _mm(a, [(gla_w_in, j, 0)], [], [BF16], _epi_scale(dk ** -0.5),
                       n_cols=kd, tm=512, tn=tn)
            (k,) = _mm(a, [(gla_w_in, j, kd // tn)], [], [BF16], _epi_identity,
                       n_cols=kd, tm=512, tn=tn)
            (v,) = _mm(a, [(gla_w_in, j, 2 * kd // tn)], [], [BF16], _epi_identity,
                       n_cols=d, tm=512, tn=tn)
            (sg,) = _mm(a, [(gla_w_in, j, (2 * kd + d) // tn)], [], [BF16], _epi_silu,
                        n_cols=d, tm=512, tn=tn)
            (a_low,) = _mm(a, [(gla_w_in, j, (2 * kd + 2 * d) // LANES)], [], [BF16],
                           _epi_first_lanes(GLA_GATE_RANK), n_cols=LANES, tm=512, tn=LANES)
            w_a2 = jnp.pad(gla_w_a2, ((0, 0), (0, LANES - GLA_GATE_RANK), (0, 0)))
            (log_a,) = _mm(a_low, [(w_a2, j, 0)], [(gla_b_a.reshape(-1, 1, kd), 'row', j)], [F32],
                           _epi_gla_gate, n_cols=kd, tm=1024, tn=tn)
            gated = _recurrence(q, k, log_a, v, sg, gla_norm[j], dk=dk, dv=dv, nh=1, seq=seq)
            w_out = gla_w_out
        else:
            wd = gmlp_w_in.shape[-1] // 2
            u, v = _mm(a, [(gmlp_w_in, j, 0), (gmlp_w_in, j, wd // 512)], [], [BF16, BF16], _epi_gelu2,
                       n_cols=wd, tm=1024, tn=512)
            gated = _gmlp_spatial(u, v, gmlp_ln_g[j], gmlp_ln_b[j], gmlp_w_s[j], gmlp_b_s[j])
            w_out = gmlp_w_out
        (h,) = _mm(gated, [(w_out, j, 0)], [(h, 'tile', None)], [F32], _epi_residual,
                   n_cols=d, tm=512, tn=512 if gated.shape[1] > d else 1024)

        w_router = jnp.pad(jnp.concatenate([moe_w_group[layer], moe_w_expert[layer]], axis=1),
                           ((0, 0), (0, LANES - MOE_GROUPS - MOE_GROUPS * MOE_EPG)))
        c, route, seg_counts = _rms_router(h, norm_ffn[layer], w_router)
        y = _moe_layer(h, route, seg_counts, norm_ffn[layer], w_gate_all, w_up_all, w_down_all, layer * n_exp)
        (h,) = _mm(c, [(ple_w_gate, layer, 0)],
                   [(h, 'tile', None), (y, 'tile', None), (p_all, 'rowtile', layer),
                    (ple_w_proj, 'kcol', layer)],
                   [F32], _epi_ple, n_cols=d, tm=512, tn=1024)
    out = _rmsnorm(h, norm_final, F32)
    return out.reshape(batch, seq, d)
```

```python
import functools

import numpy as np
import jax
import jax.numpy as jnp
from jax import lax
from jax.experimental import pallas as pl
from jax.experimental.pallas import tpu as pltpu

F32 = jnp.float32
BF16 = jnp.bfloat16

EPS = 1e-6
CHUNK = 128
N_LEVELS = 7
LOG2_E = 1.4426950408889634
LANES = 128
VMEM_LIMIT = 56 * 1024 * 1024

CONV_WIDTH = 3
HGRN_DK = 128
GLA_HEADS = 8
GLA_GATE_RANK = 16
GLA_GATE_TAU = 16.0
GMLP_BLOCK = 128
GMLP_GROUPS = 8
MOE_GROUPS = 4
MOE_EPG = 4
MOE_TOP_K = 2
MOE_PAIRS = 6
MOE_PAIR_A = (0, 0, 0, 1, 1, 2)
MOE_PAIR_B = (1, 2, 3, 2, 3, 3)
MOE_TILE = 256
MOE_SCATTER_SLOTS = 3


def _cparams(sem):
    return pltpu.CompilerParams(dimension_semantics=sem, vmem_limit_bytes=VMEM_LIMIT)


def _sigmoid(x):
    return jax.nn.sigmoid(x)


def _silu(x):
    return x * jax.nn.sigmoid(x)


def _mm_kernel(*refs, n_w, n_e, epilogue, kinds, rb):
    a_ref = refs[0]
    w_refs = refs[1:1 + n_w]
    e_refs = refs[1 + n_w:1 + n_w + n_e]
    o_refs = refs[1 + n_w + n_e:-1]
    wbf_ref = refs[-1]

    @pl.when(pl.program_id(1) == 0)
    def _():
        for j, w in enumerate(w_refs):
            wbf_ref[j] = w[...].astype(BF16)

    tm = a_ref.shape[0]
    for r in range(0, tm, rb):
        rows = slice(r, r + rb)
        a = a_ref[rows, :]
        accs = [jnp.dot(a, wbf_ref[j], preferred_element_type=F32) for j in range(n_w)]
        ex = [e[rows, :] if kind in ('tile', 'rowtile') else e[...] for e, kind in zip(e_refs, kinds)]
        res = epilogue(accs, ex)
        for o, val in zip(o_refs, res):
            o[rows, :] = val.astype(o.dtype)


def _mm(a, ws, extras, out_dtypes, epilogue, *, n_cols, tm, tn, rb=256):
    m, k = a.shape
    grid = (n_cols // tn, m // tm)
    in_specs = [pl.BlockSpec((tm, k), lambda n, i: (i, 0))]
    args = [a]
    for w, lead, off in ws:
        in_specs.append(pl.BlockSpec((None, k, tn), lambda n, i, lead=lead, off=off: (lead, 0, n + off),
                                     pipeline_mode=pl.Buffered(1)))
        args.append(w)
    for arr, kind, lead in extras:
        if kind == 'row':
            in_specs.append(pl.BlockSpec((None, 1, tn), lambda n, i, lead=lead: (lead, 0, n)))
        elif kind == 'tile':
            assert lead is None
            in_specs.append(pl.BlockSpec((tm, tn), lambda n, i: (i, n)))
        elif kind == 'rowtile':
            in_specs.append(pl.BlockSpec((None, tm, arr.shape[2]), lambda n, i, lead=lead: (lead, i, 0)))
        elif kind == 'kcol':
            in_specs.append(pl.BlockSpec((None, arr.shape[1], tn), lambda n, i, lead=lead: (lead, 0, n)))
        else:
            raise ValueError(kind)
        args.append(arr)
    out_shape = [jax.ShapeDtypeStruct((m, n_cols), dt) for dt in out_dtypes]
    out_specs = [pl.BlockSpec((tm, tn), lambda n, i: (i, n)) for _ in out_dtypes]
    res = pl.pallas_call(
        functools.partial(_mm_kernel, n_w=len(ws), n_e=len(extras), epilogue=epilogue,
                          kinds=tuple(kind for _, kind, _ in extras), rb=min(rb, tm)),
        grid=grid, in_specs=in_specs, out_specs=out_specs, out_shape=out_shape,
        scratch_shapes=[pltpu.VMEM((len(ws), k, tn), BF16)],
        compiler_params=_cparams(("arbitrary", "arbitrary")),
    )(*args)
    return res


def _rms_rows(x, gain):
    ms = jnp.mean(x * x, axis=-1, keepdims=True)
    return x * lax.rsqrt(ms + EPS) * gain


def _rms_kernel(h_ref, g_ref, o_ref):
    o_ref[...] = _rms_rows(h_ref[...], g_ref[...]).astype(o_ref.dtype)


def _rmsnorm(h, gain, out_dtype, tm=256):
    t, d = h.shape
    return pl.pallas_call(
        _rms_kernel, grid=(t // tm,),
        in_specs=[pl.BlockSpec((tm, d), lambda i: (i, 0)), pl.BlockSpec((1, d), lambda i: (0, 0))],
        out_specs=pl.BlockSpec((tm, d), lambda i: (i, 0)),
        out_shape=jax.ShapeDtypeStruct((t, d), out_dtype),
        compiler_params=_cparams(("parallel",)),
    )(h, gain.reshape(1, d))


def _first_max_index(vals, m):
    idx = jnp.full(m.shape, len(vals) - 1, jnp.int32)
    for j in range(len(vals) - 2, -1, -1):
        idx = jnp.where(vals[j] == m, j, idx)
    return idx


def _route_cols(logits):
    col = lambda j: logits[:, j:j + 1]
    gl = [col(j) for j in range(MOE_GROUPS)]
    gm = functools.reduce(jnp.maximum, gl)
    gsum = functools.reduce(lambda a, b: a + b, [jnp.exp(g - gm) for g in gl])
    group_p = 1.0 / gsum
    gidx = _first_max_index(gl, gm)
    ig = []
    for e in range(MOE_EPG):
        v = col(MOE_GROUPS + (MOE_GROUPS - 1) * MOE_EPG + e)
        for g in range(MOE_GROUPS - 2, -1, -1):
            v = jnp.where(gidx == g, col(MOE_GROUPS + g * MOE_EPG + e), v)
        ig.append(v)
    m1 = functools.reduce(jnp.maximum, ig)
    i1 = _first_max_index(ig, m1)
    ig2 = [jnp.where(i1 == e, -jnp.inf, ig[e]) for e in range(MOE_EPG)]
    m2 = functools.reduce(jnp.maximum, ig2)
    i2 = _first_max_index(ig2, m2)
    t = jnp.exp(m2 - m1)
    w1 = 1.0 / (1.0 + t) * group_p
    w2 = t / (1.0 + t) * group_p
    first = i1 < i2
    ea = jnp.where(first, i1, i2)
    eb = jnp.where(first, i2, i1)
    wa = jnp.where(first, w1, w2)
    wb = jnp.where(first, w2, w1)
    pair = jnp.where(ea == 0, 0, jnp.where(ea == 1, 3, 5)) + (eb - ea - 1)
    seg = gidx * MOE_PAIRS + pair
    lane = lax.broadcasted_iota(jnp.int32, logits.shape, 1)
    return jnp.where(lane == 0, wa, jnp.where(lane == 1, wb, jnp.where(lane == 2, seg.astype(F32), 0.0)))


def _rms_router_kernel(h_ref, g_ref, whi_ref, wlo_ref, c_ref, route_ref, counts_ref, carry_ref):
    @pl.when(pl.program_id(0) == 0)
    def _():
        carry_ref[...] = jnp.zeros_like(carry_ref)

    y = _rms_rows(h_ref[...], g_ref[...])
    c = y.astype(BF16)
    c_ref[...] = c
    y_lo = (y - c.astype(F32)).astype(BF16)
    whi = whi_ref[...]
    logits = (jnp.dot(c, whi, preferred_element_type=F32)
              + (jnp.dot(c, wlo_ref[...], preferred_element_type=F32)
                 + jnp.dot(y_lo, whi, preferred_element_type=F32)))
    route = _route_cols(logits)

    tm = route.shape[0]
    lane = lax.broadcasted_iota(jnp.int32, route.shape, 1)
    onehot = (lane == route[:, 2:3].astype(jnp.int32)).astype(F32)
    ri = lax.broadcasted_iota(jnp.int32, (tm, tm), 0)
    ci = lax.broadcasted_iota(jnp.int32, (tm, tm), 1)
    earlier = jnp.where(ci < ri, 1.0, 0.0).astype(BF16)
    before = jnp.dot(earlier, onehot.astype(BF16), preferred_element_type=F32) + carry_ref[0:1]
    rank = jnp.sum(onehot * before, axis=1, keepdims=True)
    route_ref[...] = jnp.where(lane == 3, rank, route)
    carry_ref[...] = carry_ref[...] + jnp.sum(onehot, axis=0, keepdims=True)
    counts_ref[...] = carry_ref[...]


def _rms_router(h, gain, w_router, tm=512):
    t, d = h.shape
    w_hi = w_router.astype(BF16)
    w_lo = (w_router - w_hi.astype(F32)).astype(BF16)
    return pl.pallas_call(
        _rms_router_kernel, grid=(t // tm,),
        in_specs=[pl.BlockSpec((tm, d), lambda i: (i, 0)),
                  pl.BlockSpec((1, d), lambda i: (0, 0)),
                  pl.BlockSpec((d, LANES), lambda i: (0, 0)),
                  pl.BlockSpec((d, LANES), lambda i: (0, 0))],
        out_specs=[pl.BlockSpec((tm, d), lambda i: (i, 0)),
                   pl.BlockSpec((tm, LANES), lambda i: (i, 0)),
                   pl.BlockSpec((8, LANES), lambda i: (0, 0))],
        out_shape=[jax.ShapeDtypeStruct((t, d), BF16),
                   jax.ShapeDtypeStruct((t, LANES), F32),
                   jax.ShapeDtypeStruct((8, LANES), F32)],
        scratch_shapes=[pltpu.VMEM((8, LANES), F32)],
        compiler_params=_cparams(("arbitrary",)),
    )(h, gain.reshape(1, d), w_hi, w_lo)


def _gather_kernel(src_ref, nu_ref, h_hbm, gain_ref, x_ref, hbuf0, hbuf1, hsem, *, tm):
    bufs = (hbuf0, hbuf1)
    i = pl.program_id(0)
    nu = nu_ref[0]

    def issue(tile, slot):
        for r in range(tm):
            tok = src_ref[tile * tm + r]
            pltpu.make_async_copy(h_hbm.at[pl.ds(tok, 1), :], bufs[slot].at[pl.ds(r, 1), :],
                                  hsem.at[slot]).start(priority=r % 2)

    def finish(slot):
        pltpu.make_async_copy(h_hbm.at[pl.ds(0, tm), :], bufs[slot], hsem.at[slot]).wait()
        x_ref[...] = _rms_rows(bufs[slot][...], gain_ref[...]).astype(x_ref.dtype)

    @pl.when(i == 0)
    def _():
        issue(0, 0)

    for slot in range(2):
        @pl.when((i + 1 < nu) & (i % 2 == slot))
        def _(slot=slot):
            finish(slot)
            issue(i + 1, 1 - slot)

        @pl.when((i + 1 == nu) & (i % 2 == slot))
        def _(slot=slot):
            finish(slot)

    @pl.when(i >= nu)
    def _():
        x_ref[...] = jnp.zeros_like(x_ref)


def _gather_sorted(h, gain, src, n_used, nt):
    t, d = h.shape
    tm = MOE_TILE
    return pl.pallas_call(
        functools.partial(_gather_kernel, tm=tm),
        grid_spec=pltpu.PrefetchScalarGridSpec(
            num_scalar_prefetch=2, grid=(nt,),
            in_specs=[pl.BlockSpec(memory_space=pl.ANY),
                      pl.BlockSpec((1, d), lambda i, src, nu: (0, 0))],
            out_specs=pl.BlockSpec((tm, d), lambda i, src, nu: (i, 0)),
            scratch_shapes=[pltpu.VMEM((tm, d), F32), pltpu.VMEM((tm, d), F32),
                            pltpu.SemaphoreType.DMA((2,))],
        ),
        out_shape=jax.ShapeDtypeStruct((nt * tm, d), BF16),
        compiler_params=_cparams(("arbitrary",)),
    )(src, n_used, h, gain.reshape(1, d))


def _moe_up_kernel(ea_ref, eb_ref, nu_ref, x_ref, comb_ref, wg_hbm, wu_hbm, o_ref,
                   wf32_ref, wbf_ref, wsem, run_ref):
    s = pl.program_id(0)
    i = pl.program_id(1)
    nu = nu_ref[0]

    def expert(slot, tile):
        return jnp.where(slot == 0, ea_ref[tile], eb_ref[tile])

    def weight_copies(e, buf):
        return (pltpu.make_async_copy(wg_hbm.at[e], wf32_ref.at[buf, 0], wsem.at[buf]),
                pltpu.make_async_copy(wu_hbm.at[e], wf32_ref.at[buf, 1], wsem.at[buf]))

    @pl.when(i < nu)
    def _():
        cur = expert(s, i)
        first = (s == 0) & (i == 0)
        prev = jnp.where(i > 0, expert(s, jnp.maximum(i - 1, 0)), expert(0, nu - 1))

        @pl.when(first)
        def _():
            run_ref[0] = 0
            for cp in weight_copies(cur, 0):
                cp.start()

        @pl.when(first | (cur != prev))
        def _():
            buf = run_ref[0] % 2
            for cp in weight_copies(cur, buf):
                cp.wait()
            wbf_ref[0] = wf32_ref[buf, 0].astype(BF16)
            wbf_ref[1] = wf32_ref[buf, 1].astype(BF16)

            def expert_at(k):
                slot = (k >= nu).astype(jnp.int32)
                return expert(slot, k - slot * nu)

            end = MOE_TOP_K * nu
            nxt = lax.while_loop(lambda k: (k < end) & (expert_at(jnp.minimum(k, end - 1)) == cur),
                                 lambda k: k + 1, s * nu + i + 1)

            @pl.when(nxt < end)
            def _():
                for cp in weight_copies(expert_at(nxt), 1 - buf):
                    cp.start()

            run_ref[0] = run_ref[0] + 1

        x = x_ref[...]
        g = jnp.dot(x, wbf_ref[0], preferred_element_type=F32)
        u = jnp.dot(x, wbf_ref[1], preferred_element_type=F32)
        comb = comb_ref[...]
        lane = lax.broadcasted_iota(jnp.int32, comb.shape, 1)
        sc = jnp.sum(jnp.where(lane == s, comb, 0.0), axis=1, keepdims=True)
        o_ref[...] = (_silu(g) * u * sc).astype(o_ref.dtype)

    @pl.when(i >= nu_ref[0])
    def _():
        o_ref[...] = jnp.zeros_like(o_ref)


def _moe_down_kernel(ea_ref, eb_ref, nu_ref, dst_ref, hid_ref, wda_ref, wdb_ref, y_hbm, *scratch,
                     tm, nt, t, ff):
    obufs = scratch[:MOE_SCATTER_SLOTS]
    wbf_ref, sem = scratch[MOE_SCATTER_SLOTS:]
    i = pl.program_id(0)
    nu = nu_ref[0]
    n_slots = len(obufs)

    def wait_slot(slot):
        pltpu.make_async_copy(obufs[slot], y_hbm.at[pl.ds(0, tm), :], sem.at[slot]).wait()

    def compute(slot):
        prev = jnp.maximum(i - 1, 0)

        @pl.when((i == 0) | (ea_ref[i] != ea_ref[prev]))
        def _():
            wbf_ref[0] = wda_ref[...].astype(BF16)

        @pl.when((i == 0) | (eb_ref[i] != eb_ref[prev]))
        def _():
            wbf_ref[1] = wdb_ref[...].astype(BF16)

        obufs[slot][...] = (jnp.dot(hid_ref[:, :ff], wbf_ref[0], preferred_element_type=F32)
                            + jnp.dot(hid_ref[:, ff:], wbf_ref[1], preferred_element_type=F32))

    def scatter(tile, slot):
        for r in range(tm):
            pltpu.make_async_copy(obufs[slot].at[pl.ds(r, 1), :],
                                  y_hbm.at[pl.ds(dst_ref[tile * tm + r], 1), :], sem.at[slot]).start(priority=r % 2)

    @pl.when(i == 0)
    def _():
        obufs[0][...] = jnp.zeros(obufs[0].shape, F32)
        for part in range(n_slots):
            zero_copy = pltpu.make_async_copy(obufs[0], y_hbm.at[pl.ds(t + part * tm, tm), :], sem.at[0])
            zero_copy.start()
            zero_copy.wait()
        compute(0)

    for slot in range(n_slots):
        @pl.when((i >= 1) & (i < nu) & (i % n_slots == slot))
        def _(slot=slot):
            @pl.when(i >= n_slots)
            def _():
                wait_slot(slot)

            compute(slot)
            scatter(i - 1, (slot - 1) % n_slots)

        @pl.when((i == nu) & ((nu - 1) % n_slots == slot))
        def _(slot=slot):
            scatter(nu - 1, slot)

    @pl.when(i == nt - 1)
    def _():
        for slot in range(n_slots):
            wait_slot(slot)


def _moe_sorted(x_sorted, comb_sorted, exp_a, exp_b, n_used, dst, t, w_gate, w_up, w_down):
    tp, d = x_sorted.shape
    ff = w_gate.shape[-1]
    tm = MOE_TILE
    nt = tp // tm
    row = lambda i, nu: jnp.minimum(i, nu[0] - 1)
    hidden = pl.pallas_call(
        _moe_up_kernel,
        grid_spec=pltpu.PrefetchScalarGridSpec(
            num_scalar_prefetch=3, grid=(MOE_TOP_K, nt),
            in_specs=[
                pl.BlockSpec((tm, d), lambda s, i, ea, eb, nu: (row(i, nu), 0)),
                pl.BlockSpec((tm, comb_sorted.shape[1]), lambda s, i, ea, eb, nu: (row(i, nu), 0)),
                pl.BlockSpec(memory_space=pl.ANY),
                pl.BlockSpec(memory_space=pl.ANY),
            ],
            out_specs=pl.BlockSpec((tm, ff), lambda s, i, ea, eb, nu: (i, s)),
            scratch_shapes=[pltpu.VMEM((2, 2, d, ff), F32), pltpu.VMEM((2, d, ff), BF16),
                            pltpu.SemaphoreType.DMA((2,)), pltpu.SMEM((1,), jnp.int32)],
        ),
        out_shape=jax.ShapeDtypeStruct((tp, MOE_TOP_K * ff), BF16),
        compiler_params=_cparams(("arbitrary", "arbitrary")),
    )(exp_a, exp_b, n_used, x_sorted, comb_sorted, w_gate, w_up)
    y = pl.pallas_call(
        functools.partial(_moe_down_kernel, tm=tm, nt=nt, t=t, ff=ff),
        grid_spec=pltpu.PrefetchScalarGridSpec(
            num_scalar_prefetch=4, grid=(nt,),
            in_specs=[
                pl.BlockSpec((tm, MOE_TOP_K * ff), lambda i, ea, eb, nu, dst: (row(i, nu), 0)),
                pl.BlockSpec((None, ff, d), lambda i, ea, eb, nu, dst: (ea[i], 0, 0)),
                pl.BlockSpec((None, ff, d), lambda i, ea, eb, nu, dst: (eb[i], 0, 0)),
            ],
            out_specs=pl.BlockSpec(memory_space=pl.ANY),
            scratch_shapes=([pltpu.VMEM((tm, d), F32)] * MOE_SCATTER_SLOTS
                            + [pltpu.VMEM((MOE_TOP_K, ff, d), BF16),
                               pltpu.SemaphoreType.DMA((MOE_SCATTER_SLOTS,))]),
        ),
        out_shape=jax.ShapeDtypeStruct((t + MOE_SCATTER_SLOTS * tm, d), F32),
        compiler_params=_cparams(("arbitrary",)),
    )(exp_a, exp_b, n_used, dst, hidden, w_down, w_down)
    return y


def _moe_layer(h, route, seg_counts, gain, w_gate, w_up, w_down, base):
    t, d = h.shape
    tm = MOE_TILE
    n_seg = MOE_GROUPS * MOE_PAIRS
    assert t % tm == 0 and t // tm >= MOE_SCATTER_SLOTS
    nt = t // tm + n_seg
    seg = route[:, 2].astype(jnp.int32)
    rank = route[:, 3].astype(jnp.int32)
    counts = seg_counts[0, :n_seg].astype(jnp.int32)
    ntiles = (counts + tm - 1) // tm
    tile_end = jnp.cumsum(ntiles)
    tile_start = tile_end - ntiles
    onehot = (seg[:, None] == jnp.arange(n_seg, dtype=jnp.int32)[None, :]).astype(F32)
    seg_start = jnp.dot(onehot, tile_start.astype(F32), precision=lax.Precision.HIGHEST)
    pos = jnp.round(seg_start).astype(jnp.int32) * tm + rank
    n_used = tile_end[-1:].astype(jnp.int32)
    tile_seg = jnp.minimum(
        jnp.sum((jnp.arange(nt, dtype=jnp.int32)[:, None] >= tile_end[None, :]).astype(jnp.int32), axis=1),
        n_seg - 1).astype(jnp.int32)
    tile_first = base + (tile_seg // MOE_PAIRS) * MOE_EPG
    exp_a = tile_first + jnp.asarray(MOE_PAIR_A, jnp.int32)[tile_seg % MOE_PAIRS]
    exp_b = tile_first + jnp.asarray(MOE_PAIR_B, jnp.int32)[tile_seg % MOE_PAIRS]
    token = jnp.full((nt * tm,), -1, jnp.int32).at[pos].set(jnp.arange(t, dtype=jnp.int32))
    r = jnp.arange(nt * tm, dtype=jnp.int32)
    src = jnp.where(token >= 0, token, r % t)
    dst = jnp.where(token >= 0, token, t + (r // tm % MOE_SCATTER_SLOTS) * tm + r % tm)
    x_sorted = _gather_sorted(h, gain, src, n_used, nt)
    comb_sorted = jnp.take(route[:, :8], src, axis=0, mode='clip')
    return _moe_sorted(x_sorted, comb_sorted, exp_a, exp_b, n_used, dst, t, w_gate, w_up, w_down)


def _conv_in_kernel(a_ref, wb_ref, wc_ref, wu_ref, cw_ref, o_ref, carry_ref, wbf_ref, *, tm, seq, rb):
    i = pl.program_id(1)

    @pl.when(i == 0)
    def _():
        wbf_ref[0] = wb_ref[...].astype(BF16)
        wbf_ref[1] = wc_ref[...].astype(BF16)
        wbf_ref[2] = wu_ref[...].astype(BF16)

    @pl.when((i * tm) % seq == 0)
    def _():
        carry_ref[...] = jnp.zeros(carry_ref.shape, F32)

    prev = carry_ref[...]
    cw = cw_ref[...]
    for r in range(0, tm, rb):
        rows = slice(r, r + rb)
        a = a_ref[rows, :]
        gate_b = jnp.dot(a, wbf_ref[0], preferred_element_type=F32)
        gate_c = jnp.dot(a, wbf_ref[1], preferred_element_type=F32)
        u = jnp.dot(a, wbf_ref[2], preferred_element_type=F32)
        z = gate_c * u
        rid = lax.broadcasted_iota(jnp.int32, z.shape, 0)
        z1 = jnp.where(rid == 0, prev[7:8], pltpu.roll(z, 1, 0))
        z2 = jnp.where(rid == 0, prev[6:7], jnp.where(rid == 1, prev[7:8], pltpu.roll(z, 2, 0)))
        conv = cw[0:1] * z2 + cw[1:2] * z1 + cw[2:3] * z
        o_ref[rows, :] = (gate_b * conv).astype(o_ref.dtype)
        prev = z[rb - 8:rb]
    carry_ref[...] = prev


def _conv_in(a, w_in, j, conv_w, seq, tm=1024, tn=256, rb=256):
    t, d = a.shape
    nb = d // tn
    return pl.pallas_call(
        functools.partial(_conv_in_kernel, tm=tm, seq=seq, rb=min(rb, tm)),
        grid=(nb, t // tm),
        in_specs=[pl.BlockSpec((tm, d), lambda n, i: (i, 0)),
                  pl.BlockSpec((None, d, tn), lambda n, i: (j, 0, n)),
                  pl.BlockSpec((None, d, tn), lambda n, i: (j, 0, n + nb)),
                  pl.BlockSpec((None, d, tn), lambda n, i: (j, 0, n + 2 * nb)),
                  pl.BlockSpec((CONV_WIDTH, tn), lambda n, i: (0, n))],
        out_specs=pl.BlockSpec((tm, tn), lambda n, i: (i, n)),
        out_shape=jax.ShapeDtypeStruct((t, d), BF16),
        scratch_shapes=[pltpu.VMEM((8, tn), F32), pltpu.VMEM((3, d, tn), BF16)],
        compiler_params=_cparams(("arbitrary", "arbitrary")),
    )(a, w_in, w_in, w_in, conv_w)


def _build_decay_mats():
    t = np.arange(CHUNK)
    i = t[:, None]
    tt = t[None, :]
    mats = [tt <= i, tt > i]
    for l in range(N_LEVELS):
        s = 1 << l
        ref = ((t // (2 * s)) * (2 * s) + s - 1)[:, None]
        upper = ((t % (2 * s)) >= s)[:, None]
        mats.append(np.where(upper, (tt > ref) & (tt <= i), (tt > i) & (tt <= ref)))
    return np.concatenate(mats, axis=0).astype(np.float32)


def _build_level_matrix():
    lv = np.full((CHUNK, CHUNK), -1, np.int32)
    for i in range(CHUNK):
        lv[i, i] = N_LEVELS
        for j in range(i):
            lv[i, j] = int(np.floor(np.log2(i ^ j)))
    return lv


_DECAY_MATS = _build_decay_mats()
_LEVEL_MAT = _build_level_matrix()

FAST_BLOCK = CHUNK // 2
FAST_LIMIT = 96.0


def _build_fast_mats():
    t = np.arange(CHUNK)
    i = t[:, None]
    tt = t[None, :]
    ref = ((t // FAST_BLOCK) * FAST_BLOCK + FAST_BLOCK // 2 - 1)[:, None]
    mid = ((tt > ref) & (tt <= i)).astype(np.float32) - ((tt > i) & (tt <= ref)).astype(np.float32)
    top = _DECAY_MATS[(1 + N_LEVELS) * CHUNK:(2 + N_LEVELS) * CHUNK]
    return np.concatenate([_DECAY_MATS[:2 * CHUNK], top, mid], axis=0).astype(np.float32)


def _build_fast_level_matrix():
    i = np.arange(CHUNK)[:, None]
    j = np.arange(CHUNK)[None, :]
    same = (i // FAST_BLOCK) == (j // FAST_BLOCK)
    return np.where(same & (i >= j), 0, np.where((i // FAST_BLOCK) > (j // FAST_BLOCK), 1, -1)).astype(np.int32)


_FAST_MATS = _build_fast_mats()
_FAST_LEVEL_MAT = _build_fast_level_matrix()

_NT = (((1,), (1,)), ((), ()))
_TN = (((0,), (0,)), ((), ()))


def _rec_kernel(q_ref, k_ref, la_ref, v_ref, sg_ref, gain_ref, mall_ref, lvl_ref, mfast_ref, lvlf_ref,
                o_ref, st_ref, *, nh, dk, dv, ts, seq, unroll):
    t = pl.program_id(1)

    @pl.when((t * ts) % seq == 0)
    def _():
        st_ref[...] = jnp.zeros_like(st_ref)

    gain = gain_ref[...]

    def decays(mats, rows):
        la = la_ref[rows, :] * LOG2_E
        la_hi = la.astype(BF16)
        la_lo = (la - la_hi.astype(F32)).astype(BF16)
        return jnp.dot(mats, jnp.concatenate([la_hi, la_lo], axis=0), preferred_element_type=F32)

    def finish(h, rows, q, k, g, suffix, scores):
        vc = slice(h * dv, (h + 1) * dv)
        v = v_ref[rows, vc]
        state_t = st_ref[h]
        o = (jnp.dot(scores.astype(BF16), v, preferred_element_type=F32)
             + lax.dot_general((q * jnp.exp2(g)).astype(BF16), state_t.astype(BF16), _NT,
                               preferred_element_type=F32))
        k_end = (k * jnp.exp2(suffix)).astype(BF16)
        st_ref[h] = (state_t * jnp.exp2(g[CHUNK - 1:CHUNK])
                     + lax.dot_general(v, k_end, _TN, preferred_element_type=F32))
        on = _rms_rows(o, gain[:, vc]) * sg_ref[rows, vc].astype(F32)
        o_ref[rows, vc] = on.astype(o_ref.dtype)

    def fast_chunk(c, carry):
        rows = pl.ds(pl.multiple_of(c * CHUNK, CHUNK), CHUNK)
        lvl = lvlf_ref[...]
        dall = decays(mfast_ref[...], rows)
        for h in range(nh):
            kc = slice(h * dk, (h + 1) * dk)
            q = q_ref[rows, kc].astype(F32)
            k = k_ref[rows, kc].astype(F32)
            e_top = jnp.exp2(dall[2 * CHUNK:3 * CHUNK, kc])
            d_mid = dall[3 * CHUNK:4 * CHUNK, kc]
            r_top = lax.dot_general((q * e_top).astype(BF16), (k * e_top).astype(BF16), _NT,
                                    preferred_element_type=F32)
            r_mid = lax.dot_general((q * jnp.exp2(d_mid)).astype(BF16), (k * jnp.exp2(-d_mid)).astype(BF16),
                                    _NT, preferred_element_type=F32)
            scores = jnp.where(lvl == 1, r_top, jnp.where(lvl == 0, r_mid, 0.0))
            finish(h, rows, q, k, dall[0:CHUNK, kc], dall[CHUNK:2 * CHUNK, kc], scores)
        return carry

    def exact_chunk(c, carry):
        rows = pl.ds(pl.multiple_of(c * CHUNK, CHUNK), CHUNK)
        lvl = lvl_ref[...]
        dall = decays(mall_ref[...], rows)
        for h in range(nh):
            kc = slice(h * dk, (h + 1) * dk)
            q = q_ref[rows, kc].astype(F32)
            k = k_ref[rows, kc].astype(F32)
            scores = jnp.where(
                lvl == N_LEVELS,
                lax.dot_general(q.astype(BF16), k.astype(BF16), _NT, preferred_element_type=F32), 0.0)
            for l in range(N_LEVELS):
                e = jnp.exp2(dall[(2 + l) * CHUNK:(3 + l) * CHUNK, kc])
                r = lax.dot_general((q * e).astype(BF16), (k * e).astype(BF16), _NT,
                                    preferred_element_type=F32)
                scores = jnp.where(lvl == l, r, scores)
            finish(h, rows, q, k, dall[0:CHUNK, kc], dall[CHUNK:2 * CHUNK, kc], scores)
        return carry

    half = FAST_BLOCK // 2
    mass = jnp.sum(jnp.abs(la_ref[...]).reshape(ts // half, half, nh * dk), axis=1)
    safe = jnp.max(mass) * LOG2_E < FAST_LIMIT

    @pl.when(safe)
    def _():
        lax.fori_loop(0, ts // CHUNK, fast_chunk, 0, unroll=2 * unroll)

    @pl.when(jnp.logical_not(safe))
    def _():
        lax.fori_loop(0, ts // CHUNK, exact_chunk, 0, unroll=unroll)


def _recurrence(q, k, la, v, sg, gain, *, dk, dv, nh, seq, ts=1024, unroll=2):
    t = q.shape[0]
    heads = q.shape[1] // dk
    ts = min(ts, seq)
    dup = lambda m: jnp.asarray(np.concatenate([m, m], axis=1), BF16)
    mall, mfast = dup(_DECAY_MATS), dup(_FAST_MATS)
    lvl, lvlf = jnp.asarray(_LEVEL_MAT), jnp.asarray(_FAST_LEVEL_MAT)
    kspec = pl.BlockSpec((ts, nh * dk), lambda hg, s: (s, hg))
    vspec = pl.BlockSpec((ts, nh * dv), lambda hg, s: (s, hg))
    const = lambda arr: pl.BlockSpec(arr.shape, lambda hg, s: (0, 0))
    return pl.pallas_call(
        functools.partial(_rec_kernel, nh=nh, dk=dk, dv=dv, ts=ts, seq=seq, unroll=unroll),
        grid=(heads // nh, t // ts),
        in_specs=[kspec, kspec, kspec, vspec, vspec,
                  pl.BlockSpec((1, nh * dv), lambda hg, s: (0, hg)),
                  const(mall), const(lvl), const(mfast), const(lvlf)],
        out_specs=vspec,
        out_shape=jax.ShapeDtypeStruct((t, heads * dv), BF16),
        scratch_shapes=[pltpu.VMEM((nh, dv, dk), F32)],
        compiler_params=_cparams(("parallel", "arbitrary")),
    )(q, k, la, v, sg, gain.reshape(1, -1), mall, lvl, mfast, lvlf)


def _gelu_tanh(x):
    return jax.nn.gelu(x, approximate=True)


def _gmlp_spatial_kernel(u_ref, v_ref, lng_ref, lnb_ref, ws_ref, bs_ref, o_ref, *, groups):
    v = v_ref[...].astype(F32)
    mu = jnp.mean(v, axis=-1, keepdims=True)
    vc = v - mu
    var = jnp.mean(vc * vc, axis=-1, keepdims=True)
    vn = (vc * lax.rsqrt(var + EPS) * lng_ref[...] + lnb_ref[...]).astype(BF16)
    blk = v.shape[0]
    gw = v.shape[1] // groups
    ri = lax.broadcasted_iota(jnp.int32, (blk, blk), 0)
    ci = lax.broadcasted_iota(jnp.int32, (blk, blk), 1)
    for g in range(groups):
        cols = slice(g * gw, (g + 1) * gw)
        wc = jnp.where(ci <= ri, ws_ref[g], 0.0).astype(BF16)
        mixed = jnp.dot(wc, vn[:, cols], preferred_element_type=F32) + bs_ref[g]
        o_ref[:, cols] = (u_ref[:, cols].astype(F32) * mixed).astype(o_ref.dtype)


def _gmlp_spatial(u, v, ln_g, ln_b, w_s, b_s):
    t, w = u.shape
    groups, blk = w_s.shape[0], w_s.shape[1]
    rows = pl.BlockSpec((blk, w), lambda i: (i, 0))
    return pl.pallas_call(
        functools.partial(_gmlp_spatial_kernel, groups=groups),
        grid=(t // blk,),
        in_specs=[rows, rows,
                  pl.BlockSpec((1, w), lambda i: (0, 0)),
                  pl.BlockSpec((1, w), lambda i: (0, 0)),
                  pl.BlockSpec((groups, blk, blk), lambda i: (0, 0, 0)),
                  pl.BlockSpec((groups, blk, 1), lambda i: (0, 0, 0))],
        out_specs=rows,
        out_shape=jax.ShapeDtypeStruct((t, w), BF16),
        compiler_params=_cparams(("parallel",)),
    )(u, v, ln_g.reshape(1, w), ln_b.reshape(1, w), w_s, b_s.reshape(groups, blk, 1))


def _epi_residual(accs, extras):
    return [extras[0] + accs[0]]


def _epi_hgrn(accs, extras):
    q, f, i, g = accs
    lb = extras[0]
    e = jnp.exp(-jnp.abs(f))
    log_sig = jnp.minimum(f, 0.0) - jnp.log1p(e)
    a = jnp.log(lb)
    b = jnp.log1p(-lb) + log_sig
    log_forget = jnp.maximum(a, b) + jnp.log1p(jnp.exp(-jnp.abs(a - b)))
    key = (1.0 - lb) * (jnp.where(f >= 0.0, e, 1.0) / (1.0 + e))
    return [_silu(q), key, log_forget, i, _silu(g)]


def _epi_gelu2(accs, extras):
    return [_gelu_tanh(accs[0]), _gelu_tanh(accs[1])]


def _epi_scale(scale):
    return lambda accs, extras: [accs[0] * scale]


def _epi_identity(accs, extras):
    return [accs[0]]


def _epi_silu(accs, extras):
    return [_silu(accs[0])]


def _epi_gla_gate(accs, extras):
    z = accs[0] + extras[0]
    log_sig = jnp.minimum(z, 0.0) - jnp.log1p(jnp.exp(-jnp.abs(z)))
    return [log_sig / GLA_GATE_TAU]


def _epi_first_lanes(width):
    def epi(accs, extras):
        lane = lax.broadcasted_iota(jnp.int32, accs[0].shape, 1)
        return [jnp.where(lane < width, accs[0], 0.0)]
    return epi


def _epi_ple(accs, extras):
    h, y, p, proj = extras
    emb = jnp.dot(p.astype(BF16), proj.astype(BF16), preferred_element_type=F32)
    return [(h + y) + emb * _sigmoid(accs[0])]


def kernel(x, p, norm_mix, norm_ffn, norm_final, conv_w_in, conv_w, conv_w_out, hgrn_w_in, hgrn_lb, hgrn_norm, hgrn_w_out, gla_w_in, gla_w_a2, gla_b_a, gla_norm, gla_w_out, gmlp_w_in, gmlp_ln_g, gmlp_ln_b, gmlp_w_s, gmlp_b_s, gmlp_w_out, moe_w_group, moe_w_expert, moe_w_gate, moe_w_up, moe_w_down, ple_w_proj, ple_w_gate):
    batch, seq, d = x.shape
    depth = p.shape[0]
    t = batch * seq
    ff = moe_w_gate.shape[-1]
    n_exp = moe_w_gate.shape[1]

    lb_w = jax.nn.softmax(hgrn_lb.astype(F32), axis=0)
    lower_bounds = jnp.cumsum(lb_w, axis=0) - lb_w[0]
    w_gate_all = moe_w_gate.reshape(depth * n_exp, d, ff)
    w_up_all = moe_w_up.reshape(depth * n_exp, d, ff)
    w_down_all = moe_w_down.reshape(depth * n_exp, ff, d)
    p_all = p.reshape(depth, t, p.shape[-1])

    h = x.reshape(t, d)
    for layer in range(depth):
        kind, j = layer % 4, layer // 4
        a = _rmsnorm(h, norm_mix[layer], BF16)
        if kind == 0:
            gated = _conv_in(a, conv_w_in, j, conv_w[j], seq)
            w_out = conv_w_out
        elif kind == 1:
            nb = d // 256
            qt, key, log_f, val, sg = _mm(
                a, [(hgrn_w_in, j, o * nb) for o in range(4)],
                [(lower_bounds.reshape(depth, 1, d), 'row', layer)],
                [BF16, BF16, F32, BF16, BF16], _epi_hgrn, n_cols=d, tm=1024, tn=256, rb=128)
            gated = _recurrence(qt, key, log_f, val, sg, hgrn_norm[j],
                                dk=HGRN_DK, dv=HGRN_DK, nh=4, seq=seq)
            w_out = hgrn_w_out
        elif kind == 2:
            kd = d // 2
            dk = kd // GLA_HEADS
            dv = d // GLA_HEADS
            tn = 1024
            (q,) = _mm(a, [(gla_w_in, j, 0)], [], [BF16], _epi_scale(dk ** -0.5),
                       n_cols=kd, tm=512, tn=tn)
            (k,) = _mm(a, [(gla_w_in, j, kd // tn)], [], [BF16], _epi_identity,
                       n_cols=kd, tm=512, tn=tn)
            (v,) = _mm(a, [(gla_w_in, j, 2 * kd // tn)], [], [BF16], _epi_identity,
                       n_cols=d, tm=512, tn=tn)
            (sg,) = _mm(a, [(gla_w_in, j, (2 * kd + d) // tn)], [], [BF16], _epi_silu,
                        n_cols=d, tm=512, tn=tn)
            (a_low,) = _mm(a, [(gla_w_in, j, (2 * kd + 2 * d) // LANES)], [], [BF16],
                           _epi_first_lanes(GLA_GATE_RANK), n_cols=LANES, tm=512, tn=LANES)
            w_a2 = jnp.pad(gla_w_a2, ((0, 0), (0, LANES - GLA_GATE_RANK), (0, 0)))
            (log_a,) = _mm(a_low, [(w_a2, j, 0)], [(gla_b_a.reshape(-1, 1, kd), 'row', j)], [F32],
                           _epi_gla_gate, n_cols=kd, tm=1024, tn=tn)
            gated = _recurrence(q, k, log_a, v, sg, gla_norm[j], dk=dk, dv=dv, nh=1, seq=seq)
            w_out = gla_w_out
        else:
            wd = gmlp_w_in.shape[-1] // 2
            u, v = _mm(a, [(gmlp_w_in, j, 0), (gmlp_w_in, j, wd // 512)], [], [BF16, BF16], _epi_gelu2,
                       n_cols=wd, tm=1024, tn=512)
            gated = _gmlp_spatial(u, v, gmlp_ln_g[j], gmlp_ln_b[j], gmlp_w_s[j], gmlp_b_s[j])
            w_out = gmlp_w_out
        (h,) = _mm(gated, [(w_out, j, 0)], [(h, 'tile', None)], [F32], _epi_residual,
                   n_cols=d, tm=512, tn=512 if gated.shape[1] > d else 1024)

        w_router = jnp.pad(jnp.concatenate([moe_w_group[layer], moe_w_expert[layer]], axis=1),
                           ((0, 0), (0, LANES - MOE_GROUPS - MOE_GROUPS * MOE_EPG)))
        c, route, seg_counts = _rms_router(h, norm_ffn[layer], w_router)
        y = _moe_layer(h, route, seg_counts, norm_ffn[layer], w_gate_all, w_up_all, w_down_all, layer * n_exp)
        (h,) = _mm(c, [(ple_w_gate, layer, 0)],
                   [(h, 'tile', None), (y, 'tile', None), (p_all, 'rowtile', layer),
                    (ple_w_proj, 'kcol', layer)],
                   [F32], _epi_ple, n_cols=d, tm=512, tn=1024)
    out = _rmsnorm(h, norm_final, F32)
    return out.reshape(batch, seq, d)
```

```python
import functools

import numpy as np
import jax
import jax.numpy as jnp
from jax import lax
from jax.experimental import pallas as pl
from jax.experimental.pallas import tpu as pltpu

F32 = jnp.float32
BF16 = jnp.bfloat16

EPS = 1e-6
CHUNK = 128
N_LEVELS = 7
LOG2_E = 1.4426950408889634
LANES = 128
VMEM_LIMIT = 56 * 1024 * 1024

CONV_WIDTH = 3
HGRN_DK = 128
GLA_HEADS = 8
GLA_GATE_RANK = 16
GLA_GATE_TAU = 16.0
GMLP_BLOCK = 128
GMLP_GROUPS = 8
MOE_GROUPS = 4
MOE_EPG = 4
MOE_TOP_K = 2
MOE_PAIRS = 6
MOE_PAIR_A = (0, 0, 0, 1, 1, 2)
MOE_PAIR_B = (1, 2, 3, 2, 3, 3)
MOE_TILE = 256
MOE_SCATTER_SLOTS = 3


def _cparams(sem):
    return pltpu.CompilerParams(dimension_semantics=sem, vmem_limit_bytes=VMEM_LIMIT)


def _sigmoid(x):
    return jax.nn.sigmoid(x)


def _silu(x):
    return x * jax.nn.sigmoid(x)


def _mm_kernel(*refs, n_w, n_e, epilogue, kinds, rb):
    a_ref = refs[0]
    w_refs = refs[1:1 + n_w]
    e_refs = refs[1 + n_w:1 + n_w + n_e]
    o_refs = refs[1 + n_w + n_e:-1]
    wbf_ref = refs[-1]

    @pl.when(pl.program_id(1) == 0)
    def _():
        for j, w in enumerate(w_refs):
            wbf_ref[j] = w[...].astype(BF16)

    tm = a_ref.shape[0]
    for r in range(0, tm, rb):
        rows = slice(r, r + rb)
        a = a_ref[rows, :]
        accs = [jnp.dot(a, wbf_ref[j], preferred_element_type=F32) for j in range(n_w)]
        ex = [e[rows, :] if kind in ('tile', 'rowtile') else e[...] for e, kind in zip(e_refs, kinds)]
        res = epilogue(accs, ex)
        for o, val in zip(o_refs, res):
            o[rows, :] = val.astype(o.dtype)


def _mm(a, ws, extras, out_dtypes, epilogue, *, n_cols, tm, tn, rb=256):
    m, k = a.shape
    grid = (n_cols // tn, m // tm)
    in_specs = [pl.BlockSpec((tm, k), lambda n, i: (i, 0))]
    args = [a]
    for w, lead, off in ws:
        in_specs.append(pl.BlockSpec((None, k, tn), lambda n, i, lead=lead, off=off: (lead, 0, n + off),
                                     pipeline_mode=pl.Buffered(1)))
        args.append(w)
    for arr, kind, lead in extras:
        if kind == 'row':
            in_specs.append(pl.BlockSpec((None, 1, tn), lambda n, i, lead=lead: (lead, 0, n)))
        elif kind == 'tile':
            assert lead is None
            in_specs.append(pl.BlockSpec((tm, tn), lambda n, i: (i, n)))
        elif kind == 'rowtile':
            in_specs.append(pl.BlockSpec((None, tm, arr.shape[2]), lambda n, i, lead=lead: (lead, i, 0)))
        elif kind == 'kcol':
            in_specs.append(pl.BlockSpec((None, arr.shape[1], tn), lambda n, i, lead=lead: (lead, 0, n)))
        else:
            raise ValueError(kind)
        args.append(arr)
    out_shape = [jax.ShapeDtypeStruct((m, n_cols), dt) for dt in out_dtypes]
    out_specs = [pl.BlockSpec((tm, tn), lambda n, i: (i, n)) for _ in out_dtypes]
    res = pl.pallas_call(
        functools.partial(_mm_kernel, n_w=len(ws), n_e=len(extras), epilogue=epilogue,
                          kinds=tuple(kind for _, kind, _ in extras), rb=min(rb, tm)),
        grid=grid, in_specs=in_specs, out_specs=out_specs, out_shape=out_shape,
        scratch_shapes=[pltpu.VMEM((len(ws), k, tn), BF16)],
        compiler_params=_cparams(("arbitrary", "arbitrary")),
    )(*args)
    return res


def _rms_rows(x, gain):
    ms = jnp.mean(x * x, axis=-1, keepdims=True)
    return x * lax.rsqrt(ms + EPS) * gain


def _rms_kernel(h_ref, g_ref, o_ref):
    o_ref[...] = _rms_rows(h_ref[...], g_ref[...]).astype(o_ref.dtype)


def _rmsnorm(h, gain, out_dtype, tm=256):
    t, d = h.shape
    return pl.pallas_call(
        _rms_kernel, grid=(t // tm,),
        in_specs=[pl.BlockSpec((tm, d), lambda i: (i, 0)), pl.BlockSpec((1, d), lambda i: (0, 0))],
        out_specs=pl.BlockSpec((tm, d), lambda i: (i, 0)),
        out_shape=jax.ShapeDtypeStruct((t, d), out_dtype),
        compiler_params=_cparams(("parallel",)),
    )(h, gain.reshape(1, d))


def _first_max_index(vals, m):
    idx = jnp.full(m.shape, len(vals) - 1, jnp.int32)
    for j in range(len(vals) - 2, -1, -1):
        idx = jnp.where(vals[j] == m, j, idx)
    return idx


def _route_cols(logits):
    col = lambda j: logits[:, j:j + 1]
    gl = [col(j) for j in range(MOE_GROUPS)]
    gm = functools.reduce(jnp.maximum, gl)
    gsum = functools.reduce(lambda a, b: a + b, [jnp.exp(g - gm) for g in gl])
    group_p = 1.0 / gsum
    gidx = _first_max_index(gl, gm)
    ig = []
    for e in range(MOE_EPG):
        v = col(MOE_GROUPS + (MOE_GROUPS - 1) * MOE_EPG + e)
        for g in range(MOE_GROUPS - 2, -1, -1):
            v = jnp.where(gidx == g, col(MOE_GROUPS + g * MOE_EPG + e), v)
        ig.append(v)
    m1 = functools.reduce(jnp.maximum, ig)
    i1 = _first_max_index(ig, m1)
    ig2 = [jnp.where(i1 == e, -jnp.inf, ig[e]) for e in range(MOE_EPG)]
    m2 = functools.reduce(jnp.maximum, ig2)
    i2 = _first_max_index(ig2, m2)
    t = jnp.exp(m2 - m1)
    w1 = 1.0 / (1.0 + t) * group_p
    w2 = t / (1.0 + t) * group_p
    first = i1 < i2
    ea = jnp.where(first, i1, i2)
    eb = jnp.where(first, i2, i1)
    wa = jnp.where(first, w1, w2)
    wb = jnp.where(first, w2, w1)
    pair = jnp.where(ea == 0, 0, jnp.where(ea == 1, 3, 5)) + (eb - ea - 1)
    seg = gidx * MOE_PAIRS + pair
    lane = lax.broadcasted_iota(jnp.int32, logits.shape, 1)
    return jnp.where(lane == 0, wa, jnp.where(lane == 1, wb, jnp.where(lane == 2, seg.astype(F32), 0.0)))


def _rms_router_kernel(h_ref, g_ref, whi_ref, wlo_ref, c_ref, route_ref, counts_ref, carry_ref):
    @pl.when(pl.program_id(0) == 0)
    def _():
        carry_ref[...] = jnp.zeros_like(carry_ref)

    y = _rms_rows(h_ref[...], g_ref[...])
    c = y.astype(BF16)
    c_ref[...] = c
    y_lo = (y - c.astype(F32)).astype(BF16)
    whi = whi_ref[...]
    logits = (jnp.dot(c, whi, preferred_element_type=F32)
              + (jnp.dot(c, wlo_ref[...], preferred_element_type=F32)
                 + jnp.dot(y_lo, whi, preferred_element_type=F32)))
    route = _route_cols(logits)

    tm = route.shape[0]
    lane = lax.broadcasted_iota(jnp.int32, route.shape, 1)
    onehot = (lane == route[:, 2:3].astype(jnp.int32)).astype(F32)
    ri = lax.broadcasted_iota(jnp.int32, (tm, tm), 0)
    ci = lax.broadcasted_iota(jnp.int32, (tm, tm), 1)
    earlier = jnp.where(ci < ri, 1.0, 0.0).astype(BF16)
    before = jnp.dot(earlier, onehot.astype(BF16), preferred_element_type=F32) + carry_ref[0:1]
    rank = jnp.sum(onehot * before, axis=1, keepdims=True)
    route_ref[...] = jnp.where(lane == 3, rank, route)
    carry_ref[...] = carry_ref[...] + jnp.sum(onehot, axis=0, keepdims=True)
    counts_ref[...] = carry_ref[...]


def _rms_router(h, gain, w_router, tm=512):
    t, d = h.shape
    w_hi = w_router.astype(BF16)
    w_lo = (w_router - w_hi.astype(F32)).astype(BF16)
    return pl.pallas_call(
        _rms_router_kernel, grid=(t // tm,),
        in_specs=[pl.BlockSpec((tm, d), lambda i: (i, 0)),
                  pl.BlockSpec((1, d), lambda i: (0, 0)),
                  pl.BlockSpec((d, LANES), lambda i: (0, 0)),
                  pl.BlockSpec((d, LANES), lambda i: (0, 0))],
        out_specs=[pl.BlockSpec((tm, d), lambda i: (i, 0)),
                   pl.BlockSpec((tm, LANES), lambda i: (i, 0)),
                   pl.BlockSpec((8, LANES), lambda i: (0, 0))],
        out_shape=[jax.ShapeDtypeStruct((t, d), BF16),
                   jax.ShapeDtypeStruct((t, LANES), F32),
                   jax.ShapeDtypeStruct((8, LANES), F32)],
        scratch_shapes=[pltpu.VMEM((8, LANES), F32)],
        compiler_params=_cparams(("arbitrary",)),
    )(h, gain.reshape(1, d), w_hi, w_lo)


def _gather_kernel(src_ref, nu_ref, h_hbm, gain_ref, x_ref, hbuf0, hbuf1, hsem, *, tm):
    bufs = (hbuf0, hbuf1)
    i = pl.program_id(0)
    nu = nu_ref[0]

    def issue(tile, slot):
        group = 8

        def body(b, carry):
            for j in range(group):
                r = b * group + j
                tok = src_ref[tile * tm + r]
                pltpu.make_async_copy(h_hbm.at[pl.ds(tok, 1), :], bufs[slot].at[pl.ds(r, 1), :],
                                      hsem.at[slot]).start(priority=j % 2)
            return carry
        lax.fori_loop(0, tm // group, body, 0)

    def finish(slot):
        pltpu.make_async_copy(h_hbm.at[pl.ds(0, tm), :], bufs[slot], hsem.at[slot]).wait()
        x_ref[...] = _rms_rows(bufs[slot][...], gain_ref[...]).astype(x_ref.dtype)

    @pl.when(i == 0)
    def _():
        issue(0, 0)

    for slot in range(2):
        @pl.when((i + 1 < nu) & (i % 2 == slot))
        def _(slot=slot):
            issue(i + 1, 1 - slot)
            finish(slot)

        @pl.when((i + 1 == nu) & (i % 2 == slot))
        def _(slot=slot):
            finish(slot)

    @pl.when(i >= nu)
    def _():
        x_ref[...] = jnp.zeros_like(x_ref)


def _gather_sorted(h, gain, src, n_used, nt):
    t, d = h.shape
    tm = MOE_TILE
    return pl.pallas_call(
        functools.partial(_gather_kernel, tm=tm),
        grid_spec=pltpu.PrefetchScalarGridSpec(
            num_scalar_prefetch=2, grid=(nt,),
            in_specs=[pl.BlockSpec(memory_space=pl.ANY),
                      pl.BlockSpec((1, d), lambda i, src, nu: (0, 0))],
            out_specs=pl.BlockSpec((tm, d), lambda i, src, nu: (i, 0)),
            scratch_shapes=[pltpu.VMEM((tm, d), F32), pltpu.VMEM((tm, d), F32),
                            pltpu.SemaphoreType.DMA((2,))],
        ),
        out_shape=jax.ShapeDtypeStruct((nt * tm, d), BF16),
        compiler_params=_cparams(("arbitrary",)),
    )(src, n_used, h, gain.reshape(1, d))


def _moe_up_kernel(ea_ref, eb_ref, nu_ref, x_ref, comb_ref, wg_hbm, wu_hbm, o_ref,
                   wf32_ref, wbf_ref, wsem, run_ref):
    s = pl.program_id(0)
    i = pl.program_id(1)
    nu = nu_ref[0]

    def expert(slot, tile):
        return jnp.where(slot == 0, ea_ref[tile], eb_ref[tile])

    def weight_copies(e, buf):
        return (pltpu.make_async_copy(wg_hbm.at[e], wf32_ref.at[buf, 0], wsem.at[buf]),
                pltpu.make_async_copy(wu_hbm.at[e], wf32_ref.at[buf, 1], wsem.at[buf]))

    @pl.when(i < nu)
    def _():
        cur = expert(s, i)
        first = (s == 0) & (i == 0)
        prev = jnp.where(i > 0, expert(s, jnp.maximum(i - 1, 0)), expert(0, nu - 1))

        @pl.when(first)
        def _():
            run_ref[0] = 0
            for cp in weight_copies(cur, 0):
                cp.start()

        @pl.when(first | (cur != prev))
        def _():
            buf = run_ref[0] % 2
            for cp in weight_copies(cur, buf):
                cp.wait()
            wbf_ref[0] = wf32_ref[buf, 0].astype(BF16)
            wbf_ref[1] = wf32_ref[buf, 1].astype(BF16)

            def expert_at(k):
                slot = (k >= nu).astype(jnp.int32)
                return expert(slot, k - slot * nu)

            end = MOE_TOP_K * nu
            nxt = lax.while_loop(lambda k: (k < end) & (expert_at(jnp.minimum(k, end - 1)) == cur),
                                 lambda k: k + 1, s * nu + i + 1)

            @pl.when(nxt < end)
            def _():
                for cp in weight_copies(expert_at(nxt), 1 - buf):
                    cp.start()

            run_ref[0] = run_ref[0] + 1

        x = x_ref[...]
        g = jnp.dot(x, wbf_ref[0], preferred_element_type=F32)
        u = jnp.dot(x, wbf_ref[1], preferred_element_type=F32)
        comb = comb_ref[...]
        lane = lax.broadcasted_iota(jnp.int32, comb.shape, 1)
        sc = jnp.sum(jnp.where(lane == s, comb, 0.0), axis=1, keepdims=True)
        o_ref[...] = (_silu(g) * u * sc).astype(o_ref.dtype)

    @pl.when(i >= nu_ref[0])
    def _():
        o_ref[...] = jnp.zeros_like(o_ref)


def _moe_down_kernel(ea_ref, eb_ref, nu_ref, dst_ref, hid_ref, wda_ref, wdb_ref, y_hbm, *scratch,
                     tm, nt, t, ff):
    obufs = scratch[:MOE_SCATTER_SLOTS]
    wbf_ref, sem = scratch[MOE_SCATTER_SLOTS:]
    i = pl.program_id(0)
    nu = nu_ref[0]
    n_slots = len(obufs)

    def wait_slot(slot):
        pltpu.make_async_copy(obufs[slot], y_hbm.at[pl.ds(0, tm), :], sem.at[slot]).wait()

    def compute(slot):
        prev = jnp.maximum(i - 1, 0)

        @pl.when((i == 0) | (ea_ref[i] != ea_ref[prev]))
        def _():
            wbf_ref[0] = wda_ref[...].astype(BF16)

        @pl.when((i == 0) | (eb_ref[i] != eb_ref[prev]))
        def _():
            wbf_ref[1] = wdb_ref[...].astype(BF16)

        obufs[slot][...] = (jnp.dot(hid_ref[:, :ff], wbf_ref[0], preferred_element_type=F32)
                            + jnp.dot(hid_ref[:, ff:], wbf_ref[1], preferred_element_type=F32))

    def scatter(tile, slot):
        for r in range(tm):
            pltpu.make_async_copy(obufs[slot].at[pl.ds(r, 1), :],
                                  y_hbm.at[pl.ds(dst_ref[tile * tm + r], 1), :], sem.at[slot]).start(priority=r % 2)

    @pl.when(i == 0)
    def _():
        obufs[0][...] = jnp.zeros(obufs[0].shape, F32)
        for part in range(n_slots):
            zero_copy = pltpu.make_async_copy(obufs[0], y_hbm.at[pl.ds(t + part * tm, tm), :], sem.at[0])
            zero_copy.start()
            zero_copy.wait()
        compute(0)

    for slot in range(n_slots):
        @pl.when((i >= 1) & (i < nu) & (i % n_slots == slot))
        def _(slot=slot):
            @pl.when(i >= n_slots)
            def _():
                wait_slot(slot)

            compute(slot)
            scatter(i - 1, (slot - 1) % n_slots)

        @pl.when((i == nu) & ((nu - 1) % n_slots == slot))
        def _(slot=slot):
            scatter(nu - 1, slot)

    @pl.when(i == nt - 1)
    def _():
        for slot in range(n_slots):
            wait_slot(slot)


def _moe_sorted(x_sorted, comb_sorted, exp_a, exp_b, n_used, dst, t, w_gate, w_up, w_down):
    tp, d = x_sorted.shape
    ff = w_gate.shape[-1]
    tm = MOE_TILE
    nt = tp // tm
    row = lambda i, nu: jnp.minimum(i, nu[0] - 1)
    hidden = pl.pallas_call(
        _moe_up_kernel,
        grid_spec=pltpu.PrefetchScalarGridSpec(
            num_scalar_prefetch=3, grid=(MOE_TOP_K, nt),
            in_specs=[
                pl.BlockSpec((tm, d), lambda s, i, ea, eb, nu: (row(i, nu), 0)),
                pl.BlockSpec((tm, comb_sorted.shape[1]), lambda s, i, ea, eb, nu: (row(i, nu), 0)),
                pl.BlockSpec(memory_space=pl.ANY),
                pl.BlockSpec(memory_space=pl.ANY),
            ],
            out_specs=pl.BlockSpec((tm, ff), lambda s, i, ea, eb, nu: (i, s)),
            scratch_shapes=[pltpu.VMEM((2, 2, d, ff), F32), pltpu.VMEM((2, d, ff), BF16),
                            pltpu.SemaphoreType.DMA((2,)), pltpu.SMEM((1,), jnp.int32)],
        ),
        out_shape=jax.ShapeDtypeStruct((tp, MOE_TOP_K * ff), BF16),
        compiler_params=_cparams(("arbitrary", "arbitrary")),
    )(exp_a, exp_b, n_used, x_sorted, comb_sorted, w_gate, w_up)
    y = pl.pallas_call(
        functools.partial(_moe_down_kernel, tm=tm, nt=nt, t=t, ff=ff),
        grid_spec=pltpu.PrefetchScalarGridSpec(
            num_scalar_prefetch=4, grid=(nt,),
            in_specs=[
                pl.BlockSpec((tm, MOE_TOP_K * ff), lambda i, ea, eb, nu, dst: (row(i, nu), 0)),
                pl.BlockSpec((None, ff, d), lambda i, ea, eb, nu, dst: (ea[i], 0, 0)),
                pl.BlockSpec((None, ff, d), lambda i, ea, eb, nu, dst: (eb[i], 0, 0)),
            ],
            out_specs=pl.BlockSpec(memory_space=pl.ANY),
            scratch_shapes=([pltpu.VMEM((tm, d), F32)] * MOE_SCATTER_SLOTS
                            + [pltpu.VMEM((MOE_TOP_K, ff, d), BF16),
                               pltpu.SemaphoreType.DMA((MOE_SCATTER_SLOTS,))]),
        ),
        out_shape=jax.ShapeDtypeStruct((t + MOE_SCATTER_SLOTS * tm, d), F32),
        compiler_params=_cparams(("arbitrary",)),
    )(exp_a, exp_b, n_used, dst, hidden, w_down, w_down)
    return y


def _moe_layer(h, route, seg_counts, gain, w_gate, w_up, w_down, base):
    t, d = h.shape
    tm = MOE_TILE
    n_seg = MOE_GROUPS * MOE_PAIRS
    assert t % tm == 0 and t // tm >= MOE_SCATTER_SLOTS
    nt = t // tm + n_seg
    seg = route[:, 2].astype(jnp.int32)
    rank = route[:, 3].astype(jnp.int32)
    counts = seg_counts[0, :n_seg].astype(jnp.int32)
    ntiles = (counts + tm - 1) // tm
    tile_end = jnp.cumsum(ntiles)
    tile_start = tile_end - ntiles
    onehot = (seg[:, None] == jnp.arange(n_seg, dtype=jnp.int32)[None, :]).astype(F32)
    seg_start = jnp.dot(onehot, tile_start.astype(F32), precision=lax.Precision.HIGHEST)
    pos = jnp.round(seg_start).astype(jnp.int32) * tm + rank
    n_used = tile_end[-1:].astype(jnp.int32)
    tile_seg = jnp.minimum(
        jnp.sum((jnp.arange(nt, dtype=jnp.int32)[:, None] >= tile_end[None, :]).astype(jnp.int32), axis=1),
        n_seg - 1).astype(jnp.int32)
    tile_first = base + (tile_seg // MOE_PAIRS) * MOE_EPG
    exp_a = tile_first + jnp.asarray(MOE_PAIR_A, jnp.int32)[tile_seg % MOE_PAIRS]
    exp_b = tile_first + jnp.asarray(MOE_PAIR_B, jnp.int32)[tile_seg % MOE_PAIRS]
    token = jnp.full((nt * tm,), -1, jnp.int32).at[pos].set(jnp.arange(t, dtype=jnp.int32))
    r = jnp.arange(nt * tm, dtype=jnp.int32)
    src = jnp.where(token >= 0, token, r % t)
    dst = jnp.where(token >= 0, token, t + (r // tm % MOE_SCATTER_SLOTS) * tm + r % tm)
    x_sorted = _gather_sorted(h, gain, src, n_used, nt)
    comb_sorted = jnp.take(route[:, :8], src, axis=0, mode='clip')
    return _moe_sorted(x_sorted, comb_sorted, exp_a, exp_b, n_used, dst, t, w_gate, w_up, w_down)


def _conv_in_kernel(a_ref, wb_ref, wc_ref, wu_ref, cw_ref, o_ref, carry_ref, wbf_ref, *, tm, seq, rb):
    i = pl.program_id(1)

    @pl.when(i == 0)
    def _():
        wbf_ref[0] = wb_ref[...].astype(BF16)
        wbf_ref[1] = wc_ref[...].astype(BF16)
        wbf_ref[2] = wu_ref[...].astype(BF16)

    @pl.when((i * tm) % seq == 0)
    def _():
        carry_ref[...] = jnp.zeros(carry_ref.shape, F32)

    prev = carry_ref[...]
    cw = cw_ref[...]
    for r in range(0, tm, rb):
        rows = slice(r, r + rb)
        a = a_ref[rows, :]
        gate_b = jnp.dot(a, wbf_ref[0], preferred_element_type=F32)
        gate_c = jnp.dot(a, wbf_ref[1], preferred_element_type=F32)
        u = jnp.dot(a, wbf_ref[2], preferred_element_type=F32)
        z = gate_c * u
        rid = lax.broadcasted_iota(jnp.int32, z.shape, 0)
        z1 = jnp.where(rid == 0, prev[7:8], pltpu.roll(z, 1, 0))
        z2 = jnp.where(rid == 0, prev[6:7], jnp.where(rid == 1, prev[7:8], pltpu.roll(z, 2, 0)))
        conv = cw[0:1] * z2 + cw[1:2] * z1 + cw[2:3] * z
        o_ref[rows, :] = (gate_b * conv).astype(o_ref.dtype)
        prev = z[rb - 8:rb]
    carry_ref[...] = prev


def _conv_in(a, w_in, j, conv_w, seq, tm=1024, tn=256, rb=256):
    t, d = a.shape
    nb = d // tn
    return pl.pallas_call(
        functools.partial(_conv_in_kernel, tm=tm, seq=seq, rb=min(rb, tm)),
        grid=(nb, t // tm),
        in_specs=[pl.BlockSpec((tm, d), lambda n, i: (i, 0)),
                  pl.BlockSpec((None, d, tn), lambda n, i: (j, 0, n)),
                  pl.BlockSpec((None, d, tn), lambda n, i: (j, 0, n + nb)),
                  pl.BlockSpec((None, d, tn), lambda n, i: (j, 0, n + 2 * nb)),
                  pl.BlockSpec((CONV_WIDTH, tn), lambda n, i: (0, n))],
        out_specs=pl.BlockSpec((tm, tn), lambda n, i: (i, n)),
        out_shape=jax.ShapeDtypeStruct((t, d), BF16),
        scratch_shapes=[pltpu.VMEM((8, tn), F32), pltpu.VMEM((3, d, tn), BF16)],
        compiler_params=_cparams(("arbitrary", "arbitrary")),
    )(a, w_in, w_in, w_in, conv_w)


def _build_decay_mats():
    t = np.arange(CHUNK)
    i = t[:, None]
    tt = t[None, :]
    mats = [tt <= i, tt > i]
    for l in range(N_LEVELS):
        s = 1 << l
        ref = ((t // (2 * s)) * (2 * s) + s - 1)[:, None]
        upper = ((t % (2 * s)) >= s)[:, None]
        mats.append(np.where(upper, (tt > ref) & (tt <= i), (tt > i) & (tt <= ref)))
    return np.concatenate(mats, axis=0).astype(np.float32)


def _build_level_matrix():
    lv = np.full((CHUNK, CHUNK), -1, np.int32)
    for i in range(CHUNK):
        lv[i, i] = N_LEVELS
        for j in range(i):
            lv[i, j] = int(np.floor(np.log2(i ^ j)))
    return lv


_DECAY_MATS = _build_decay_mats()
_LEVEL_MAT = _build_level_matrix()

FAST_BLOCK = CHUNK // 2
FAST_LIMIT = 96.0


def _build_fast_mats():
    t = np.arange(CHUNK)
    i = t[:, None]
    tt = t[None, :]
    ref = ((t // FAST_BLOCK) * FAST_BLOCK + FAST_BLOCK // 2 - 1)[:, None]
    mid = ((tt > ref) & (tt <= i)).astype(np.float32) - ((tt > i) & (tt <= ref)).astype(np.float32)
    top = _DECAY_MATS[(1 + N_LEVELS) * CHUNK:(2 + N_LEVELS) * CHUNK]
    return np.concatenate([_DECAY_MATS[:2 * CHUNK], top, mid], axis=0).astype(np.float32)


def _build_fast_level_matrix():
    i = np.arange(CHUNK)[:, None]
    j = np.arange(CHUNK)[None, :]
    same = (i // FAST_BLOCK) == (j // FAST_BLOCK)
    return np.where(same & (i >= j), 0, np.where((i // FAST_BLOCK) > (j // FAST_BLOCK), 1, -1)).astype(np.int32)


_FAST_MATS = _build_fast_mats()
_FAST_LEVEL_MAT = _build_fast_level_matrix()

_NT = (((1,), (1,)), ((), ()))
_TN = (((0,), (0,)), ((), ()))


def _rec_kernel(q_ref, k_ref, la_ref, v_ref, sg_ref, gain_ref, mall_ref, lvl_ref, mfast_ref, lvlf_ref,
                o_ref, st_ref, *, nh, dk, dv, ts, seq, unroll):
    t = pl.program_id(1)

    @pl.when((t * ts) % seq == 0)
    def _():
        st_ref[...] = jnp.zeros_like(st_ref)

    gain = gain_ref[...]

    def decays(mats, rows):
        la = la_ref[rows, :] * LOG2_E
        la_hi = la.astype(BF16)
        la_lo = (la - la_hi.astype(F32)).astype(BF16)
        return jnp.dot(mats, jnp.concatenate([la_hi, la_lo], axis=0), preferred_element_type=F32)

    def finish(h, rows, q, k, g, suffix, scores):
        vc = slice(h * dv, (h + 1) * dv)
        v = v_ref[rows, vc]
        state_t = st_ref[h]
        o = (jnp.dot(scores.astype(BF16), v, preferred_element_type=F32)
             + lax.dot_general((q * jnp.exp2(g)).astype(BF16), state_t.astype(BF16), _NT,
                               preferred_element_type=F32))
        k_end = (k * jnp.exp2(suffix)).astype(BF16)
        st_ref[h] = (state_t * jnp.exp2(g[CHUNK - 1:CHUNK])
                     + lax.dot_general(v, k_end, _TN, preferred_element_type=F32))
        on = _rms_rows(o, gain[:, vc]) * sg_ref[rows, vc].astype(F32)
        o_ref[rows, vc] = on.astype(o_ref.dtype)

    def fast_chunk(c, carry):
        rows = pl.ds(pl.multiple_of(c * CHUNK, CHUNK), CHUNK)
        lvl = lvlf_ref[...]
        dall = decays(mfast_ref[...], rows)
        for h in range(nh):
            kc = slice(h * dk, (h + 1) * dk)
            q = q_ref[rows, kc].astype(F32)
            k = k_ref[rows, kc].astype(F32)
            e_top = jnp.exp2(dall[2 * CHUNK:3 * CHUNK, kc])
            d_mid = dall[3 * CHUNK:4 * CHUNK, kc]
            r_top = lax.dot_general((q * e_top).astype(BF16), (k * e_top).astype(BF16), _NT,
                                    preferred_element_type=F32)
            r_mid = lax.dot_general((q * jnp.exp2(d_mid)).astype(BF16), (k * jnp.exp2(-d_mid)).astype(BF16),
                                    _NT, preferred_element_type=F32)
            scores = jnp.where(lvl == 1, r_top, jnp.where(lvl == 0, r_mid, 0.0))
            finish(h, rows, q, k, dall[0:CHUNK, kc], dall[CHUNK:2 * CHUNK, kc], scores)
        return carry

    def exact_chunk(c, carry):
        rows = pl.ds(pl.multiple_of(c * CHUNK, CHUNK), CHUNK)
        lvl = lvl_ref[...]
        dall = decays(mall_ref[...], rows)
        for h in range(nh):
            kc = slice(h * dk, (h + 1) * dk)
            q = q_ref[rows, kc].astype(F32)
            k = k_ref[rows, kc].astype(F32)
            scores = jnp.where(
                lvl == N_LEVELS,
                lax.dot_general(q.astype(BF16), k.astype(BF16), _NT, preferred_element_type=F32), 0.0)
            for l in range(N_LEVELS):
                e = jnp.exp2(dall[(2 + l) * CHUNK:(3 + l) * CHUNK, kc])
                r = lax.dot_general((q * e).astype(BF16), (k * e).astype(BF16), _NT,
                                    preferred_element_type=F32)
                scores = jnp.where(lvl == l, r, scores)
            finish(h, rows, q, k, dall[0:CHUNK, kc], dall[CHUNK:2 * CHUNK, kc], scores)
        return carry

    half = FAST_BLOCK // 2
    mass = jnp.sum(jnp.abs(la_ref[...]).reshape(ts // half, half, nh * dk), axis=1)
    safe = jnp.max(mass) * LOG2_E < FAST_LIMIT

    @pl.when(safe)
    def _():
        lax.fori_loop(0, ts // CHUNK, fast_chunk, 0, unroll=2 * unroll)

    @pl.when(jnp.logical_not(safe))
    def _():
        lax.fori_loop(0, ts // CHUNK, exact_chunk, 0, unroll=unroll)


def _recurrence(q, k, la, v, sg, gain, *, dk, dv, nh, seq, ts=1024, unroll=2):
    t = q.shape[0]
    heads = q.shape[1] // dk
    ts = min(ts, seq)
    dup = lambda m: jnp.asarray(np.concatenate([m, m], axis=1), BF16)
    mall, mfast = dup(_DECAY_MATS), dup(_FAST_MATS)
    lvl, lvlf = jnp.asarray(_LEVEL_MAT), jnp.asarray(_FAST_LEVEL_MAT)
    kspec = pl.BlockSpec((ts, nh * dk), lambda hg, s: (s, hg))
    vspec = pl.BlockSpec((ts, nh * dv), lambda hg, s: (s, hg))
    const = lambda arr: pl.BlockSpec(arr.shape, lambda hg, s: (0, 0))
    return pl.pallas_call(
        functools.partial(_rec_kernel, nh=nh, dk=dk, dv=dv, ts=ts, seq=seq, unroll=unroll),
        grid=(heads // nh, t // ts),
        in_specs=[kspec, kspec, kspec, vspec, vspec,
                  pl.BlockSpec((1, nh * dv), lambda hg, s: (0, hg)),
                  const(mall), const(lvl), const(mfast), const(lvlf)],
        out_specs=vspec,
        out_shape=jax.ShapeDtypeStruct((t, heads * dv), BF16),
        scratch_shapes=[pltpu.VMEM((nh, dv, dk), F32)],
        compiler_params=_cparams(("parallel", "arbitrary")),
    )(q, k, la, v, sg, gain.reshape(1, -1), mall, lvl, mfast, lvlf)


def _gelu_tanh(x):
    return jax.nn.gelu(x, approximate=True)


def _gmlp_spatial_kernel(u_ref, v_ref, lng_ref, lnb_ref, ws_ref, bs_ref, o_ref, *, groups):
    v = v_ref[...].astype(F32)
    mu = jnp.mean(v, axis=-1, keepdims=True)
    vc = v - mu
    var = jnp.mean(vc * vc, axis=-1, keepdims=True)
    vn = (vc * lax.rsqrt(var + EPS) * lng_ref[...] + lnb_ref[...]).astype(BF16)
    blk = v.shape[0]
    gw = v.shape[1] // groups
    ri = lax.broadcasted_iota(jnp.int32, (blk, blk), 0)
    ci = lax.broadcasted_iota(jnp.int32, (blk, blk), 1)
    for g in range(groups):
        cols = slice(g * gw, (g + 1) * gw)
        wc = jnp.where(ci <= ri, ws_ref[g], 0.0).astype(BF16)
        mixed = jnp.dot(wc, vn[:, cols], preferred_element_type=F32) + bs_ref[g]
        o_ref[:, cols] = (u_ref[:, cols].astype(F32) * mixed).astype(o_ref.dtype)


def _gmlp_spatial(u, v, ln_g, ln_b, w_s, b_s):
    t, w = u.shape
    groups, blk = w_s.shape[0], w_s.shape[1]
    rows = pl.BlockSpec((blk, w), lambda i: (i, 0))
    return pl.pallas_call(
        functools.partial(_gmlp_spatial_kernel, groups=groups),
        grid=(t // blk,),
        in_specs=[rows, rows,
                  pl.BlockSpec((1, w), lambda i: (0, 0)),
                  pl.BlockSpec((1, w), lambda i: (0, 0)),
                  pl.BlockSpec((groups, blk, blk), lambda i: (0, 0, 0)),
                  pl.BlockSpec((groups, blk, 1), lambda i: (0, 0, 0))],
        out_specs=rows,
        out_shape=jax.ShapeDtypeStruct((t, w), BF16),
        compiler_params=_cparams(("parallel",)),
    )(u, v, ln_g.reshape(1, w), ln_b.reshape(1, w), w_s, b_s.reshape(groups, blk, 1))


def _epi_residual(accs, extras):
    return [extras[0] + accs[0]]


def _epi_hgrn(accs, extras):
    q, f, i, g = accs
    lb = extras[0]
    e = jnp.exp(-jnp.abs(f))
    log_sig = jnp.minimum(f, 0.0) - jnp.log1p(e)
    a = jnp.log(lb)
    b = jnp.log1p(-lb) + log_sig
    log_forget = jnp.maximum(a, b) + jnp.log1p(jnp.exp(-jnp.abs(a - b)))
    key = (1.0 - lb) * (jnp.where(f >= 0.0, e, 1.0) / (1.0 + e))
    return [_silu(q), key, log_forget, i, _silu(g)]


def _epi_gelu2(accs, extras):
    return [_gelu_tanh(accs[0]), _gelu_tanh(accs[1])]


def _epi_scale(scale):
    return lambda accs, extras: [accs[0] * scale]


def _epi_identity(accs, extras):
    return [accs[0]]


def _epi_silu(accs, extras):
    return [_silu(accs[0])]


def _epi_gla_gate(accs, extras):
    z = accs[0] + extras[0]
    log_sig = jnp.minimum(z, 0.0) - jnp.log1p(jnp.exp(-jnp.abs(z)))
    return [log_sig / GLA_GATE_TAU]


def _epi_first_lanes(width):
    def epi(accs, extras):
        lane = lax.broadcasted_iota(jnp.int32, accs[0].shape, 1)
        return [jnp.where(lane < width, accs[0], 0.0)]
    return epi


def _epi_ple(accs, extras):
    h, y, p, proj = extras
    emb = jnp.dot(p.astype(BF16), proj.astype(BF16), preferred_element_type=F32)
    return [(h + y) + emb * _sigmoid(accs[0])]


def kernel(x, p, norm_mix, norm_ffn, norm_final, conv_w_in, conv_w, conv_w_out, hgrn_w_in, hgrn_lb, hgrn_norm, hgrn_w_out, gla_w_in, gla_w_a2, gla_b_a, gla_norm, gla_w_out, gmlp_w_in, gmlp_ln_g, gmlp_ln_b, gmlp_w_s, gmlp_b_s, gmlp_w_out, moe_w_group, moe_w_expert, moe_w_gate, moe_w_up, moe_w_down, ple_w_proj, ple_w_gate):
    batch, seq, d = x.shape
    depth = p.shape[0]
    t = batch * seq
    ff = moe_w_gate.shape[-1]
    n_exp = moe_w_gate.shape[1]

    lb_w = jax.nn.softmax(hgrn_lb.astype(F32), axis=0)
    lower_bounds = jnp.cumsum(lb_w, axis=0) - lb_w[0]
    w_gate_all = moe_w_gate.reshape(depth * n_exp, d, ff)
    w_up_all = moe_w_up.reshape(depth * n_exp, d, ff)
    w_down_all = moe_w_down.reshape(depth * n_exp, ff, d)
    p_all = p.reshape(depth, t, p.shape[-1])

    h = x.reshape(t, d)
    for layer in range(depth):
        kind, j = layer % 4, layer // 4
        a = _rmsnorm(h, norm_mix[layer], BF16)
        if kind == 0:
            gated = _conv_in(a, conv_w_in, j, conv_w[j], seq)
            w_out = conv_w_out
        elif kind == 1:
            nb = d // 256
            qt, key, log_f, val, sg = _mm(
                a, [(hgrn_w_in, j, o * nb) for o in range(4)],
                [(lower_bounds.reshape(depth, 1, d), 'row', layer)],
                [BF16, BF16, F32, BF16, BF16], _epi_hgrn, n_cols=d, tm=1024, tn=256, rb=128)
            gated = _recurrence(qt, key, log_f, val, sg, hgrn_norm[j],
                                dk=HGRN_DK, dv=HGRN_DK, nh=8, seq=seq)
            w_out = hgrn_w_out
        elif kind == 2:
            kd = d // 2
            dk = kd // GLA_HEADS
            dv = d // GLA_HEADS
            tn = 1024
            (q,) = _mm(a, [(gla_w_in, j, 0)], [], [BF16], _epi_scale(dk ** -0.5),
                       n_cols=kd, tm=512, tn=tn)
            (k,) = _mm(a, [(gla_w_in, j, kd // tn)], [], [BF16], _epi_identity,
                       n_cols=kd, tm=512, tn=tn)
            (v,) = _mm(a, [(gla_w_in, j, 2 * kd // tn)], [], [BF16], _epi_identity,
                       n_cols=d, tm=512, tn=tn)
            (sg,) = _mm(a, [(gla_w_in, j, (2 * kd + d) // tn)], [], [BF16], _epi_silu,
                        n_cols=d, tm=512, tn=tn)
            (a_low,) = _mm(a, [(gla_w_in, j, (2 * kd + 2 * d) // LANES)], [], [BF16],
                           _epi_first_lanes(GLA_GATE_RANK), n_cols=LANES, tm=512, tn=LANES)
            w_a2 = jnp.pad(gla_w_a2, ((0, 0), (0, LANES - GLA_GATE_RANK), (0, 0)))
            (log_a,) = _mm(a_low, [(w_a2, j, 0)], [(gla_b_a.reshape(-1, 1, kd), 'row', j)], [F32],
                           _epi_gla_gate, n_cols=kd, tm=1024, tn=tn)
            gated = _recurrence(q, k, log_a, v, sg, gla_norm[j], dk=dk, dv=dv, nh=2, seq=seq)
            w_out = gla_w_out
        else:
            wd = gmlp_w_in.shape[-1] // 2
            u, v = _mm(a, [(gmlp_w_in, j, 0), (gmlp_w_in, j, wd // 512)], [], [BF16, BF16], _epi_gelu2,
                       n_cols=wd, tm=1024, tn=512)
            gated = _gmlp_spatial(u, v, gmlp_ln_g[j], gmlp_ln_b[j], gmlp_w_s[j], gmlp_b_s[j])
            w_out = gmlp_w_out
        (h,) = _mm(gated, [(w_out, j, 0)], [(h, 'tile', None)], [F32], _epi_residual,
                   n_cols=d, tm=512, tn=512 if gated.shape[1] > d else 1024)

        w_router = jnp.pad(jnp.concatenate([moe_w_group[layer], moe_w_expert[layer]], axis=1),
                           ((0, 0), (0, LANES - MOE_GROUPS - MOE_GROUPS * MOE_EPG)))
        c, route, seg_counts = _rms_router(h, norm_ffn[layer], w_router)
        y = _moe_layer(h, route, seg_counts, norm_ffn[layer], w_gate_all, w_up_all, w_down_all, layer * n_exp)
        (h,) = _mm(c, [(ple_w_gate, layer, 0)],
                   [(h, 'tile', None), (y, 'tile', None), (p_all, 'rowtile', layer),
                    (ple_w_proj, 'kcol', layer)],
                   [F32], _epi_ple, n_cols=d, tm=512, tn=1024)
    out = _rmsnorm(h, norm_final, F32)
    return out.reshape(batch, seq, d)
```

```python
import functools

import numpy as np
import jax
import jax.numpy as jnp
from jax import lax
from jax.experimental import pallas as pl
from jax.experimental.pallas import tpu as pltpu

F32 = jnp.float32
BF16 = jnp.bfloat16

EPS = 1e-6
CHUNK = 128
N_LEVELS = 7
LOG2_E = 1.4426950408889634
LANES = 128
VMEM_LIMIT = 56 * 1024 * 1024

CONV_WIDTH = 3
HGRN_DK = 128
GLA_HEADS = 8
GLA_GATE_RANK = 16
GLA_GATE_TAU = 16.0
GMLP_BLOCK = 128
GMLP_GROUPS = 8
MOE_GROUPS = 4
MOE_EPG = 4
MOE_TOP_K = 2
MOE_PAIRS = 6
MOE_PAIR_A = (0, 0, 0, 1, 1, 2)
MOE_PAIR_B = (1, 2, 3, 2, 3, 3)
MOE_TILE = 256
MOE_SCATTER_SLOTS = 3


def _cparams(sem):
    return pltpu.CompilerParams(dimension_semantics=sem, vmem_limit_bytes=VMEM_LIMIT)


def _sigmoid(x):
    return jax.nn.sigmoid(x)


def _silu(x):
    return x * jax.nn.sigmoid(x)


def _mm_kernel(*refs, n_w, n_e, epilogue, kinds, rb):
    a_ref = refs[0]
    w_refs = refs[1:1 + n_w]
    e_refs = refs[1 + n_w:1 + n_w + n_e]
    o_refs = refs[1 + n_w + n_e:-1]
    wbf_ref = refs[-1]

    @pl.when(pl.program_id(1) == 0)
    def _():
        for j, w in enumerate(w_refs):
            wbf_ref[j] = w[...].astype(BF16)

    tm = a_ref.shape[0]
    for r in range(0, tm, rb):
        rows = slice(r, r + rb)
        a = a_ref[rows, :]
        accs = [jnp.dot(a, wbf_ref[j], preferred_element_type=F32) for j in range(n_w)]
        ex = [e[rows, :] if kind in ('tile', 'rowtile') else e[...] for e, kind in zip(e_refs, kinds)]
        res = epilogue(accs, ex)
        for o, val in zip(o_refs, res):
            o[rows, :] = val.astype(o.dtype)


def _mm(a, ws, extras, out_dtypes, epilogue, *, n_cols, tm, tn, rb=256):
    m, k = a.shape
    grid = (n_cols // tn, m // tm)
    in_specs = [pl.BlockSpec((tm, k), lambda n, i: (i, 0))]
    args = [a]
    for w, lead, off in ws:
        in_specs.append(pl.BlockSpec((None, k, tn), lambda n, i, lead=lead, off=off: (lead, 0, n + off),
                                     pipeline_mode=pl.Buffered(1)))
        args.append(w)
    for arr, kind, lead in extras:
        if kind == 'row':
            in_specs.append(pl.BlockSpec((None, 1, tn), lambda n, i, lead=lead: (lead, 0, n)))
        elif kind == 'tile':
            assert lead is None
            in_specs.append(pl.BlockSpec((tm, tn), lambda n, i: (i, n)))
        elif kind == 'rowtile':
            in_specs.append(pl.BlockSpec((None, tm, arr.shape[2]), lambda n, i, lead=lead: (lead, i, 0)))
        elif kind == 'kcol':
            in_specs.append(pl.BlockSpec((None, arr.shape[1], tn), lambda n, i, lead=lead: (lead, 0, n)))
        else:
            raise ValueError(kind)
        args.append(arr)
    out_shape = [jax.ShapeDtypeStruct((m, n_cols), dt) for dt in out_dtypes]
    out_specs = [pl.BlockSpec((tm, tn), lambda n, i: (i, n)) for _ in out_dtypes]
    res = pl.pallas_call(
        functools.partial(_mm_kernel, n_w=len(ws), n_e=len(extras), epilogue=epilogue,
                          kinds=tuple(kind for _, kind, _ in extras), rb=min(rb, tm)),
        grid=grid, in_specs=in_specs, out_specs=out_specs, out_shape=out_shape,
        scratch_shapes=[pltpu.VMEM((len(ws), k, tn), BF16)],
        compiler_params=_cparams(("arbitrary", "arbitrary")),
    )(*args)
    return res


def _rms_rows(x, gain):
    ms = jnp.mean(x * x, axis=-1, keepdims=True)
    return x * lax.rsqrt(ms + EPS) * gain


def _rms_kernel(h_ref, g_ref, o_ref):
    o_ref[...] = _rms_rows(h_ref[...], g_ref[...]).astype(o_ref.dtype)


def _rmsnorm(h, gain, out_dtype, tm=256):
    t, d = h.shape
    return pl.pallas_call(
        _rms_kernel, grid=(t // tm,),
        in_specs=[pl.BlockSpec((tm, d), lambda i: (i, 0)), pl.BlockSpec((1, d), lambda i: (0, 0))],
        out_specs=pl.BlockSpec((tm, d), lambda i: (i, 0)),
        out_shape=jax.ShapeDtypeStruct((t, d), out_dtype),
        compiler_params=_cparams(("parallel",)),
    )(h, gain.reshape(1, d))


def _first_max_index(vals, m):
    idx = jnp.full(m.shape, len(vals) - 1, jnp.int32)
    for j in range(len(vals) - 2, -1, -1):
        idx = jnp.where(vals[j] == m, j, idx)
    return idx


def _route_cols(logits):
    col = lambda j: logits[:, j:j + 1]
    gl = [col(j) for j in range(MOE_GROUPS)]
    gm = functools.reduce(jnp.maximum, gl)
    gsum = functools.reduce(lambda a, b: a + b, [jnp.exp(g - gm) for g in gl])
    group_p = 1.0 / gsum
    gidx = _first_max_index(gl, gm)
    ig = []
    for e in range(MOE_EPG):
        v = col(MOE_GROUPS + (MOE_GROUPS - 1) * MOE_EPG + e)
        for g in range(MOE_GROUPS - 2, -1, -1):
            v = jnp.where(gidx == g, col(MOE_GROUPS + g * MOE_EPG + e), v)
        ig.append(v)
    m1 = functools.reduce(jnp.maximum, ig)
    i1 = _first_max_index(ig, m1)
    ig2 = [jnp.where(i1 == e, -jnp.inf, ig[e]) for e in range(MOE_EPG)]
    m2 = functools.reduce(jnp.maximum, ig2)
    i2 = _first_max_index(ig2, m2)
    t = jnp.exp(m2 - m1)
    w1 = 1.0 / (1.0 + t) * group_p
    w2 = t / (1.0 + t) * group_p
    first = i1 < i2
    ea = jnp.where(first, i1, i2)
    eb = jnp.where(first, i2, i1)
    wa = jnp.where(first, w1, w2)
    wb = jnp.where(first, w2, w1)
    pair = jnp.where(ea == 0, 0, jnp.where(ea == 1, 3, 5)) + (eb - ea - 1)
    seg = gidx * MOE_PAIRS + pair
    lane = lax.broadcasted_iota(jnp.int32, logits.shape, 1)
    return jnp.where(lane == 0, wa, jnp.where(lane == 1, wb, jnp.where(lane == 2, seg.astype(F32), 0.0)))


def _rms_router_kernel(h_ref, g_ref, whi_ref, wlo_ref, c_ref, route_ref, counts_ref, carry_ref):
    @pl.when(pl.program_id(0) == 0)
    def _():
        carry_ref[...] = jnp.zeros_like(carry_ref)

    y = _rms_rows(h_ref[...], g_ref[...])
    c = y.astype(BF16)
    c_ref[...] = c
    y_lo = (y - c.astype(F32)).astype(BF16)
    whi = whi_ref[...]
    logits = (jnp.dot(c, whi, preferred_element_type=F32)
              + (jnp.dot(c, wlo_ref[...], preferred_element_type=F32)
                 + jnp.dot(y_lo, whi, preferred_element_type=F32)))
    route = _route_cols(logits)

    tm = route.shape[0]
    lane = lax.broadcasted_iota(jnp.int32, route.shape, 1)
    onehot = (lane == route[:, 2:3].astype(jnp.int32)).astype(F32)
    ri = lax.broadcasted_iota(jnp.int32, (tm, tm), 0)
    ci = lax.broadcasted_iota(jnp.int32, (tm, tm), 1)
    earlier = jnp.where(ci < ri, 1.0, 0.0).astype(BF16)
    before = jnp.dot(earlier, onehot.astype(BF16), preferred_element_type=F32) + carry_ref[0:1]
    rank = jnp.sum(onehot * before, axis=1, keepdims=True)
    route_ref[...] = jnp.where(lane == 3, rank, route)
    carry_ref[...] = carry_ref[...] + jnp.sum(onehot, axis=0, keepdims=True)
    counts_ref[...] = carry_ref[...]


def _rms_router(h, gain, w_router, tm=512):
    t, d = h.shape
    w_hi = w_router.astype(BF16)
    w_lo = (w_router - w_hi.astype(F32)).astype(BF16)
    return pl.pallas_call(
        _rms_router_kernel, grid=(t // tm,),
        in_specs=[pl.BlockSpec((tm, d), lambda i: (i, 0)),
                  pl.BlockSpec((1, d), lambda i: (0, 0)),
                  pl.BlockSpec((d, LANES), lambda i: (0, 0)),
                  pl.BlockSpec((d, LANES), lambda i: (0, 0))],
        out_specs=[pl.BlockSpec((tm, d), lambda i: (i, 0)),
                   pl.BlockSpec((tm, LANES), lambda i: (i, 0)),
                   pl.BlockSpec((8, LANES), lambda i: (0, 0))],
        out_shape=[jax.ShapeDtypeStruct((t, d), BF16),
                   jax.ShapeDtypeStruct((t, LANES), F32),
                   jax.ShapeDtypeStruct((8, LANES), F32)],
        scratch_shapes=[pltpu.VMEM((8, LANES), F32)],
        compiler_params=_cparams(("arbitrary",)),
    )(h, gain.reshape(1, d), w_hi, w_lo)


def _gather_kernel(src_ref, nu_ref, h_hbm, gain_ref, x_ref, hbuf0, hbuf1, hsem, *, tm):
    bufs = (hbuf0, hbuf1)
    i = pl.program_id(0)
    nu = nu_ref[0]

    def issue(tile, slot):
        group = 8

        def body(b, carry):
            for j in range(group):
                r = b * group + j
                tok = src_ref[tile * tm + r]
                pltpu.make_async_copy(h_hbm.at[pl.ds(tok, 1), :], bufs[slot].at[pl.ds(r, 1), :],
                                      hsem.at[slot]).start(priority=j % 2)
            return carry
        lax.fori_loop(0, tm // group, body, 0)

    def finish(slot):
        pltpu.make_async_copy(h_hbm.at[pl.ds(0, tm), :], bufs[slot], hsem.at[slot]).wait()
        x_ref[...] = _rms_rows(bufs[slot][...], gain_ref[...]).astype(x_ref.dtype)

    @pl.when(i == 0)
    def _():
        issue(0, 0)

    for slot in range(2):
        @pl.when((i + 1 < nu) & (i % 2 == slot))
        def _(slot=slot):
            issue(i + 1, 1 - slot)
            finish(slot)

        @pl.when((i + 1 == nu) & (i % 2 == slot))
        def _(slot=slot):
            finish(slot)

    @pl.when(i >= nu)
    def _():
        x_ref[...] = jnp.zeros_like(x_ref)


def _gather_sorted(h, gain, src, n_used, nt):
    t, d = h.shape
    tm = MOE_TILE
    return pl.pallas_call(
        functools.partial(_gather_kernel, tm=tm),
        grid_spec=pltpu.PrefetchScalarGridSpec(
            num_scalar_prefetch=2, grid=(nt,),
            in_specs=[pl.BlockSpec(memory_space=pl.ANY),
                      pl.BlockSpec((1, d), lambda i, src, nu: (0, 0))],
            out_specs=pl.BlockSpec((tm, d), lambda i, src, nu: (i, 0)),
            scratch_shapes=[pltpu.VMEM((tm, d), F32), pltpu.VMEM((tm, d), F32),
                            pltpu.SemaphoreType.DMA((2,))],
        ),
        out_shape=jax.ShapeDtypeStruct((nt * tm, d), BF16),
        compiler_params=_cparams(("arbitrary",)),
    )(src, n_used, h, gain.reshape(1, d))


def _moe_up_kernel(ea_ref, eb_ref, nu_ref, x_ref, comb_ref, wg_hbm, wu_hbm, o_ref,
                   wf32_ref, wbf_ref, wsem, run_ref):
    s = pl.program_id(0)
    i = pl.program_id(1)
    nu = nu_ref[0]

    def expert(slot, tile):
        return jnp.where(slot == 0, ea_ref[tile], eb_ref[tile])

    def weight_copies(e, buf):
        return (pltpu.make_async_copy(wg_hbm.at[e], wf32_ref.at[buf, 0], wsem.at[buf]),
                pltpu.make_async_copy(wu_hbm.at[e], wf32_ref.at[buf, 1], wsem.at[buf]))

    @pl.when(i < nu)
    def _():
        cur = expert(s, i)
        first = (s == 0) & (i == 0)
        prev = jnp.where(i > 0, expert(s, jnp.maximum(i - 1, 0)), expert(0, nu - 1))

        @pl.when(first)
        def _():
            run_ref[0] = 0
            for cp in weight_copies(cur, 0):
                cp.start()

        @pl.when(first | (cur != prev))
        def _():
            buf = run_ref[0] % 2
            for cp in weight_copies(cur, buf):
                cp.wait()
            wbf_ref[0] = wf32_ref[buf, 0].astype(BF16)
            wbf_ref[1] = wf32_ref[buf, 1].astype(BF16)

            def expert_at(k):
                slot = (k >= nu).astype(jnp.int32)
                return expert(slot, k - slot * nu)

            end = MOE_TOP_K * nu
            nxt = lax.while_loop(lambda k: (k < end) & (expert_at(jnp.minimum(k, end - 1)) == cur),
                                 lambda k: k + 1, s * nu + i + 1)

            @pl.when(nxt < end)
            def _():
                for cp in weight_copies(expert_at(nxt), 1 - buf):
                    cp.start()

            run_ref[0] = run_ref[0] + 1

        x = x_ref[...]
        g = jnp.dot(x, wbf_ref[0], preferred_element_type=F32)
        u = jnp.dot(x, wbf_ref[1], preferred_element_type=F32)
        comb = comb_ref[...]
        lane = lax.broadcasted_iota(jnp.int32, comb.shape, 1)
        sc = jnp.sum(jnp.where(lane == s, comb, 0.0), axis=1, keepdims=True)
        o_ref[...] = (_silu(g) * u * sc).astype(o_ref.dtype)

    @pl.when(i >= nu_ref[0])
    def _():
        o_ref[...] = jnp.zeros_like(o_ref)


def _moe_down_kernel(ea_ref, eb_ref, nu_ref, dst_ref, hid_ref, wda_ref, wdb_ref, y_hbm, *scratch,
                     tm, nt, t, ff):
    obufs = scratch[:MOE_SCATTER_SLOTS]
    wbf_ref, sem = scratch[MOE_SCATTER_SLOTS:]
    i = pl.program_id(0)
    nu = nu_ref[0]
    n_slots = len(obufs)

    def wait_slot(slot):
        pltpu.make_async_copy(obufs[slot], y_hbm.at[pl.ds(0, tm), :], sem.at[slot]).wait()

    def compute(slot):
        prev = jnp.maximum(i - 1, 0)

        @pl.when((i == 0) | (ea_ref[i] != ea_ref[prev]))
        def _():
            wbf_ref[0] = wda_ref[...].astype(BF16)

        @pl.when((i == 0) | (eb_ref[i] != eb_ref[prev]))
        def _():
            wbf_ref[1] = wdb_ref[...].astype(BF16)

        obufs[slot][...] = (jnp.dot(hid_ref[:, :ff], wbf_ref[0], preferred_element_type=F32)
                            + jnp.dot(hid_ref[:, ff:], wbf_ref[1], preferred_element_type=F32))

    def scatter(tile, slot):
        for r in range(tm):
            pltpu.make_async_copy(obufs[slot].at[pl.ds(r, 1), :],
                                  y_hbm.at[pl.ds(dst_ref[tile * tm + r], 1), :], sem.at[slot]).start(priority=r % 2)

    @pl.when(i == 0)
    def _():
        obufs[0][...] = jnp.zeros(obufs[0].shape, F32)
        for part in range(n_slots):
            zero_copy = pltpu.make_async_copy(obufs[0], y_hbm.at[pl.ds(t + part * tm, tm), :], sem.at[0])
            zero_copy.start()
            zero_copy.wait()
        compute(0)

    for slot in range(n_slots):
        @pl.when((i >= 1) & (i < nu) & (i % n_slots == slot))
        def _(slot=slot):
            @pl.when(i >= n_slots)
            def _():
                wait_slot(slot)

            compute(slot)
            scatter(i - 1, (slot - 1) % n_slots)

        @pl.when((i == nu) & ((nu - 1) % n_slots == slot))
        def _(slot=slot):
            scatter(nu - 1, slot)

    @pl.when(i == nt - 1)
    def _():
        for slot in range(n_slots):
            wait_slot(slot)


def _moe_sorted(x_sorted, comb_sorted, exp_a, exp_b, n_used, dst, t, w_gate, w_up, w_down):
    tp, d = x_sorted.shape
    ff = w_gate.shape[-1]
    tm = MOE_TILE
    nt = tp // tm
    row = lambda i, nu: jnp.minimum(i, nu[0] - 1)
    hidden = pl.pallas_call(
        _moe_up_kernel,
        grid_spec=pltpu.PrefetchScalarGridSpec(
            num_scalar_prefetch=3, grid=(MOE_TOP_K, nt),
            in_specs=[
                pl.BlockSpec((tm, d), lambda s, i, ea, eb, nu: (row(i, nu), 0)),
                pl.BlockSpec((tm, comb_sorted.shape[1]), lambda s, i, ea, eb, nu: (row(i, nu), 0)),
                pl.BlockSpec(memory_space=pl.ANY),
                pl.BlockSpec(memory_space=pl.ANY),
            ],
            out_specs=pl.BlockSpec((tm, ff), lambda s, i, ea, eb, nu: (i, s)),
            scratch_shapes=[pltpu.VMEM((2, 2, d, ff), F32), pltpu.VMEM((2, d, ff), BF16),
                            pltpu.SemaphoreType.DMA((2,)), pltpu.SMEM((1,), jnp.int32)],
        ),
        out_shape=jax.ShapeDtypeStruct((tp, MOE_TOP_K * ff), BF16),
        compiler_params=_cparams(("arbitrary", "arbitrary")),
    )(exp_a, exp_b, n_used, x_sorted, comb_sorted, w_gate, w_up)
    y = pl.pallas_call(
        functools.partial(_moe_down_kernel, tm=tm, nt=nt, t=t, ff=ff),
        grid_spec=pltpu.PrefetchScalarGridSpec(
            num_scalar_prefetch=4, grid=(nt,),
            in_specs=[
                pl.BlockSpec((tm, MOE_TOP_K * ff), lambda i, ea, eb, nu, dst: (row(i, nu), 0)),
                pl.BlockSpec((None, ff, d), lambda i, ea, eb, nu, dst: (ea[i], 0, 0)),
                pl.BlockSpec((None, ff, d), lambda i, ea, eb, nu, dst: (eb[i], 0, 0)),
            ],
            out_specs=pl.BlockSpec(memory_space=pl.ANY),
            scratch_shapes=([pltpu.VMEM((tm, d), F32)] * MOE_SCATTER_SLOTS
                            + [pltpu.VMEM((MOE_TOP_K, ff, d), BF16),
                               pltpu.SemaphoreType.DMA((MOE_SCATTER_SLOTS,))]),
        ),
        out_shape=jax.ShapeDtypeStruct((t + MOE_SCATTER_SLOTS * tm, d), F32),
        compiler_params=_cparams(("arbitrary",)),
    )(exp_a, exp_b, n_used, dst, hidden, w_down, w_down)
    return y


def _moe_layer(h, route, seg_counts, gain, w_gate, w_up, w_down, base):
    t, d = h.shape
    tm = MOE_TILE
    n_seg = MOE_GROUPS * MOE_PAIRS
    assert t % tm == 0 and t // tm >= MOE_SCATTER_SLOTS
    nt = t // tm + n_seg
    seg = route[:, 2].astype(jnp.int32)
    rank = route[:, 3].astype(jnp.int32)
    counts = seg_counts[0, :n_seg].astype(jnp.int32)
    ntiles = (counts + tm - 1) // tm
    tile_end = jnp.cumsum(ntiles)
    tile_start = tile_end - ntiles
    onehot = (seg[:, None] == jnp.arange(n_seg, dtype=jnp.int32)[None, :]).astype(F32)
    seg_start = jnp.dot(onehot, tile_start.astype(F32), precision=lax.Precision.HIGHEST)
    pos = jnp.round(seg_start).astype(jnp.int32) * tm + rank
    n_used = tile_end[-1:].astype(jnp.int32)
    tile_seg = jnp.minimum(
        jnp.sum((jnp.arange(nt, dtype=jnp.int32)[:, None] >= tile_end[None, :]).astype(jnp.int32), axis=1),
        n_seg - 1).astype(jnp.int32)
    tile_first = base + (tile_seg // MOE_PAIRS) * MOE_EPG
    exp_a = tile_first + jnp.asarray(MOE_PAIR_A, jnp.int32)[tile_seg % MOE_PAIRS]
    exp_b = tile_first + jnp.asarray(MOE_PAIR_B, jnp.int32)[tile_seg % MOE_PAIRS]
    token = jnp.full((nt * tm,), -1, jnp.int32).at[pos].set(jnp.arange(t, dtype=jnp.int32))
    r = jnp.arange(nt * tm, dtype=jnp.int32)
    src = jnp.where(token >= 0, token, r % t)
    dst = jnp.where(token >= 0, token, t + (r // tm % MOE_SCATTER_SLOTS) * tm + r % tm)
    x_sorted = _gather_sorted(h, gain, src, n_used, nt)
    comb_sorted = jnp.take(route[:, :8], src, axis=0, mode='clip')
    return _moe_sorted(x_sorted, comb_sorted, exp_a, exp_b, n_used, dst, t, w_gate, w_up, w_down)


def _conv_in_kernel(a_ref, wb_ref, wc_ref, wu_ref, cw_ref, o_ref, carry_ref, wbf_ref, *, tm, seq, rb):
    i = pl.program_id(1)

    @pl.when(i == 0)
    def _():
        wbf_ref[0] = wb_ref[...].astype(BF16)
        wbf_ref[1] = wc_ref[...].astype(BF16)
        wbf_ref[2] = wu_ref[...].astype(BF16)

    @pl.when((i * tm) % seq == 0)
    def _():
        carry_ref[...] = jnp.zeros(carry_ref.shape, F32)

    prev = carry_ref[...]
    cw = cw_ref[...]
    for r in range(0, tm, rb):
        rows = slice(r, r + rb)
        a = a_ref[rows, :]
        gate_b = jnp.dot(a, wbf_ref[0], preferred_element_type=F32)
        gate_c = jnp.dot(a, wbf_ref[1], preferred_element_type=F32)
        u = jnp.dot(a, wbf_ref[2], preferred_element_type=F32)
        z = gate_c * u
        rid = lax.broadcasted_iota(jnp.int32, z.shape, 0)
        z1 = jnp.where(rid == 0, prev[7:8], pltpu.roll(z, 1, 0))
        z2 = jnp.where(rid == 0, prev[6:7], jnp.where(rid == 1, prev[7:8], pltpu.roll(z, 2, 0)))
        conv = cw[0:1] * z2 + cw[1:2] * z1 + cw[2:3] * z
        o_ref[rows, :] = (gate_b * conv).astype(o_ref.dtype)
        prev = z[rb - 8:rb]
    carry_ref[...] = prev


def _conv_in(a, w_in, j, conv_w, seq, tm=1024, tn=256, rb=256):
    t, d = a.shape
    nb = d // tn
    return pl.pallas_call(
        functools.partial(_conv_in_kernel, tm=tm, seq=seq, rb=min(rb, tm)),
        grid=(nb, t // tm),
        in_specs=[pl.BlockSpec((tm, d), lambda n, i: (i, 0)),
                  pl.BlockSpec((None, d, tn), lambda n, i: (j, 0, n)),
                  pl.BlockSpec((None, d, tn), lambda n, i: (j, 0, n + nb)),
                  pl.BlockSpec((None, d, tn), lambda n, i: (j, 0, n + 2 * nb)),
                  pl.BlockSpec((CONV_WIDTH, tn), lambda n, i: (0, n))],
        out_specs=pl.BlockSpec((tm, tn), lambda n, i: (i, n)),
        out_shape=jax.ShapeDtypeStruct((t, d), BF16),
        scratch_shapes=[pltpu.VMEM((8, tn), F32), pltpu.VMEM((3, d, tn), BF16)],
        compiler_params=_cparams(("arbitrary", "arbitrary")),
    )(a, w_in, w_in, w_in, conv_w)


def _build_decay_mats():
    t = np.arange(CHUNK)
    i = t[:, None]
    tt = t[None, :]
    mats = [tt <= i, tt > i]
    for l in range(N_LEVELS):
        s = 1 << l
        ref = ((t // (2 * s)) * (2 * s) + s - 1)[:, None]
        upper = ((t % (2 * s)) >= s)[:, None]
        mats.append(np.where(upper, (tt > ref) & (tt <= i), (tt > i) & (tt <= ref)))
    return np.concatenate(mats, axis=0).astype(np.float32)


def _build_level_matrix():
    lv = np.full((CHUNK, CHUNK), -1, np.int32)
    for i in range(CHUNK):
        lv[i, i] = N_LEVELS
        for j in range(i):
            lv[i, j] = int(np.floor(np.log2(i ^ j)))
    return lv


_DECAY_MATS = _build_decay_mats()
_LEVEL_MAT = _build_level_matrix()

FAST_BLOCK = CHUNK // 2
FAST_LIMIT = 96.0


def _build_fast_mats():
    t = np.arange(CHUNK)
    i = t[:, None]
    tt = t[None, :]
    ref = ((t // FAST_BLOCK) * FAST_BLOCK + FAST_BLOCK // 2 - 1)[:, None]
    mid = ((tt > ref) & (tt <= i)).astype(np.float32) - ((tt > i) & (tt <= ref)).astype(np.float32)
    top = _DECAY_MATS[(1 + N_LEVELS) * CHUNK:(2 + N_LEVELS) * CHUNK]
    return np.concatenate([_DECAY_MATS[:2 * CHUNK], top, mid], axis=0).astype(np.float32)


def _build_fast_level_matrix():
    i = np.arange(CHUNK)[:, None]
    j = np.arange(CHUNK)[None, :]
    same = (i // FAST_BLOCK) == (j // FAST_BLOCK)
    return np.where(same & (i >= j), 0, np.where((i // FAST_BLOCK) > (j // FAST_BLOCK), 1, -1)).astype(np.int32)


_FAST_MATS = _build_fast_mats()
_FAST_LEVEL_MAT = _build_fast_level_matrix()

_NT = (((1,), (1,)), ((), ()))
_TN = (((0,), (0,)), ((), ()))


def _rec_kernel(q_ref, k_ref, la_ref, v_ref, sg_ref, gain_ref, mall_ref, lvl_ref, mfast_ref, lvlf_ref,
                o_ref, st_ref, *, nh, dk, dv, ts, seq, unroll):
    t = pl.program_id(1)

    @pl.when((t * ts) % seq == 0)
    def _():
        st_ref[...] = jnp.zeros_like(st_ref)

    gain = gain_ref[...]

    def decays(mats, rows):
        la = la_ref[rows, :] * LOG2_E
        la_hi = la.astype(BF16)
        la_lo = (la - la_hi.astype(F32)).astype(BF16)
        return jnp.dot(mats, jnp.concatenate([la_hi, la_lo], axis=0), preferred_element_type=F32)

    def finish(h, rows, q, k, g, suffix, scores):
        vc = slice(h * dv, (h + 1) * dv)
        v = v_ref[rows, vc]
        state_t = st_ref[h]
        o = (jnp.dot(scores.astype(BF16), v, preferred_element_type=F32)
             + lax.dot_general((q * jnp.exp2(g)).astype(BF16), state_t.astype(BF16), _NT,
                               preferred_element_type=F32))
        k_end = (k * jnp.exp2(suffix)).astype(BF16)
        st_ref[h] = (state_t * jnp.exp2(g[CHUNK - 1:CHUNK])
                     + lax.dot_general(v, k_end, _TN, preferred_element_type=F32))
        on = _rms_rows(o, gain[:, vc]) * sg_ref[rows, vc].astype(F32)
        o_ref[rows, vc] = on.astype(o_ref.dtype)

    def fast_chunk(c, carry):
        rows = pl.ds(pl.multiple_of(c * CHUNK, CHUNK), CHUNK)
        lvl = lvlf_ref[...]
        dall = decays(mfast_ref[...], rows)
        for h in range(nh):
            kc = slice(h * dk, (h + 1) * dk)
            q = q_ref[rows, kc].astype(F32)
            k = k_ref[rows, kc].astype(F32)
            e_top = jnp.exp2(dall[2 * CHUNK:3 * CHUNK, kc])
            d_mid = dall[3 * CHUNK:4 * CHUNK, kc]
            r_top = lax.dot_general((q * e_top).astype(BF16), (k * e_top).astype(BF16), _NT,
                                    preferred_element_type=F32)
            r_mid = lax.dot_general((q * jnp.exp2(d_mid)).astype(BF16), (k * jnp.exp2(-d_mid)).astype(BF16),
                                    _NT, preferred_element_type=F32)
            scores = jnp.where(lvl == 1, r_top, jnp.where(lvl == 0, r_mid, 0.0))
            finish(h, rows, q, k, dall[0:CHUNK, kc], dall[CHUNK:2 * CHUNK, kc], scores)
        return carry

    def exact_chunk(c, carry):
        rows = pl.ds(pl.multiple_of(c * CHUNK, CHUNK), CHUNK)
        lvl = lvl_ref[...]
        dall = decays(mall_ref[...], rows)
        for h in range(nh):
            kc = slice(h * dk, (h + 1) * dk)
            q = q_ref[rows, kc].astype(F32)
            k = k_ref[rows, kc].astype(F32)
            scores = jnp.where(
                lvl == N_LEVELS,
                lax.dot_general(q.astype(BF16), k.astype(BF16), _NT, preferred_element_type=F32), 0.0)
            for l in range(N_LEVELS):
                e = jnp.exp2(dall[(2 + l) * CHUNK:(3 + l) * CHUNK, kc])
                r = lax.dot_general((q * e).astype(BF16), (k * e).astype(BF16), _NT,
                                    preferred_element_type=F32)
                scores = jnp.where(lvl == l, r, scores)
            finish(h, rows, q, k, dall[0:CHUNK, kc], dall[CHUNK:2 * CHUNK, kc], scores)
        return carry

    half = FAST_BLOCK // 2
    mass = jnp.sum(jnp.abs(la_ref[...]).reshape(ts // half, half, nh * dk), axis=1)
    safe = jnp.max(mass) * LOG2_E < FAST_LIMIT

    @pl.when(safe)
    def _():
        lax.fori_loop(0, ts // CHUNK, fast_chunk, 0, unroll=2 * unroll)

    @pl.when(jnp.logical_not(safe))
    def _():
        lax.fori_loop(0, ts // CHUNK, exact_chunk, 0, unroll=unroll)


def _recurrence(q, k, la, v, sg, gain, *, dk, dv, nh, seq, ts=1024, unroll=2):
    t = q.shape[0]
    heads = q.shape[1] // dk
    ts = min(ts, seq)
    dup = lambda m: jnp.asarray(np.concatenate([m, m], axis=1), BF16)
    mall, mfast = dup(_DECAY_MATS), dup(_FAST_MATS)
    lvl, lvlf = jnp.asarray(_LEVEL_MAT), jnp.asarray(_FAST_LEVEL_MAT)
    kspec = pl.BlockSpec((ts, nh * dk), lambda hg, s: (s, hg))
    vspec = pl.BlockSpec((ts, nh * dv), lambda hg, s: (s, hg))
    const = lambda arr: pl.BlockSpec(arr.shape, lambda hg, s: (0, 0))
    return pl.pallas_call(
        functools.partial(_rec_kernel, nh=nh, dk=dk, dv=dv, ts=ts, seq=seq, unroll=unroll),
        grid=(heads // nh, t // ts),
        in_specs=[kspec, kspec, kspec, vspec, vspec,
                  pl.BlockSpec((1, nh * dv), lambda hg, s: (0, hg)),
                  const(mall), const(lvl), const(mfast), const(lvlf)],
        out_specs=vspec,
        out_shape=jax.ShapeDtypeStruct((t, heads * dv), BF16),
        scratch_shapes=[pltpu.VMEM((nh, dv, dk), F32)],
        compiler_params=_cparams(("parallel", "arbitrary")),
    )(q, k, la, v, sg, gain.reshape(1, -1), mall, lvl, mfast, lvlf)


def _gelu_tanh(x):
    return jax.nn.gelu(x, approximate=True)


def _gmlp_spatial_kernel(u_ref, v_ref, lng_ref, lnb_ref, ws_ref, bs_ref, o_ref, *, groups):
    v = v_ref[...].astype(F32)
    mu = jnp.mean(v, axis=-1, keepdims=True)
    vc = v - mu
    var = jnp.mean(vc * vc, axis=-1, keepdims=True)
    vn = (vc * lax.rsqrt(var + EPS) * lng_ref[...] + lnb_ref[...]).astype(BF16)
    blk = v.shape[0]
    gw = v.shape[1] // groups
    ri = lax.broadcasted_iota(jnp.int32, (blk, blk), 0)
    ci = lax.broadcasted_iota(jnp.int32, (blk, blk), 1)
    for g in range(groups):
        cols = slice(g * gw, (g + 1) * gw)
        wc = jnp.where(ci <= ri, ws_ref[g], 0.0).astype(BF16)
        mixed = jnp.dot(wc, vn[:, cols], preferred_element_type=F32) + bs_ref[g]
        o_ref[:, cols] = (u_ref[:, cols].astype(F32) * mixed).astype(o_ref.dtype)


def _gmlp_spatial(u, v, ln_g, ln_b, w_s, b_s):
    t, w = u.shape
    groups, blk = w_s.shape[0], w_s.shape[1]
    rows = pl.BlockSpec((blk, w), lambda i: (i, 0))
    return pl.pallas_call(
        functools.partial(_gmlp_spatial_kernel, groups=groups),
        grid=(t // blk,),
        in_specs=[rows, rows,
                  pl.BlockSpec((1, w), lambda i: (0, 0)),
                  pl.BlockSpec((1, w), lambda i: (0, 0)),
                  pl.BlockSpec((groups, blk, blk), lambda i: (0, 0, 0)),
                  pl.BlockSpec((groups, blk, 1), lambda i: (0, 0, 0))],
        out_specs=rows,
        out_shape=jax.ShapeDtypeStruct((t, w), BF16),
        compiler_params=_cparams(("parallel",)),
    )(u, v, ln_g.reshape(1, w), ln_b.reshape(1, w), w_s, b_s.reshape(groups, blk, 1))


def _epi_residual(accs, extras):
    return [extras[0] + accs[0]]


def _epi_hgrn(accs, extras):
    q, f, i, g = accs
    lb = extras[0]
    e = jnp.exp(-jnp.abs(f))
    log_sig = jnp.minimum(f, 0.0) - jnp.log1p(e)
    a = jnp.log(lb)
    b = jnp.log1p(-lb) + log_sig
    log_forget = jnp.maximum(a, b) + jnp.log1p(jnp.exp(-jnp.abs(a - b)))
    key = (1.0 - lb) * (jnp.where(f >= 0.0, e, 1.0) / (1.0 + e))
    return [_silu(q), key, log_forget, i, _silu(g)]


def _epi_gelu2(accs, extras):
    return [_gelu_tanh(accs[0]), _gelu_tanh(accs[1])]


def _epi_scale(scale):
    return lambda accs, extras: [accs[0] * scale]


def _epi_identity(accs, extras):
    return [accs[0]]


def _epi_silu(accs, extras):
    return [_silu(accs[0])]


def _epi_gla_gate(accs, extras):
    z = accs[0] + extras[0]
    log_sig = jnp.minimum(z, 0.0) - jnp.log1p(jnp.exp(-jnp.abs(z)))
    return [log_sig / GLA_GATE_TAU]


def _epi_first_lanes(width):
    def epi(accs, extras):
        lane = lax.broadcasted_iota(jnp.int32, accs[0].shape, 1)
        return [jnp.where(lane < width, accs[0], 0.0)]
    return epi


def _epi_ple(accs, extras):
    h, y, p, proj = extras
    emb = jnp.dot(p.astype(BF16), proj.astype(BF16), preferred_element_type=F32)
    return [(h + y) + emb * _sigmoid(accs[0])]


def kernel(x, p, norm_mix, norm_ffn, norm_final, conv_w_in, conv_w, conv_w_out, hgrn_w_in, hgrn_lb, hgrn_norm, hgrn_w_out, gla_w_in, gla_w_a2, gla_b_a, gla_norm, gla_w_out, gmlp_w_in, gmlp_ln_g, gmlp_ln_b, gmlp_w_s, gmlp_b_s, gmlp_w_out, moe_w_group, moe_w_expert, moe_w_gate, moe_w_up, moe_w_down, ple_w_proj, ple_w_gate):
    batch, seq, d = x.shape
    depth = p.shape[0]
    t = batch * seq
    ff = moe_w_gate.shape[-1]
    n_exp = moe_w_gate.shape[1]

    lb_w = jax.nn.softmax(hgrn_lb.astype(F32), axis=0)
    lower_bounds = jnp.cumsum(lb_w, axis=0) - lb_w[0]
    w_gate_all = moe_w_gate.reshape(depth * n_exp, d, ff)
    w_up_all = moe_w_up.reshape(depth * n_exp, d, ff)
    w_down_all = moe_w_down.reshape(depth * n_exp, ff, d)
    p_all = p.reshape(depth, t, p.shape[-1])

    h = x.reshape(t, d)
    for layer in range(depth):
        kind, j = layer % 4, layer // 4
        a = _rmsnorm(h, norm_mix[layer], BF16)
        if kind == 0:
            gated = _conv_in(a, conv_w_in, j, conv_w[j], seq)
            w_out = conv_w_out
        elif kind == 1:
            nb = d // 256
            qt, key, log_f, val, sg = _mm(
                a, [(hgrn_w_in, j, o * nb) for o in range(4)],
                [(lower_bounds.reshape(depth, 1, d), 'row', layer)],
                [BF16, BF16, F32, BF16, BF16], _epi_hgrn, n_cols=d, tm=1024, tn=256, rb=128)
            gated = _recurrence(qt, key, log_f, val, sg, hgrn_norm[j],
                                dk=HGRN_DK, dv=HGRN_DK, nh=8, seq=seq)
            w_out = hgrn_w_out
        elif kind == 2:
            kd = d // 2
            dk = kd // GLA_HEADS
            dv = d // GLA_HEADS
            tn = 1024
            (q,) = _mm(a, [(gla_w_in, j, 0)], [], [BF16], _epi_scale(dk ** -0.5),
                       n_cols=kd, tm=512, tn=tn)
            (k,) = _mm(a, [(gla_w_in, j, kd // tn)], [], [BF16], _epi_identity,
                       n_cols=kd, tm=512, tn=tn)
            (v,) = _mm(a, [(gla_w_in, j, 2 * kd // tn)], [], [BF16], _epi_identity,
                       n_cols=d, tm=512, tn=tn)
            (sg,) = _mm(a, [(gla_w_in, j, (2 * kd + d) // tn)], [], [BF16], _epi_silu,
                        n_cols=d, tm=512, tn=tn)
            (a_low,) = _mm(a, [(gla_w_in, j, (2 * kd + 2 * d) // LANES)], [], [BF16],
                           _epi_first_lanes(GLA_GATE_RANK), n_cols=LANES, tm=512, tn=LANES)
            w_a2 = jnp.pad(gla_w_a2, ((0, 0), (0, LANES - GLA_GATE_RANK), (0, 0)))
            (log_a,) = _mm(a_low, [(w_a2, j, 0)], [(gla_b_a.reshape(-1, 1, kd), 'row', j)], [F32],
                           _epi_gla_gate, n_cols=kd, tm=1024, tn=tn)
            gated = _recurrence(q, k, log_a, v, sg, gla_norm[j], dk=dk, dv=dv, nh=4, seq=seq)
            w_out = gla_w_out
        else:
            wd = gmlp_w_in.shape[-1] // 2
            u, v = _mm(a, [(gmlp_w_in, j, 0), (gmlp_w_in, j, wd // 512)], [], [BF16, BF16], _epi_gelu2,
                       n_cols=wd, tm=1024, tn=512)
            gated = _gmlp_spatial(u, v, gmlp_ln_g[j], gmlp_ln_b[j], gmlp_w_s[j], gmlp_b_s[j])
            w_out = gmlp_w_out
        (h,) = _mm(gated, [(w_out, j, 0)], [(h, 'tile', None)], [F32], _epi_residual,
                   n_cols=d, tm=512, tn=512 if gated.shape[1] > d else 1024)

        w_router = jnp.pad(jnp.concatenate([moe_w_group[layer], moe_w_expert[layer]], axis=1),
                           ((0, 0), (0, LANES - MOE_GROUPS - MOE_GROUPS * MOE_EPG)))
        c, route, seg_counts = _rms_router(h, norm_ffn[layer], w_router)
        y = _moe_layer(h, route, seg_counts, norm_ffn[layer], w_gate_all, w_up_all, w_down_all, layer * n_exp)
        (h,) = _mm(c, [(ple_w_gate, layer, 0)],
                   [(h, 'tile', None), (y, 'tile', None), (p_all, 'rowtile', layer),
                    (ple_w_proj, 'kcol', layer)],
                   [F32], _epi_ple, n_cols=d, tm=512, tn=1024)
    out = _rmsnorm(h, norm_final, F32)
    return out.reshape(batch, seq, d)
```

```python
import functools

import numpy as np
import jax
import jax.numpy as jnp
from jax import lax
from jax.experimental import pallas as pl
from jax.experimental.pallas import tpu as pltpu

F32 = jnp.float32
BF16 = jnp.bfloat16

EPS = 1e-6
CHUNK = 128
N_LEVELS = 7
LOG2_E = 1.4426950408889634
LANES = 128
VMEM_LIMIT = 56 * 1024 * 1024

CONV_WIDTH = 3
HGRN_DK = 128
GLA_HEADS = 8
GLA_GATE_RANK = 16
GLA_GATE_TAU = 16.0
GMLP_BLOCK = 128
GMLP_GROUPS = 8
MOE_GROUPS = 4
MOE_EPG = 4
MOE_TOP_K = 2
MOE_PAIRS = 6
MOE_PAIR_A = (0, 0, 0, 1, 1, 2)
MOE_PAIR_B = (1, 2, 3, 2, 3, 3)
MOE_TILE = 256
MOE_SCATTER_SLOTS = 3


def _cparams(sem):
    return pltpu.CompilerParams(dimension_semantics=sem, vmem_limit_bytes=VMEM_LIMIT)


def _sigmoid(x):
    return jax.nn.sigmoid(x)


def _silu(x):
    return x * jax.nn.sigmoid(x)


def _mm_kernel(*refs, n_w, n_e, epilogue, kinds, rb):
    a_ref = refs[0]
    w_refs = refs[1:1 + n_w]
    e_refs = refs[1 + n_w:1 + n_w + n_e]
    o_refs = refs[1 + n_w + n_e:-1]
    wbf_ref = refs[-1]

    @pl.when(pl.program_id(1) == 0)
    def _():
        for j, w in enumerate(w_refs):
            wbf_ref[j] = w[...].astype(BF16)

    tm = a_ref.shape[0]
    for r in range(0, tm, rb):
        rows = slice(r, r + rb)
        a = a_ref[rows, :]
        accs = [jnp.dot(a, wbf_ref[j], preferred_element_type=F32) for j in range(n_w)]
        ex = [e[rows, :] if kind in ('tile', 'rowtile') else e[...] for e, kind in zip(e_refs, kinds)]
        res = epilogue(accs, ex)
        for o, val in zip(o_refs, res):
            o[rows, :] = val.astype(o.dtype)


def _mm(a, ws, extras, out_dtypes, epilogue, *, n_cols, tm, tn, rb=256):
    m, k = a.shape
    grid = (n_cols // tn, m // tm)
    in_specs = [pl.BlockSpec((tm, k), lambda n, i: (i, 0))]
    args = [a]
    for w, lead, off in ws:
        in_specs.append(pl.BlockSpec((None, k, tn), lambda n, i, lead=lead, off=off: (lead, 0, n + off),
                                     pipeline_mode=pl.Buffered(1)))
        args.append(w)
    for arr, kind, lead in extras:
        if kind == 'row':
            in_specs.append(pl.BlockSpec((None, 1, tn), lambda n, i, lead=lead: (lead, 0, n)))
        elif kind == 'tile':
            assert lead is None
            in_specs.append(pl.BlockSpec((tm, tn), lambda n, i: (i, n)))
        elif kind == 'rowtile':
            in_specs.append(pl.BlockSpec((None, tm, arr.shape[2]), lambda n, i, lead=lead: (lead, i, 0)))
        elif kind == 'kcol':
            in_specs.append(pl.BlockSpec((None, arr.shape[1], tn), lambda n, i, lead=lead: (lead, 0, n)))
        else:
            raise ValueError(kind)
        args.append(arr)
    out_shape = [jax.ShapeDtypeStruct((m, n_cols), dt) for dt in out_dtypes]
    out_specs = [pl.BlockSpec((tm, tn), lambda n, i: (i, n)) for _ in out_dtypes]
    res = pl.pallas_call(
        functools.partial(_mm_kernel, n_w=len(ws), n_e=len(extras), epilogue=epilogue,
                          kinds=tuple(kind for _, kind, _ in extras), rb=min(rb, tm)),
        grid=grid, in_specs=in_specs, out_specs=out_specs, out_shape=out_shape,
        scratch_shapes=[pltpu.VMEM((len(ws), k, tn), BF16)],
        compiler_params=_cparams(("arbitrary", "arbitrary")),
    )(*args)
    return res


def _rms_rows(x, gain):
    ms = jnp.mean(x * x, axis=-1, keepdims=True)
    return x * lax.rsqrt(ms + EPS) * gain


def _rms_kernel(h_ref, g_ref, o_ref):
    o_ref[...] = _rms_rows(h_ref[...], g_ref[...]).astype(o_ref.dtype)


def _rmsnorm(h, gain, out_dtype, tm=256):
    t, d = h.shape
    return pl.pallas_call(
        _rms_kernel, grid=(t // tm,),
        in_specs=[pl.BlockSpec((tm, d), lambda i: (i, 0)), pl.BlockSpec((1, d), lambda i: (0, 0))],
        out_specs=pl.BlockSpec((tm, d), lambda i: (i, 0)),
        out_shape=jax.ShapeDtypeStruct((t, d), out_dtype),
        compiler_params=_cparams(("parallel",)),
    )(h, gain.reshape(1, d))


def _first_max_index(vals, m):
    idx = jnp.full(m.shape, len(vals) - 1, jnp.int32)
    for j in range(len(vals) - 2, -1, -1):
        idx = jnp.where(vals[j] == m, j, idx)
    return idx


def _route_cols(logits):
    col = lambda j: logits[:, j:j + 1]
    gl = [col(j) for j in range(MOE_GROUPS)]
    gm = functools.reduce(jnp.maximum, gl)
    gsum = functools.reduce(lambda a, b: a + b, [jnp.exp(g - gm) for g in gl])
    group_p = 1.0 / gsum
    gidx = _first_max_index(gl, gm)
    ig = []
    for e in range(MOE_EPG):
        v = col(MOE_GROUPS + (MOE_GROUPS - 1) * MOE_EPG + e)
        for g in range(MOE_GROUPS - 2, -1, -1):
            v = jnp.where(gidx == g, col(MOE_GROUPS + g * MOE_EPG + e), v)
        ig.append(v)
    m1 = functools.reduce(jnp.maximum, ig)
    i1 = _first_max_index(ig, m1)
    ig2 = [jnp.where(i1 == e, -jnp.inf, ig[e]) for e in range(MOE_EPG)]
    m2 = functools.reduce(jnp.maximum, ig2)
    i2 = _first_max_index(ig2, m2)
    t = jnp.exp(m2 - m1)
    w1 = 1.0 / (1.0 + t) * group_p
    w2 = t / (1.0 + t) * group_p
    first = i1 < i2
    ea = jnp.where(first, i1, i2)
    eb = jnp.where(first, i2, i1)
    wa = jnp.where(first, w1, w2)
    wb = jnp.where(first, w2, w1)
    pair = jnp.where(ea == 0, 0, jnp.where(ea == 1, 3, 5)) + (eb - ea - 1)
    seg = gidx * MOE_PAIRS + pair
    lane = lax.broadcasted_iota(jnp.int32, logits.shape, 1)
    return jnp.where(lane == 0, wa, jnp.where(lane == 1, wb, jnp.where(lane == 2, seg.astype(F32), 0.0)))


def _rms_router_kernel(h_ref, g_ref, whi_ref, wlo_ref, c_ref, route_ref, counts_ref, carry_ref):
    @pl.when(pl.program_id(0) == 0)
    def _():
        carry_ref[...] = jnp.zeros_like(carry_ref)

    y = _rms_rows(h_ref[...], g_ref[...])
    c = y.astype(BF16)
    c_ref[...] = c
    y_lo = (y - c.astype(F32)).astype(BF16)
    whi = whi_ref[...]
    logits = (jnp.dot(c, whi, preferred_element_type=F32)
              + (jnp.dot(c, wlo_ref[...], preferred_element_type=F32)
                 + jnp.dot(y_lo, whi, preferred_element_type=F32)))
    route = _route_cols(logits)

    tm = route.shape[0]
    lane = lax.broadcasted_iota(jnp.int32, route.shape, 1)
    onehot = (lane == route[:, 2:3].astype(jnp.int32)).astype(F32)
    ri = lax.broadcasted_iota(jnp.int32, (tm, tm), 0)
    ci = lax.broadcasted_iota(jnp.int32, (tm, tm), 1)
    earlier = jnp.where(ci < ri, 1.0, 0.0).astype(BF16)
    before = jnp.dot(earlier, onehot.astype(BF16), preferred_element_type=F32) + carry_ref[0:1]
    rank = jnp.sum(onehot * before, axis=1, keepdims=True)
    route_ref[...] = jnp.where(lane == 3, rank, route)
    carry_ref[...] = carry_ref[...] + jnp.sum(onehot, axis=0, keepdims=True)
    counts_ref[...] = carry_ref[...]


def _rms_router(h, gain, w_router, tm=512):
    t, d = h.shape
    w_hi = w_router.astype(BF16)
    w_lo = (w_router - w_hi.astype(F32)).astype(BF16)
    return pl.pallas_call(
        _rms_router_kernel, grid=(t // tm,),
        in_specs=[pl.BlockSpec((tm, d), lambda i: (i, 0)),
                  pl.BlockSpec((1, d), lambda i: (0, 0)),
                  pl.BlockSpec((d, LANES), lambda i: (0, 0)),
                  pl.BlockSpec((d, LANES), lambda i: (0, 0))],
        out_specs=[pl.BlockSpec((tm, d), lambda i: (i, 0)),
                   pl.BlockSpec((tm, LANES), lambda i: (i, 0)),
                   pl.BlockSpec((8, LANES), lambda i: (0, 0))],
        out_shape=[jax.ShapeDtypeStruct((t, d), BF16),
                   jax.ShapeDtypeStruct((t, LANES), F32),
                   jax.ShapeDtypeStruct((8, LANES), F32)],
        scratch_shapes=[pltpu.VMEM((8, LANES), F32)],
        compiler_params=_cparams(("arbitrary",)),
    )(h, gain.reshape(1, d), w_hi, w_lo)


def _gather_kernel(src_ref, nu_ref, h_hbm, gain_ref, x_ref, hbuf0, hbuf1, hsem, *, tm):
    bufs = (hbuf0, hbuf1)
    i = pl.program_id(0)
    nu = nu_ref[0]

    def issue(tile, slot):
        group = 8

        def body(b, carry):
            for j in range(group):
                r = b * group + j
                tok = src_ref[tile * tm + r]
                pltpu.make_async_copy(h_hbm.at[pl.ds(tok, 1), :], bufs[slot].at[pl.ds(r, 1), :],
                                      hsem.at[slot]).start(priority=j % 2)
            return carry
        lax.fori_loop(0, tm // group, body, 0)

    def finish(slot):
        pltpu.make_async_copy(h_hbm.at[pl.ds(0, tm), :], bufs[slot], hsem.at[slot]).wait()
        x_ref[...] = _rms_rows(bufs[slot][...], gain_ref[...]).astype(x_ref.dtype)

    @pl.when(i == 0)
    def _():
        issue(0, 0)

    for slot in range(2):
        @pl.when((i + 1 < nu) & (i % 2 == slot))
        def _(slot=slot):
            issue(i + 1, 1 - slot)
            finish(slot)

        @pl.when((i + 1 == nu) & (i % 2 == slot))
        def _(slot=slot):
            finish(slot)

    @pl.when(i >= nu)
    def _():
        x_ref[...] = jnp.zeros_like(x_ref)


def _gather_sorted(h, gain, src, n_used, nt):
    t, d = h.shape
    tm = MOE_TILE
    return pl.pallas_call(
        functools.partial(_gather_kernel, tm=tm),
        grid_spec=pltpu.PrefetchScalarGridSpec(
            num_scalar_prefetch=2, grid=(nt,),
            in_specs=[pl.BlockSpec(memory_space=pl.ANY),
                      pl.BlockSpec((1, d), lambda i, src, nu: (0, 0))],
            out_specs=pl.BlockSpec((tm, d), lambda i, src, nu: (i, 0)),
            scratch_shapes=[pltpu.VMEM((tm, d), F32), pltpu.VMEM((tm, d), F32),
                            pltpu.SemaphoreType.DMA((2,))],
        ),
        out_shape=jax.ShapeDtypeStruct((nt * tm, d), BF16),
        compiler_params=_cparams(("arbitrary",)),
    )(src, n_used, h, gain.reshape(1, d))


def _moe_up_kernel(ea_ref, eb_ref, nu_ref, x_ref, comb_ref, wg_hbm, wu_hbm, o_ref,
                   wf32_ref, wbf_ref, wsem, run_ref):
    s = pl.program_id(0)
    i = pl.program_id(1)
    nu = nu_ref[0]

    def expert(slot, tile):
        return jnp.where(slot == 0, ea_ref[tile], eb_ref[tile])

    def weight_copies(e, buf):
        return (pltpu.make_async_copy(wg_hbm.at[e], wf32_ref.at[buf, 0], wsem.at[buf]),
                pltpu.make_async_copy(wu_hbm.at[e], wf32_ref.at[buf, 1], wsem.at[buf]))

    @pl.when(i < nu)
    def _():
        cur = expert(s, i)
        first = (s == 0) & (i == 0)
        prev = jnp.where(i > 0, expert(s, jnp.maximum(i - 1, 0)), expert(0, nu - 1))

        @pl.when(first)
        def _():
            run_ref[0] = 0
            for cp in weight_copies(cur, 0):
                cp.start()

        @pl.when(first | (cur != prev))
        def _():
            buf = run_ref[0] % 2
            for cp in weight_copies(cur, buf):
                cp.wait()
            wbf_ref[0] = wf32_ref[buf, 0].astype(BF16)
            wbf_ref[1] = wf32_ref[buf, 1].astype(BF16)

            def expert_at(k):
                slot = (k >= nu).astype(jnp.int32)
                return expert(slot, k - slot * nu)

            end = MOE_TOP_K * nu
            nxt = lax.while_loop(lambda k: (k < end) & (expert_at(jnp.minimum(k, end - 1)) == cur),
                                 lambda k: k + 1, s * nu + i + 1)

            @pl.when(nxt < end)
            def _():
                for cp in weight_copies(expert_at(nxt), 1 - buf):
                    cp.start()

            run_ref[0] = run_ref[0] + 1

        x = x_ref[...]
        g = jnp.dot(x, wbf_ref[0], preferred_element_type=F32)
        u = jnp.dot(x, wbf_ref[1], preferred_element_type=F32)
        comb = comb_ref[...]
        lane = lax.broadcasted_iota(jnp.int32, comb.shape, 1)
        sc = jnp.sum(jnp.where(lane == s, comb, 0.0), axis=1, keepdims=True)
        o_ref[...] = (_silu(g) * u * sc).astype(o_ref.dtype)

    @pl.when(i >= nu_ref[0])
    def _():
        o_ref[...] = jnp.zeros_like(o_ref)


def _moe_down_kernel(ea_ref, eb_ref, nu_ref, dst_ref, hid_ref, wda_ref, wdb_ref, y_hbm, *scratch,
                     tm, nt, t, ff):
    obufs = scratch[:MOE_SCATTER_SLOTS]
    wbf_ref, sem = scratch[MOE_SCATTER_SLOTS:]
    i = pl.program_id(0)
    nu = nu_ref[0]
    n_slots = len(obufs)

    def wait_slot(slot):
        pltpu.make_async_copy(obufs[slot], y_hbm.at[pl.ds(0, tm), :], sem.at[slot]).wait()

    def compute(slot):
        prev = jnp.maximum(i - 1, 0)

        @pl.when((i == 0) | (ea_ref[i] != ea_ref[prev]))
        def _():
            wbf_ref[0] = wda_ref[...].astype(BF16)

        @pl.when((i == 0) | (eb_ref[i] != eb_ref[prev]))
        def _():
            wbf_ref[1] = wdb_ref[...].astype(BF16)

        obufs[slot][...] = (jnp.dot(hid_ref[:, :ff], wbf_ref[0], preferred_element_type=F32)
                            + jnp.dot(hid_ref[:, ff:], wbf_ref[1], preferred_element_type=F32))

    def scatter(tile, slot):
        for r in range(tm):
            pltpu.make_async_copy(obufs[slot].at[pl.ds(r, 1), :],
                                  y_hbm.at[pl.ds(dst_ref[tile * tm + r], 1), :], sem.at[slot]).start(priority=r % 2)

    @pl.when(i == 0)
    def _():
        obufs[0][...] = jnp.zeros(obufs[0].shape, F32)
        for part in range(n_slots):
            zero_copy = pltpu.make_async_copy(obufs[0], y_hbm.at[pl.ds(t + part * tm, tm), :], sem.at[0])
            zero_copy.start()
            zero_copy.wait()
        compute(0)

    for slot in range(n_slots):
        @pl.when((i >= 1) & (i < nu) & (i % n_slots == slot))
        def _(slot=slot):
            @pl.when(i >= n_slots)
            def _():
                wait_slot(slot)

            compute(slot)
            scatter(i - 1, (slot - 1) % n_slots)

        @pl.when((i == nu) & ((nu - 1) % n_slots == slot))
        def _(slot=slot):
            scatter(nu - 1, slot)

    @pl.when(i == nt - 1)
    def _():
        for slot in range(n_slots):
            wait_slot(slot)


def _moe_sorted(x_sorted, comb_sorted, exp_a, exp_b, n_used, dst, t, w_gate, w_up, w_down):
    tp, d = x_sorted.shape
    ff = w_gate.shape[-1]
    tm = MOE_TILE
    nt = tp // tm
    row = lambda i, nu: jnp.minimum(i, nu[0] - 1)
    hidden = pl.pallas_call(
        _moe_up_kernel,
        grid_spec=pltpu.PrefetchScalarGridSpec(
            num_scalar_prefetch=3, grid=(MOE_TOP_K, nt),
            in_specs=[
                pl.BlockSpec((tm, d), lambda s, i, ea, eb, nu: (row(i, nu), 0)),
                pl.BlockSpec((tm, comb_sorted.shape[1]), lambda s, i, ea, eb, nu: (row(i, nu), 0)),
                pl.BlockSpec(memory_space=pl.ANY),
                pl.BlockSpec(memory_space=pl.ANY),
            ],
            out_specs=pl.BlockSpec((tm, ff), lambda s, i, ea, eb, nu: (i, s)),
            scratch_shapes=[pltpu.VMEM((2, 2, d, ff), F32), pltpu.VMEM((2, d, ff), BF16),
                            pltpu.SemaphoreType.DMA((2,)), pltpu.SMEM((1,), jnp.int32)],
        ),
        out_shape=jax.ShapeDtypeStruct((tp, MOE_TOP_K * ff), BF16),
        compiler_params=_cparams(("arbitrary", "arbitrary")),
    )(exp_a, exp_b, n_used, x_sorted, comb_sorted, w_gate, w_up)
    y = pl.pallas_call(
        functools.partial(_moe_down_kernel, tm=tm, nt=nt, t=t, ff=ff),
        grid_spec=pltpu.PrefetchScalarGridSpec(
            num_scalar_prefetch=4, grid=(nt,),
            in_specs=[
                pl.BlockSpec((tm, MOE_TOP_K * ff), lambda i, ea, eb, nu, dst: (row(i, nu), 0)),
                pl.BlockSpec((None, ff, d), lambda i, ea, eb, nu, dst: (ea[i], 0, 0)),
                pl.BlockSpec((None, ff, d), lambda i, ea, eb, nu, dst: (eb[i], 0, 0)),
            ],
            out_specs=pl.BlockSpec(memory_space=pl.ANY),
            scratch_shapes=([pltpu.VMEM((tm, d), F32)] * MOE_SCATTER_SLOTS
                            + [pltpu.VMEM((MOE_TOP_K, ff, d), BF16),
                               pltpu.SemaphoreType.DMA((MOE_SCATTER_SLOTS,))]),
        ),
        out_shape=jax.ShapeDtypeStruct((t + MOE_SCATTER_SLOTS * tm, d), F32),
        compiler_params=_cparams(("arbitrary",)),
    )(exp_a, exp_b, n_used, dst, hidden, w_down, w_down)
    return y


def _moe_layer(h, route, seg_counts, gain, w_gate, w_up, w_down, base):
    t, d = h.shape
    tm = MOE_TILE
    n_seg = MOE_GROUPS * MOE_PAIRS
    assert t % tm == 0 and t // tm >= MOE_SCATTER_SLOTS
    nt = t // tm + n_seg
    seg = route[:, 2].astype(jnp.int32)
    rank = route[:, 3].astype(jnp.int32)
    counts = seg_counts[0, :n_seg].astype(jnp.int32)
    ntiles = (counts + tm - 1) // tm
    tile_end = jnp.cumsum(ntiles)
    tile_start = tile_end - ntiles
    onehot = (seg[:, None] == jnp.arange(n_seg, dtype=jnp.int32)[None, :]).astype(F32)
    seg_start = jnp.dot(onehot, tile_start.astype(F32), precision=lax.Precision.HIGHEST)
    pos = jnp.round(seg_start).astype(jnp.int32) * tm + rank
    n_used = tile_end[-1:].astype(jnp.int32)
    tile_seg = jnp.minimum(
        jnp.sum((jnp.arange(nt, dtype=jnp.int32)[:, None] >= tile_end[None, :]).astype(jnp.int32), axis=1),
        n_seg - 1).astype(jnp.int32)
    tile_first = base + (tile_seg // MOE_PAIRS) * MOE_EPG
    exp_a = tile_first + jnp.asarray(MOE_PAIR_A, jnp.int32)[tile_seg % MOE_PAIRS]
    exp_b = tile_first + jnp.asarray(MOE_PAIR_B, jnp.int32)[tile_seg % MOE_PAIRS]
    token = jnp.full((nt * tm,), -1, jnp.int32).at[pos].set(jnp.arange(t, dtype=jnp.int32))
    r = jnp.arange(nt * tm, dtype=jnp.int32)
    src = jnp.where(token >= 0, token, r % t)
    dst = jnp.where(token >= 0, token, t + (r // tm % MOE_SCATTER_SLOTS) * tm + r % tm)
    x_sorted = _gather_sorted(h, gain, src, n_used, nt)
    comb_sorted = jnp.take(route[:, :8], src, axis=0, mode='clip')
    return _moe_sorted(x_sorted, comb_sorted, exp_a, exp_b, n_used, dst, t, w_gate, w_up, w_down)


def _conv_in_kernel(a_ref, wb_ref, wc_ref, wu_ref, cw_ref, o_ref, carry_ref, wbf_ref, *, tm, seq, rb):
    i = pl.program_id(1)

    @pl.when(i == 0)
    def _():
        wbf_ref[0] = wb_ref[...].astype(BF16)
        wbf_ref[1] = wc_ref[...].astype(BF16)
        wbf_ref[2] = wu_ref[...].astype(BF16)

    @pl.when((i * tm) % seq == 0)
    def _():
        carry_ref[...] = jnp.zeros(carry_ref.shape, F32)

    prev = carry_ref[...]
    cw = cw_ref[...]
    for r in range(0, tm, rb):
        rows = slice(r, r + rb)
        a = a_ref[rows, :]
        gate_b = jnp.dot(a, wbf_ref[0], preferred_element_type=F32)
        gate_c = jnp.dot(a, wbf_ref[1], preferred_element_type=F32)
        u = jnp.dot(a, wbf_ref[2], preferred_element_type=F32)
        z = gate_c * u
        rid = lax.broadcasted_iota(jnp.int32, z.shape, 0)
        z1 = jnp.where(rid == 0, prev[7:8], pltpu.roll(z, 1, 0))
        z2 = jnp.where(rid == 0, prev[6:7], jnp.where(rid == 1, prev[7:8], pltpu.roll(z, 2, 0)))
        conv = cw[0:1] * z2 + cw[1:2] * z1 + cw[2:3] * z
        o_ref[rows, :] = (gate_b * conv).astype(o_ref.dtype)
        prev = z[rb - 8:rb]
    carry_ref[...] = prev


def _conv_in(a, w_in, j, conv_w, seq, tm=1024, tn=256, rb=256):
    t, d = a.shape
    nb = d // tn
    return pl.pallas_call(
        functools.partial(_conv_in_kernel, tm=tm, seq=seq, rb=min(rb, tm)),
        grid=(nb, t // tm),
        in_specs=[pl.BlockSpec((tm, d), lambda n, i: (i, 0)),
                  pl.BlockSpec((None, d, tn), lambda n, i: (j, 0, n)),
                  pl.BlockSpec((None, d, tn), lambda n, i: (j, 0, n + nb)),
                  pl.BlockSpec((None, d, tn), lambda n, i: (j, 0, n + 2 * nb)),
                  pl.BlockSpec((CONV_WIDTH, tn), lambda n, i: (0, n))],
        out_specs=pl.BlockSpec((tm, tn), lambda n, i: (i, n)),
        out_shape=jax.ShapeDtypeStruct((t, d), BF16),
        scratch_shapes=[pltpu.VMEM((8, tn), F32), pltpu.VMEM((3, d, tn), BF16)],
        compiler_params=_cparams(("arbitrary", "arbitrary")),
    )(a, w_in, w_in, w_in, conv_w)


def _build_decay_mats():
    t = np.arange(CHUNK)
    i = t[:, None]
    tt = t[None, :]
    mats = [tt <= i, tt > i]
    for l in range(N_LEVELS):
        s = 1 << l
        ref = ((t // (2 * s)) * (2 * s) + s - 1)[:, None]
        upper = ((t % (2 * s)) >= s)[:, None]
        mats.append(np.where(upper, (tt > ref) & (tt <= i), (tt > i) & (tt <= ref)))
    return np.concatenate(mats, axis=0).astype(np.float32)


def _build_level_matrix():
    lv = np.full((CHUNK, CHUNK), -1, np.int32)
    for i in range(CHUNK):
        lv[i, i] = N_LEVELS
        for j in range(i):
            lv[i, j] = int(np.floor(np.log2(i ^ j)))
    return lv


_DECAY_MATS = _build_decay_mats()
_LEVEL_MAT = _build_level_matrix()

FAST_BLOCK = CHUNK // 2
FAST_LIMIT = 96.0


def _build_fast_mats():
    t = np.arange(CHUNK)
    i = t[:, None]
    tt = t[None, :]
    ref = ((t // FAST_BLOCK) * FAST_BLOCK + FAST_BLOCK // 2 - 1)[:, None]
    mid = ((tt > ref) & (tt <= i)).astype(np.float32) - ((tt > i) & (tt <= ref)).astype(np.float32)
    top = _DECAY_MATS[(1 + N_LEVELS) * CHUNK:(2 + N_LEVELS) * CHUNK]
    return np.concatenate([_DECAY_MATS[:2 * CHUNK], top, mid], axis=0).astype(np.float32)


def _build_fast_level_matrix():
    i = np.arange(CHUNK)[:, None]
    j = np.arange(CHUNK)[None, :]
    same = (i // FAST_BLOCK) == (j // FAST_BLOCK)
    return np.where(same & (i >= j), 0, np.where((i // FAST_BLOCK) > (j // FAST_BLOCK), 1, -1)).astype(np.int32)


_FAST_MATS = _build_fast_mats()
_FAST_LEVEL_MAT = _build_fast_level_matrix()

_NT = (((1,), (1,)), ((), ()))
_TN = (((0,), (0,)), ((), ()))


def _rec_kernel(q_ref, k_ref, la_ref, v_ref, sg_ref, gain_ref, mall_ref, lvl_ref, mfast_ref, lvlf_ref,
                o_ref, st_ref, *, nh, dk, dv, ts, seq, unroll):
    t = pl.program_id(1)

    @pl.when((t * ts) % seq == 0)
    def _():
        st_ref[...] = jnp.zeros_like(st_ref)

    gain = gain_ref[...]

    def decays(mats, rows):
        la = la_ref[rows, :] * LOG2_E
        la_hi = la.astype(BF16)
        la_lo = (la - la_hi.astype(F32)).astype(BF16)
        return jnp.dot(mats, jnp.concatenate([la_hi, la_lo], axis=0), preferred_element_type=F32)

    def finish(h, rows, q, k, g, suffix, scores):
        vc = slice(h * dv, (h + 1) * dv)
        v = v_ref[rows, vc]
        state_t = st_ref[h]
        o = (jnp.dot(scores.astype(BF16), v, preferred_element_type=F32)
             + lax.dot_general((q * jnp.exp2(g)).astype(BF16), state_t.astype(BF16), _NT,
                               preferred_element_type=F32))
        k_end = (k * jnp.exp2(suffix)).astype(BF16)
        st_ref[h] = (state_t * jnp.exp2(g[CHUNK - 1:CHUNK])
                     + lax.dot_general(v, k_end, _TN, preferred_element_type=F32))
        on = _rms_rows(o, gain[:, vc]) * sg_ref[rows, vc].astype(F32)
        o_ref[rows, vc] = on.astype(o_ref.dtype)

    def fast_chunk(c, carry):
        rows = pl.ds(pl.multiple_of(c * CHUNK, CHUNK), CHUNK)
        lvl = lvlf_ref[...]
        dall = decays(mfast_ref[...], rows)
        for h in range(nh):
            kc = slice(h * dk, (h + 1) * dk)
            q = q_ref[rows, kc].astype(F32)
            k = k_ref[rows, kc].astype(F32)
            e_top = jnp.exp2(dall[2 * CHUNK:3 * CHUNK, kc])
            d_mid = dall[3 * CHUNK:4 * CHUNK, kc]
            r_top = lax.dot_general((q * e_top).astype(BF16), (k * e_top).astype(BF16), _NT,
                                    preferred_element_type=F32)
            r_mid = lax.dot_general((q * jnp.exp2(d_mid)).astype(BF16), (k * jnp.exp2(-d_mid)).astype(BF16),
                                    _NT, preferred_element_type=F32)
            scores = jnp.where(lvl == 1, r_top, jnp.where(lvl == 0, r_mid, 0.0))
            finish(h, rows, q, k, dall[0:CHUNK, kc], dall[CHUNK:2 * CHUNK, kc], scores)
        return carry

    def exact_chunk(c, carry):
        rows = pl.ds(pl.multiple_of(c * CHUNK, CHUNK), CHUNK)
        lvl = lvl_ref[...]
        dall = decays(mall_ref[...], rows)
        for h in range(nh):
            kc = slice(h * dk, (h + 1) * dk)
            q = q_ref[rows, kc].astype(F32)
            k = k_ref[rows, kc].astype(F32)
            scores = jnp.where(
                lvl == N_LEVELS,
                lax.dot_general(q.astype(BF16), k.astype(BF16), _NT, preferred_element_type=F32), 0.0)
            for l in range(N_LEVELS):
                e = jnp.exp2(dall[(2 + l) * CHUNK:(3 + l) * CHUNK, kc])
                r = lax.dot_general((q * e).astype(BF16), (k * e).astype(BF16), _NT,
                                    preferred_element_type=F32)
                scores = jnp.where(lvl == l, r, scores)
            finish(h, rows, q, k, dall[0:CHUNK, kc], dall[CHUNK:2 * CHUNK, kc], scores)
        return carry

    half = FAST_BLOCK // 2
    mass = jnp.sum(jnp.abs(la_ref[...]).reshape(ts // half, half, nh * dk), axis=1)
    safe = jnp.max(mass) * LOG2_E < FAST_LIMIT

    @pl.when(safe)
    def _():
        lax.fori_loop(0, ts // CHUNK, fast_chunk, 0, unroll=2 * unroll)

    @pl.when(jnp.logical_not(safe))
    def _():
        lax.fori_loop(0, ts // CHUNK, exact_chunk, 0, unroll=unroll)


def _recurrence(q, k, la, v, sg, gain, *, dk, dv, nh, seq, ts=1024, unroll=2):
    t = q.shape[0]
    heads = q.shape[1] // dk
    ts = min(ts, seq)
    dup = lambda m: jnp.asarray(np.concatenate([m, m], axis=1), BF16)
    mall, mfast = dup(_DECAY_MATS), dup(_FAST_MATS)
    lvl, lvlf = jnp.asarray(_LEVEL_MAT), jnp.asarray(_FAST_LEVEL_MAT)
    kspec = pl.BlockSpec((ts, nh * dk), lambda hg, s: (s, hg))
    vspec = pl.BlockSpec((ts, nh * dv), lambda hg, s: (s, hg))
    const = lambda arr: pl.BlockSpec(arr.shape, lambda hg, s: (0, 0))
    return pl.pallas_call(
        functools.partial(_rec_kernel, nh=nh, dk=dk, dv=dv, ts=ts, seq=seq, unroll=unroll),
        grid=(heads // nh, t // ts),
        in_specs=[kspec, kspec, kspec, vspec, vspec,
                  pl.BlockSpec((1, nh * dv), lambda hg, s: (0, hg)),
                  const(mall), const(lvl), const(mfast), const(lvlf)],
        out_specs=vspec,
        out_shape=jax.ShapeDtypeStruct((t, heads * dv), BF16),
        scratch_shapes=[pltpu.VMEM((nh, dv, dk), F32)],
        compiler_params=_cparams(("parallel", "arbitrary")),
    )(q, k, la, v, sg, gain.reshape(1, -1), mall, lvl, mfast, lvlf)


def _gelu_tanh(x):
    return jax.nn.gelu(x, approximate=True)


def _gmlp_spatial_kernel(u_ref, v_ref, lng_ref, lnb_ref, ws_ref, bs_ref, o_ref, *, groups):
    v = v_ref[...].astype(F32)
    mu = jnp.mean(v, axis=-1, keepdims=True)
    vc = v - mu
    var = jnp.mean(vc * vc, axis=-1, keepdims=True)
    vn = (vc * lax.rsqrt(var + EPS) * lng_ref[...] + lnb_ref[...]).astype(BF16)
    blk = v.shape[0]
    gw = v.shape[1] // groups
    ri = lax.broadcasted_iota(jnp.int32, (blk, blk), 0)
    ci = lax.broadcasted_iota(jnp.int32, (blk, blk), 1)
    for g in range(groups):
        cols = slice(g * gw, (g + 1) * gw)
        wc = jnp.where(ci <= ri, ws_ref[g], 0.0).astype(BF16)
        mixed = jnp.dot(wc, vn[:, cols], preferred_element_type=F32) + bs_ref[g]
        o_ref[:, cols] = (u_ref[:, cols].astype(F32) * mixed).astype(o_ref.dtype)


def _gmlp_spatial(u, v, ln_g, ln_b, w_s, b_s):
    t, w = u.shape
    groups, blk = w_s.shape[0], w_s.shape[1]
    rows = pl.BlockSpec((blk, w), lambda i: (i, 0))
    return pl.pallas_call(
        functools.partial(_gmlp_spatial_kernel, groups=groups),
        grid=(t // blk,),
        in_specs=[rows, rows,
                  pl.BlockSpec((1, w), lambda i: (0, 0)),
                  pl.BlockSpec((1, w), lambda i: (0, 0)),
                  pl.BlockSpec((groups, blk, blk), lambda i: (0, 0, 0)),
                  pl.BlockSpec((groups, blk, 1), lambda i: (0, 0, 0))],
        out_specs=rows,
        out_shape=jax.ShapeDtypeStruct((t, w), BF16),
        compiler_params=_cparams(("parallel",)),
    )(u, v, ln_g.reshape(1, w), ln_b.reshape(1, w), w_s, b_s.reshape(groups, blk, 1))


def _epi_residual(accs, extras):
    return [extras[0] + accs[0]]


def _epi_hgrn(accs, extras):
    q, f, i, g = accs
    lb = extras[0]
    e = jnp.exp(-jnp.abs(f))
    log_sig = jnp.minimum(f, 0.0) - jnp.log1p(e)
    a = jnp.log(lb)
    b = jnp.log1p(-lb) + log_sig
    log_forget = jnp.maximum(a, b) + jnp.log1p(jnp.exp(-jnp.abs(a - b)))
    key = (1.0 - lb) * (jnp.where(f >= 0.0, e, 1.0) / (1.0 + e))
    return [_silu(q), key, log_forget, i, _silu(g)]


def _epi_gelu2(accs, extras):
    return [_gelu_tanh(accs[0]), _gelu_tanh(accs[1])]


def _epi_scale(scale):
    return lambda accs, extras: [accs[0] * scale]


def _epi_identity(accs, extras):
    return [accs[0]]


def _epi_silu(accs, extras):
    return [_silu(accs[0])]


def _epi_gla_gate(accs, extras):
    z = accs[0] + extras[0]
    log_sig = jnp.minimum(z, 0.0) - jnp.log1p(jnp.exp(-jnp.abs(z)))
    return [log_sig / GLA_GATE_TAU]


def _epi_first_lanes(width):
    def epi(accs, extras):
        lane = lax.broadcasted_iota(jnp.int32, accs[0].shape, 1)
        return [jnp.where(lane < width, accs[0], 0.0)]
    return epi


def _epi_ple(accs, extras):
    h, y, p, proj = extras
    emb = jnp.dot(p.astype(BF16), proj.astype(BF16), preferred_element_type=F32)
    return [(h + y) + emb * _sigmoid(accs[0])]


def kernel(x, p, norm_mix, norm_ffn, norm_final, conv_w_in, conv_w, conv_w_out, hgrn_w_in, hgrn_lb, hgrn_norm, hgrn_w_out, gla_w_in, gla_w_a2, gla_b_a, gla_norm, gla_w_out, gmlp_w_in, gmlp_ln_g, gmlp_ln_b, gmlp_w_s, gmlp_b_s, gmlp_w_out, moe_w_group, moe_w_expert, moe_w_gate, moe_w_up, moe_w_down, ple_w_proj, ple_w_gate):
    batch, seq, d = x.shape
    depth = p.shape[0]
    t = batch * seq
    ff = moe_w_gate.shape[-1]
    n_exp = moe_w_gate.shape[1]

    lb_w = jax.nn.softmax(hgrn_lb.astype(F32), axis=0)
    lower_bounds = jnp.cumsum(lb_w, axis=0) - lb_w[0]
    w_gate_all = moe_w_gate.reshape(depth * n_exp, d, ff)
    w_up_all = moe_w_up.reshape(depth * n_exp, d, ff)
    w_down_all = moe_w_down.reshape(depth * n_exp, ff, d)
    p_all = p.reshape(depth, t, p.shape[-1])

    h = x.reshape(t, d)
    for layer in range(depth):
        kind, j = layer % 4, layer // 4
        a = _rmsnorm(h, norm_mix[layer], BF16)
        if kind == 0:
            gated = _conv_in(a, conv_w_in, j, conv_w[j], seq)
            w_out = conv_w_out
        elif kind == 1:
            nb = d // 256
            qt, key, log_f, val, sg = _mm(
                a, [(hgrn_w_in, j, o * nb) for o in range(4)],
                [(lower_bounds.reshape(depth, 1, d), 'row', layer)],
                [BF16, BF16, F32, BF16, BF16], _epi_hgrn, n_cols=d, tm=1024, tn=256, rb=128)
            gated = _recurrence(qt, key, log_f, val, sg, hgrn_norm[j],
                                dk=HGRN_DK, dv=HGRN_DK, nh=8, seq=seq)
            w_out = hgrn_w_out
        elif kind == 2:
            kd = d // 2
            dk = kd // GLA_HEADS
            dv = d // GLA_HEADS
            tn = 1024
            tp = tn // 2
            q, k = _mm(a, [(gla_w_in, j, 0), (gla_w_in, j, kd // tp)], [], [BF16, BF16],
                       lambda accs, extras: [accs[0] * dk ** -0.5, accs[1]], n_cols=kd, tm=512, tn=tp)
            v, sg = _mm(a, [(gla_w_in, j, 2 * kd // tp), (gla_w_in, j, (2 * kd + d) // tp)], [], [BF16, BF16],
                        lambda accs, extras: [accs[0], _silu(accs[1])], n_cols=d, tm=512, tn=tp)
            (a_low,) = _mm(a, [(gla_w_in, j, (2 * kd + 2 * d) // LANES)], [], [BF16],
                           _epi_first_lanes(GLA_GATE_RANK), n_cols=LANES, tm=512, tn=LANES)
            w_a2 = jnp.pad(gla_w_a2, ((0, 0), (0, LANES - GLA_GATE_RANK), (0, 0)))
            (log_a,) = _mm(a_low, [(w_a2, j, 0)], [(gla_b_a.reshape(-1, 1, kd), 'row', j)], [F32],
                           _epi_gla_gate, n_cols=kd, tm=1024, tn=tn)
            gated = _recurrence(q, k, log_a, v, sg, gla_norm[j], dk=dk, dv=dv, nh=4, seq=seq)
            w_out = gla_w_out
        else:
            wd = gmlp_w_in.shape[-1] // 2
            u, v = _mm(a, [(gmlp_w_in, j, 0), (gmlp_w_in, j, wd // 512)], [], [BF16, BF16], _epi_gelu2,
                       n_cols=wd, tm=1024, tn=512)
            gated = _gmlp_spatial(u, v, gmlp_ln_g[j], gmlp_ln_b[j], gmlp_w_s[j], gmlp_b_s[j])
            w_out = gmlp_w_out
        (h,) = _mm(gated, [(w_out, j, 0)], [(h, 'tile', None)], [F32], _epi_residual,
                   n_cols=d, tm=512, tn=512 if gated.shape[1] > d else 1024)

        w_router = jnp.pad(jnp.concatenate([moe_w_group[layer], moe_w_expert[layer]], axis=1),
                           ((0, 0), (0, LANES - MOE_GROUPS - MOE_GROUPS * MOE_EPG)))
        c, route, seg_counts = _rms_router(h, norm_ffn[layer], w_router)
        y = _moe_layer(h, route, seg_counts, norm_ffn[layer], w_gate_all, w_up_all, w_down_all, layer * n_exp)
        (h,) = _mm(c, [(ple_w_gate, layer, 0)],
                   [(h, 'tile', None), (y, 'tile', None), (p_all, 'rowtile', layer),
                    (ple_w_proj, 'kcol', layer)],
                   [F32], _epi_ple, n_cols=d, tm=512, tn=1024)
    out = _rmsnorm(h, norm_final, F32)
    return out.reshape(batch, seq, d)
```
